```python
import math
import jax, jax.numpy as jnp
from jax import lax
import numpy as np

D_MODEL = 1024
BATCH = 2
SEQ = 16384
DEPTH = 2

CTX_LEN = 256
GRID_W = 64
EPS = 1e-6
N_MOD = 9
MACARON_WEIGHT = 0.5
D_FF = 2816

S5_WIDTH = 256
S5_GROUP = 16
S5_GROUPS = S5_WIDTH // S5_GROUP
S5_STATE = 64
DT_MIN = 1e-3
DT_MAX = 1e-1
CONV_WIDTH = 256
CONV_K = 31
POOL_WIDTH = 256
POOL_WINDOWS = (2, 4, 8, 16)
POOL_GROUP = POOL_WIDTH // len(POOL_WINDOWS)
MLA_HEADS = 8
MLA_NOPE = 64
MLA_ROPE = 32
MLA_V = 64
MLA_Q_RANK = 256
MLA_KV_RANK = 128
MLA_WIDTH = MLA_HEADS * MLA_V
ROPE_BASE = 10000.0
Q_BLOCK = 128

D_MIX = S5_WIDTH + CONV_WIDTH + POOL_WIDTH + MLA_WIDTH
IN_S5 = 0
IN_CONV = IN_S5 + S5_WIDTH
IN_POOL = IN_CONV + 2 * CONV_WIDTH
IN_CQ = IN_POOL + POOL_WIDTH
IN_CKV = IN_CQ + MLA_Q_RANK
IN_KR = IN_CKV + MLA_KV_RANK
D_IN = IN_KR + MLA_ROPE

kernel_name = "hybrid_s5_conv_pool_mla_prefix_dit"


def rms_norm(x, g):
    xf = x.astype(jnp.float32)
    y = xf * lax.rsqrt(jnp.mean(xf * xf, axis=-1, keepdims=True) + EPS)
    return (y * g.astype(jnp.float32)).astype(x.dtype)


def layer_norm(x, g, b):
    xf = x.astype(jnp.float32)
    mu = jnp.mean(xf, axis=-1, keepdims=True)
    var = jnp.mean(jnp.square(xf - mu), axis=-1, keepdims=True)
    y = (xf - mu) * lax.rsqrt(var + EPS)
    return (y * g.astype(jnp.float32) + b.astype(jnp.float32)).astype(x.dtype)


def ffn_sublayer(x, mod, base, pre_g, post_g, w_gate, w_up, w_down):
    h = rms_norm(x, pre_g) * (1 + mod[:, :, base + 1]) + mod[:, :, base]
    y = (jax.nn.silu(h @ w_gate) * (h @ w_up)) @ w_down
    return x + MACARON_WEIGHT * mod[:, :, base + 2] * rms_norm(y, post_g)


def s5_scan(u, lam_bar, b_bar, h0):
    bu = jnp.einsum('blgh,gph->blgp', u, b_bar)
    if h0 is not None:
        bu = bu.at[:, 0].add(lam_bar * h0)
    a = jnp.broadcast_to(lam_bar, bu.shape)

    def combine(left, right):
        a1, b1 = left
        a2, b2 = right
        return a1 * a2, a2 * b1 + b2

    _, h = lax.associative_scan(combine, (a, bu), axis=1)
    return h


def s5_direction(u_ctx, u_lat, p, d, ctx_out):
    f32 = jnp.float32
    lam = lax.complex(p['s5_lam_re'][d].astype(f32), p['s5_lam_im'][d].astype(f32))
    lam = lax.complex(jnp.minimum(lam.real, -1e-4), lam.imag)
    dt = jnp.exp(p['s5_log_dt'][d].astype(f32))[:, None]
    lam_bar = jnp.exp(lam * dt)
    b = lax.complex(p['s5_b_re'][d].astype(f32), p['s5_b_im'][d].astype(f32))
    b_bar = ((lam_bar - 1.0) / lam)[..., None] * b
    c_mat = lax.complex(p['s5_c_re'][d].astype(f32), p['s5_c_im'][d].astype(f32))
    h_ctx = s5_scan(u_ctx, lam_bar, b_bar, None)
    h_lat = s5_scan(u_lat, lam_bar, b_bar, h_ctx[:, -1])
    y_lat = jnp.einsum('blgp,ghp->blgh', h_lat, c_mat).real
    y_ctx = jnp.einsum('blgp,ghp->blgh', h_ctx, c_mat).real if ctx_out else None
    return y_lat, y_ctx


def s5_mixer(u_ctx, u_lat, p, ctx_out):
    def groups(u):
        return u.astype(jnp.float32).reshape(u.shape[0], u.shape[1], S5_GROUPS, S5_GROUP)

    def rev(t):
        return jnp.flip(t, axis=1)

    uc, ul = groups(u_ctx), groups(u_lat)
    yl_f, yc_f = s5_direction(uc, ul, p, 0, ctx_out)
    yl_b, yc_b = s5_direction(rev(uc), rev(ul), p, 1, ctx_out)

    def finish(y, u):
        y = y.reshape(u.shape) + p['s5_d'].astype(jnp.float32) * u.astype(jnp.float32)
        y = jax.nn.gelu(y).astype(u.dtype)
        return y * jax.nn.sigmoid(y @ p['s5_w_glu'] + p['s5_b_glu'])

    y_lat = finish(yl_f + rev(yl_b), u_lat)
    y_ctx = finish(yc_f + rev(yc_b), u_ctx) if ctx_out else None
    return y_lat, y_ctx


def conv_mixer(z, p):
    h = z[..., :CONV_WIDTH] * jax.nn.sigmoid(z[..., CONV_WIDTH:])
    h = lax.conv_general_dilated(
        h, p['conv_w'][:, None, :], window_strides=(1,),
        padding=[(CONV_K // 2, CONV_K // 2)],
        dimension_numbers=('NWC', 'WIO', 'NWC'),
        feature_group_count=CONV_WIDTH) + p['conv_b']
    return jax.nn.silu(layer_norm(h, p['conv_ln_g'], p['conv_ln_b']))


def pool_mixer(z, p):
    L = z.shape[1]
    zf = z.astype(jnp.float32)
    cs = jnp.pad(jnp.cumsum(zf, axis=1), ((0, 0), (1, 0), (0, 0)))
    t = jnp.arange(L)
    outs = []
    for gi, w in enumerate(POOL_WINDOWS):
        sl = slice(gi * POOL_GROUP, (gi + 1) * POOL_GROUP)
        lo = jnp.maximum(t - (w - 1) // 2, 0)
        hi = jnp.minimum(t + w // 2, L - 1)
        csg = cs[..., sl]
        mean = (csg[:, hi + 1] - csg[:, lo]) / (hi - lo + 1).astype(jnp.float32)[:, None]
        outs.append(jnp.einsum('blc,cd->bld', mean - zf[..., sl], p['pool_w'][gi].astype(jnp.float32)))
    y = jnp.concatenate(outs, axis=-1) * p['pool_scale'].astype(jnp.float32)
    return y.astype(z.dtype)


def grid_rope(n_lat):
    rows = n_lat // GRID_W
    row = jnp.repeat(jnp.arange(rows), GRID_W).astype(jnp.float32)
    col = (jnp.arange(rows * GRID_W) % GRID_W).astype(jnp.float32)
    axis_dim = MLA_ROPE // 2
    inv = ROPE_BASE ** (-jnp.arange(0, axis_dim, 2, dtype=jnp.float32) / axis_dim)
    ang_r = (row[:, None] * inv)[:, None, :]
    ang_c = (col[:, None] * inv)[:, None, :]
    return jnp.cos(ang_r), jnp.sin(ang_r), jnp.cos(ang_c), jnp.sin(ang_c)


def rope_2d(x, rope):
    cos_r, sin_r, cos_c, sin_c = rope
    xf = x.astype(jnp.float32)

    def rot(xh, cos, sin):
        x1, x2 = jnp.split(xh, 2, axis=-1)
        return jnp.concatenate([x1 * cos - x2 * sin, x2 * cos + x1 * sin], axis=-1)

    xr, xc = jnp.split(xf, 2, axis=-1)
    return jnp.concatenate([rot(xr, cos_r, sin_r), rot(xc, cos_c, sin_c)], axis=-1).astype(x.dtype)


def mla_keys_values(ckv, k_rope, p, rope):
    B, L, _ = ckv.shape
    kv = (rms_norm(ckv, p['mla_kv_norm']) @ p['mla_w_ukv']).reshape(B, L, MLA_HEADS, MLA_NOPE + MLA_V)
    k_nope, v = kv[..., :MLA_NOPE], kv[..., MLA_NOPE:]
    k_rope = k_rope[:, :, None, :]
    if rope is not None:
        k_rope = rope_2d(k_rope, rope)
    k_rope = jnp.broadcast_to(k_rope, (B, L, MLA_HEADS, MLA_ROPE))
    return jnp.concatenate([k_nope, k_rope], axis=-1), v


def mla_queries(cq, p, rope):
    B, L, _ = cq.shape
    q = (rms_norm(cq, p['mla_q_norm']) @ p['mla_w_uq']).reshape(B, L, MLA_HEADS, MLA_NOPE + MLA_ROPE)
    if rope is None:
        return q
    return jnp.concatenate([q[..., :MLA_NOPE], rope_2d(q[..., MLA_NOPE:], rope)], axis=-1)


def block_attention(q, k, v):
    B, Lq, H, Dk = q.shape
    nb = Lq // Q_BLOCK
    scale = Dk ** -0.5
    qb = jnp.moveaxis(q.reshape(B, nb, Q_BLOCK, H, Dk), 1, 0)

    def one_block(qi):
        s = jnp.einsum('bqhd,bkhd->bhqk', qi, k, preferred_element_type=jnp.float32) * scale
        pr = jax.nn.softmax(s, axis=-1).astype(v.dtype)
        return jnp.einsum('bhqk,bkhd->bqhd', pr, v)

    o = lax.map(one_block, qb)
    return jnp.moveaxis(o, 0, 1).reshape(B, Lq, H * v.shape[-1])


def token_mixing(x_lat, x_ctx, mod_lat, mod_ctx, rope, p, ctx_out):
    base = 3

    def pre(x, mod):
        return rms_norm(x, p['pre'][1]) * (1 + mod[:, :, base + 1]) + mod[:, :, base]

    h_lat = pre(x_lat, mod_lat)
    h_ctx = pre(x_ctx, mod_ctx)
    z_lat = h_lat @ p['w_in']
    if ctx_out:
        z_ctx = h_ctx @ p['w_in']
        u_ctx, kvz_ctx = z_ctx[..., IN_S5:IN_CONV], z_ctx[..., IN_CKV:D_IN]
    else:
        u_ctx = h_ctx @ p['w_in'][:, IN_S5:IN_CONV]
        kvz_ctx = h_ctx @ p['w_in'][:, IN_CKV:D_IN]

    s5_lat, s5_ctx = s5_mixer(u_ctx, z_lat[..., IN_S5:IN_CONV], p, ctx_out)
    conv_lat = conv_mixer(z_lat[..., IN_CONV:IN_POOL], p)
    pool_lat = pool_mixer(z_lat[..., IN_POOL:IN_CQ], p)
    k_ctx, v_ctx = mla_keys_values(kvz_ctx[..., :MLA_KV_RANK], kvz_ctx[..., MLA_KV_RANK:], p, None)
    k_lat, v_lat = mla_keys_values(z_lat[..., IN_CKV:IN_KR], z_lat[..., IN_KR:D_IN], p, rope)
    q_lat = mla_queries(z_lat[..., IN_CQ:IN_CKV], p, rope)
    att_lat = block_attention(q_lat, jnp.concatenate([k_lat, k_ctx], axis=1),
                              jnp.concatenate([v_lat, v_ctx], axis=1))

    y_lat = jnp.concatenate([s5_lat, conv_lat, pool_lat, att_lat], axis=-1) @ p['w_out']
    x_lat = x_lat + mod_lat[:, :, base + 2] * rms_norm(y_lat, p['post'][1])
    if not ctx_out:
        return x_lat, None

    conv_ctx = conv_mixer(z_ctx[..., IN_CONV:IN_POOL], p)
    pool_ctx = pool_mixer(z_ctx[..., IN_POOL:IN_CQ], p)
    q_ctx = mla_queries(z_ctx[..., IN_CQ:IN_CKV], p, None)
    att_ctx = block_attention(q_ctx, k_ctx, v_ctx)
    y_ctx = jnp.concatenate([s5_ctx, conv_ctx, pool_ctx, att_ctx], axis=-1) @ p['w_out']
    x_ctx = x_ctx + mod_ctx[:, :, base + 2] * rms_norm(y_ctx, p['post'][1])
    return x_lat, x_ctx


def setup_inputs(seed: int = 0) -> dict:
    key = jax.random.key(seed)
    ks = iter(jax.random.split(key, 40))
    f32 = jnp.float32

    def nrm(shape, std):
        return std * jax.random.normal(next(ks), shape, f32)

    def gain(shape):
        return 1.0 + nrm(shape, 0.02)

    D, F = D_MODEL, D_FF
    G, P, H = S5_GROUPS, S5_STATE, S5_GROUP
    return {
        "x": nrm((BATCH, SEQ, D), 1.0),
        "c": nrm((BATCH, D), 1.0),
        "ctx": nrm((BATCH, CTX_LEN, D), 1.0),
        "c_ctx": nrm((D,), 1.0),
        "w_ada": nrm((DEPTH, D, N_MOD * D), 0.5 * D ** -0.5),
        "b_ada": nrm((DEPTH, N_MOD * D), 0.02),
        "norm_pre": gain((DEPTH, 3, D)),
        "norm_post": gain((DEPTH, 3, D)),
        "ffn_w_gate": nrm((DEPTH, 2, D, F), D ** -0.5),
        "ffn_w_up": nrm((DEPTH, 2, D, F), D ** -0.5),
        "ffn_w_down": nrm((DEPTH, 2, F, D), F ** -0.5),
        "w_in": nrm((DEPTH, D, D_IN), D ** -0.5),
        "w_out": nrm((DEPTH, D_MIX, D), D_MIX ** -0.5),
        "s5_lam_re": -0.5 + nrm((DEPTH, 2, G, P), 0.01),
        "s5_lam_im": math.pi * jnp.arange(P, dtype=f32) + nrm((DEPTH, 2, G, P), 0.01),
        "s5_log_dt": jax.random.uniform(next(ks), (DEPTH, 2, G), f32, math.log(DT_MIN), math.log(DT_MAX)),
        "s5_b_re": nrm((DEPTH, 2, G, P, H), (2 * H) ** -0.5),
        "s5_b_im": nrm((DEPTH, 2, G, P, H), (2 * H) ** -0.5),
        "s5_c_re": nrm((DEPTH, 2, G, H, P), 0.5),
        "s5_c_im": nrm((DEPTH, 2, G, H, P), 0.5),
        "s5_d": nrm((DEPTH, S5_WIDTH), 1.0),
        "s5_w_glu": nrm((DEPTH, S5_WIDTH, S5_WIDTH), S5_WIDTH ** -0.5),
        "s5_b_glu": nrm((DEPTH, S5_WIDTH), 0.02),
        "conv_w": nrm((DEPTH, CONV_K, CONV_WIDTH), CONV_K ** -0.5),
        "conv_b": nrm((DEPTH, CONV_WIDTH), 0.02),
        "conv_ln_g": gain((DEPTH, CONV_WIDTH)),
        "conv_ln_b": nrm((DEPTH, CONV_WIDTH), 0.02),
        "pool_w": nrm((DEPTH, len(POOL_WINDOWS), POOL_GROUP, POOL_GROUP), POOL_GROUP ** -0.5),
        "pool_scale": gain((DEPTH, POOL_WIDTH)),
        "mla_q_norm": gain((DEPTH, MLA_Q_RANK)),
        "mla_w_uq": nrm((DEPTH, MLA_Q_RANK, MLA_HEADS * (MLA_NOPE + MLA_ROPE)), MLA_Q_RANK ** -0.5),
        "mla_kv_norm": gain((DEPTH, MLA_KV_RANK)),
        "mla_w_ukv": nrm((DEPTH, MLA_KV_RANK, MLA_HEADS * (MLA_NOPE + MLA_V)), MLA_KV_RANK ** -0.5),
    }


def reference(x, c, ctx, c_ctx, w_ada, b_ada, norm_pre, norm_post, ffn_w_gate, ffn_w_up, ffn_w_down,
              w_in, w_out, s5_lam_re, s5_lam_im, s5_log_dt, s5_b_re, s5_b_im, s5_c_re, s5_c_im, s5_d,
              s5_w_glu, s5_b_glu, conv_w, conv_b, conv_ln_g, conv_ln_b, pool_w, pool_scale,
              mla_q_norm, mla_w_uq, mla_kv_norm, mla_w_ukv):
    batch, n_lat, _ = x.shape
    rope = grid_rope(n_lat)
    x_lat, x_ctx = x, ctx
    for l in range(DEPTH):
        last = l == DEPTH - 1
        p = {
            'pre': norm_pre[l], 'post': norm_post[l], 'w_in': w_in[l], 'w_out': w_out[l],
            's5_lam_re': s5_lam_re[l], 's5_lam_im': s5_lam_im[l], 's5_log_dt': s5_log_dt[l],
            's5_b_re': s5_b_re[l], 's5_b_im': s5_b_im[l], 's5_c_re': s5_c_re[l], 's5_c_im': s5_c_im[l],
            's5_d': s5_d[l], 's5_w_glu': s5_w_glu[l], 's5_b_glu': s5_b_glu[l],
            'conv_w': conv_w[l], 'conv_b': conv_b[l], 'conv_ln_g': conv_ln_g[l], 'conv_ln_b': conv_ln_b[l],
            'pool_w': pool_w[l], 'pool_scale': pool_scale[l],
            'mla_q_norm': mla_q_norm[l], 'mla_w_uq': mla_w_uq[l],
            'mla_kv_norm': mla_kv_norm[l], 'mla_w_ukv': mla_w_ukv[l],
        }
        mod_lat = (jax.nn.silu(c) @ w_ada[l] + b_ada[l]).reshape(batch, 1, N_MOD, D_MODEL)
        mod_ctx = (jax.nn.silu(c_ctx) @ w_ada[l] + b_ada[l]).reshape(1, 1, N_MOD, D_MODEL)
        x_lat = ffn_sublayer(x_lat, mod_lat, 0, norm_pre[l, 0], norm_post[l, 0],
                             ffn_w_gate[l, 0], ffn_w_up[l, 0], ffn_w_down[l, 0])
        x_ctx = ffn_sublayer(x_ctx, mod_ctx, 0, norm_pre[l, 0], norm_post[l, 0],
                             ffn_w_gate[l, 0], ffn_w_up[l, 0], ffn_w_down[l, 0])
        x_lat, x_ctx = token_mixing(x_lat, x_ctx, mod_lat, mod_ctx, rope, p, not last)
        x_lat = ffn_sublayer(x_lat, mod_lat, 6, norm_pre[l, 2], norm_post[l, 2],
                             ffn_w_gate[l, 1], ffn_w_up[l, 1], ffn_w_down[l, 1])
        if not last:
            x_ctx = ffn_sublayer(x_ctx, mod_ctx, 6, norm_pre[l, 2], norm_post[l, 2],
                                 ffn_w_gate[l, 1], ffn_w_up[l, 1], ffn_w_down[l, 1])
    return x_lat
```

```python
import functools
import math

import jax
import jax.numpy as jnp
from jax import lax
from jax.experimental import pallas as pl
from jax.experimental.pallas import tpu as pltpu

F32 = jnp.float32
BF16 = jnp.bfloat16

D_MODEL = 1024
GRID_W = 64
EPS = 1e-6
N_MOD = 9
MACARON_WEIGHT = 0.5
D_FF = 2816
S5_WIDTH = 256
S5_GROUP = 16
S5_GROUPS = 16
S5_STATE = 64
CONV_WIDTH = 256
CONV_K = 31
POOL_WIDTH = 256
POOL_WINDOWS = (2, 4, 8, 16)
POOL_GROUP = 64
MLA_HEADS = 8
MLA_NOPE = 64
MLA_ROPE = 32
MLA_V = 64
MLA_Q_RANK = 256
MLA_KV_RANK = 128
MLA_WIDTH = MLA_HEADS * MLA_V
ROPE_BASE = 10000.0
D_MIX = S5_WIDTH + CONV_WIDTH + POOL_WIDTH + MLA_WIDTH
IN_S5 = 0
IN_CONV = IN_S5 + S5_WIDTH
IN_POOL = IN_CONV + 2 * CONV_WIDTH
IN_CQ = IN_POOL + POOL_WIDTH
IN_CKV = IN_CQ + MLA_Q_RANK
IN_KR = IN_CKV + MLA_KV_RANK
D_IN = IN_KR + MLA_ROPE

LANES = 128
HEAD_PAD = 128
S5_CHUNK = 16
S5_FLAT = S5_CHUNK * S5_GROUP
HALO = 16
D_IN_EXT = IN_KR + 2 * HEAD_PAD
FF_CHUNKS = ((0, 1024), (1024, 2048), (2048, 2816))
VMEM_LIMIT = 56 * 1024 * 1024


def _cparams(*sem):
    return pltpu.CompilerParams(dimension_semantics=sem, vmem_limit_bytes=VMEM_LIMIT)


def _const_spec(shape):
    nd = len(shape)
    return pl.BlockSpec(shape, lambda *_: (0,) * nd, pipeline_mode=pl.Buffered(1))


def _rms(x, g):
    return x * lax.rsqrt(jnp.mean(x * x, axis=-1, keepdims=True) + EPS) * g


def _dot(a, b):
    return jnp.dot(a, b, preferred_element_type=F32)


def _mod_kernel(c_ref, w_ref, b_ref, o_ref):
    c = c_ref[...]
    h = (c * jax.nn.sigmoid(c)).astype(BF16)
    o_ref[0] = _dot(h, w_ref[0].astype(BF16)) + b_ref[0]


def _modulation(c_rows, w_ada, b_ada):
    depth = w_ada.shape[0]
    d = D_MODEL
    return pl.pallas_call(
        _mod_kernel,
        grid=(depth, N_MOD),
        in_specs=[
            pl.BlockSpec((8, d), lambda l, j: (0, 0)),
            pl.BlockSpec((1, d, d), lambda l, j: (l, 0, j)),
            pl.BlockSpec((1, 1, d), lambda l, j: (l, 0, j)),
        ],
        out_specs=pl.BlockSpec((1, 8, d), lambda l, j: (l, 0, j)),
        out_shape=jax.ShapeDtypeStruct((depth, 8, N_MOD * d), F32),
        compiler_params=_cparams("arbitrary", "arbitrary"),
        name="adaln_mod",
    )(c_rows, w_ada, b_ada.reshape(depth, 1, N_MOD * d))


def _mod_slices(mod_ref, base):
    d = D_MODEL
    return (mod_ref[0, :, base * d:(base + 1) * d], mod_ref[0, :, (base + 1) * d:(base + 2) * d],
            mod_ref[0, :, (base + 2) * d:(base + 3) * d])


def _ffn_kernel(x_ref, mod_ref, pre_ref, post_ref, wg_ref, wu_ref, wd_ref, o_ref, *, base):
    x = x_ref[0]
    shift, scale, gate = _mod_slices(mod_ref, base)
    h = (_rms(x, pre_ref[...]) * (1.0 + scale) + shift).astype(BF16)
    acc = None
    for lo, hi in FF_CHUNKS:
        g = _dot(h, wg_ref[:, lo:hi])
        u = _dot(h, wu_ref[:, lo:hi])
        a = (g * jax.nn.sigmoid(g) * u).astype(BF16)
        part = _dot(a, wd_ref[lo:hi, :])
        acc = part if acc is None else acc + part
    o_ref[0] = x + MACARON_WEIGHT * gate * _rms(acc, post_ref[...])


def _ffn(x, mod_rows, base, pre_g, post_g, wg, wu, wd, tm):
    b, s, d = x.shape
    per_batch = mod_rows.shape[0] > 1
    return pl.pallas_call(
        functools.partial(_ffn_kernel, base=base),
        grid=(b, s // tm),
        in_specs=[
            pl.BlockSpec((1, tm, d), lambda i, j: (i, j, 0)),
            pl.BlockSpec((1, 1, N_MOD * d), (lambda i, j: (i, 0, 0)) if per_batch else (lambda i, j: (0, 0, 0))),
            _const_spec((1, d)), _const_spec((1, d)),
            _const_spec((d, D_FF)), _const_spec((d, D_FF)), _const_spec((D_FF, d)),
        ],
        out_specs=pl.BlockSpec((1, tm, d), lambda i, j: (i, j, 0)),
        out_shape=jax.ShapeDtypeStruct(x.shape, F32),
        compiler_params=_cparams("arbitrary", "arbitrary"),
        name="ffn_half_step",
    )(x, mod_rows, pre_g.reshape(1, d), post_g.reshape(1, d), wg, wu, wd)


def _inproj_kernel(x_ref, mod_ref, pre_ref, win_ref, qn_ref, kvn_ref, wq_ref, wk_ref, wv_ref, place_ref,
                   cos_ref, sin_ref, u_ref, zc_ref, zp_ref, q_ref, k_ref, v_ref):
    x = x_ref[0]
    shift, scale, _ = _mod_slices(mod_ref, 3)
    h = (_rms(x, pre_ref[...]) * (1.0 + scale) + shift).astype(BF16)
    z = _dot(h, win_ref[...])
    u_ref[0] = z[:, IN_S5:IN_CONV]
    zc_ref[0] = z[:, IN_CONV:IN_POOL]
    zp_ref[0] = z[:, IN_POOL:IN_CQ]
    cos = cos_ref[...]
    sin = sin_ref[...]
    cos_h = jnp.concatenate([cos] * MLA_HEADS, axis=1)
    sin_h = jnp.concatenate([sin] * MLA_HEADS, axis=1)
    cqn = _rms(z[:, IN_CQ:IN_CKV], qn_ref[...]).astype(BF16)
    qq = _dot(cqn, wq_ref[...])
    hw = MLA_HEADS * HEAD_PAD
    q = qq[:, :hw] * cos_h + qq[:, hw:] * sin_h
    q_ref[0] = (q * ((MLA_NOPE + MLA_ROPE) ** -0.5)).astype(BF16)
    ckvn = _rms(z[:, IN_CKV:IN_KR], kvn_ref[...]).astype(BF16)
    kr = z[:, IN_KR:IN_KR + HEAD_PAD] * cos + z[:, IN_KR + HEAD_PAD:IN_KR + 2 * HEAD_PAD] * sin
    k = _dot(ckvn, wk_ref[...]) + _dot(kr.astype(BF16), place_ref[...])
    k_ref[0] = k.astype(BF16)
    v_ref[0] = _dot(ckvn, wv_ref[...]).astype(BF16)


def _inproj(x, mod_rows, pre_g, lw, cos_t, sin_t, tm):
    b, s, d = x.shape
    per_batch = mod_rows.shape[0] > 1
    hw = MLA_HEADS * HEAD_PAD
    row = lambda w: pl.BlockSpec((1, tm, w), lambda i, j: (i, j, 0))
    return pl.pallas_call(
        _inproj_kernel,
        grid=(b, s // tm),
        in_specs=[
            row(d),
            pl.BlockSpec((1, 1, N_MOD * d), (lambda i, j: (i, 0, 0)) if per_batch else (lambda i, j: (0, 0, 0))),
            _const_spec((1, d)),
            _const_spec((d, D_IN_EXT)),
            _const_spec((1, MLA_Q_RANK)), _const_spec((1, MLA_KV_RANK)),
            _const_spec((MLA_Q_RANK, 2 * hw)), _const_spec((MLA_KV_RANK, hw)),
            _const_spec((MLA_KV_RANK, MLA_WIDTH)), _const_spec((HEAD_PAD, hw)),
            pl.BlockSpec((tm, HEAD_PAD), lambda i, j: (j, 0)),
            pl.BlockSpec((tm, HEAD_PAD), lambda i, j: (j, 0)),
        ],
        out_specs=[row(S5_WIDTH), row(2 * CONV_WIDTH), row(POOL_WIDTH), row(hw), row(hw), row(MLA_WIDTH)],
        out_shape=[
            jax.ShapeDtypeStruct((b, s, S5_WIDTH), F32),
            jax.ShapeDtypeStruct((b, s, 2 * CONV_WIDTH), F32),
            jax.ShapeDtypeStruct((b, s, POOL_WIDTH), F32),
            jax.ShapeDtypeStruct((b, s, hw), BF16),
            jax.ShapeDtypeStruct((b, s, hw), BF16),
            jax.ShapeDtypeStruct((b, s, MLA_WIDTH), BF16),
        ],
        compiler_params=_cparams("arbitrary", "arbitrary"),
        name="mixer_in_proj",
    )(x, mod_rows, pre_g.reshape(1, d), lw["w_in"], lw["q_norm"], lw["kv_norm"], lw["wq"], lw["wk"], lw["wv"],
      lw["place"], cos_t, sin_t)


def _s5_state_kernel(u_ref, m_ref, s_ref):
    s_ref[0, 0] = _dot(u_ref[0, 0], m_ref[0])


def _s5_chunk_states(u_flat, m_state):
    b, g, ct, w = u_flat.shape
    return pl.pallas_call(
        _s5_state_kernel,
        grid=(b, g),
        in_specs=[pl.BlockSpec((1, 1, ct, w), lambda i, j: (i, j, 0, 0)),
                  pl.BlockSpec((1, w, w), lambda i, j: (j, 0, 0))],
        out_specs=pl.BlockSpec((1, 1, ct, w), lambda i, j: (i, j, 0, 0)),
        out_shape=jax.ShapeDtypeStruct((b, g, ct, w), F32),
        compiler_params=_cparams("arbitrary", "arbitrary"),
        name="s5_chunk_states",
    )(u_flat, m_state)


def _s5_scan_kernel(sc_ref, sf_ref, sb_ref, a_ref, hc_ref, hf_ref, hb_ref, st_ref):
    half = LANES // 2
    a_re = a_ref[:, :LANES]
    a_im = a_ref[:, LANES:]
    fwd_lane = lax.broadcasted_iota(jnp.int32, a_re.shape, 1) < half

    def swap(v):
        return pltpu.roll(v, half, 1)

    def run(n, load_f, load_b, store_f, store_b, carry):
        def body(i, c):
            h_re, h_im = c
            r = n - 1 - i
            sf = load_f(i)
            sb = load_b(r)
            store_f(i, jnp.where(fwd_lane, h_re, swap(h_im)))
            store_b(r, jnp.where(fwd_lane, swap(h_re), h_im))
            s_re = jnp.where(fwd_lane, sf, swap(sb))
            s_im = jnp.where(fwd_lane, swap(sf), sb)
            return (a_re * h_re - a_im * h_im + s_re, a_re * h_im + a_im * h_re + s_im)
        return lax.fori_loop(0, n, body, carry)

    @pl.when(pl.program_id(1) == 0)
    def _():
        def st_f(i, v):
            hc_ref[0, i, :, 0:LANES] = v

        def st_b(i, v):
            hc_ref[0, i, :, LANES:2 * LANES] = v

        zero = jnp.zeros(a_re.shape, F32)
        c = run(sc_ref.shape[1], lambda i: sc_ref[0, i, :, 0:LANES], lambda i: sc_ref[0, i, :, LANES:2 * LANES],
                st_f, st_b, (zero, zero))
        st_ref[0] = c[0]
        st_ref[1] = c[1]

    def st_f(i, v):
        hf_ref[0, i] = v

    def st_b(i, v):
        hb_ref[0, i] = v

    c = run(sf_ref.shape[1], lambda i: sf_ref[0, i], lambda i: sb_ref[0, i], st_f, st_b, (st_ref[0], st_ref[1]))
    st_ref[0] = c[0]
    st_ref[1] = c[1]


def _s5_scan(s_ctx, s_lat, a16):
    b, nc, g, w = s_ctx.shape
    c = s_lat.shape[1]
    cb = min(128, c)
    nb = c // cb
    return pl.pallas_call(
        _s5_scan_kernel,
        grid=(b, nb),
        in_specs=[pl.BlockSpec((1, nc, g, w), lambda i, j: (i, 0, 0, 0)),
                  pl.BlockSpec((1, cb, g, LANES), lambda i, j: (i, j, 0, 0)),
                  pl.BlockSpec((1, cb, g, LANES), lambda i, j: (i, nb - 1 - j, 0, 1)),
                  pl.BlockSpec((g, w), lambda i, j: (0, 0))],
        out_specs=[pl.BlockSpec((1, nc, g, w), lambda i, j: (i, 0, 0, 0)),
                   pl.BlockSpec((1, cb, g, LANES), lambda i, j: (i, j, 0, 0)),
                   pl.BlockSpec((1, cb, g, LANES), lambda i, j: (i, nb - 1 - j, 0, 0))],
        out_shape=[jax.ShapeDtypeStruct((b, nc, g, w), F32), jax.ShapeDtypeStruct((b, c, g, LANES), F32),
                   jax.ShapeDtypeStruct((b, c, g, LANES), F32)],
        scratch_shapes=[pltpu.VMEM((2, g, LANES), F32)],
        compiler_params=_cparams("arbitrary", "arbitrary"),
        name="s5_chunk_scan",
    )(s_ctx, s_lat, s_lat, a16)


def _s5_out_kernel(u_ref, h_ref, mi_ref, mo_ref, y_ref):
    y_ref[0, 0] = _dot(u_ref[0, 0], mi_ref[0]) + _dot(h_ref[0, 0], mo_ref[0])


def _s5_outputs(u_flat, h_in, m_intra, m_out):
    b, g, ct, w = u_flat.shape
    blk = pl.BlockSpec((1, 1, ct, w), lambda i, j: (i, j, 0, 0))
    mat = pl.BlockSpec((1, w, w), lambda i, j: (j, 0, 0))
    return pl.pallas_call(
        _s5_out_kernel,
        grid=(b, g),
        in_specs=[blk, blk, mat, mat],
        out_specs=blk,
        out_shape=jax.ShapeDtypeStruct((b, g, ct, w), F32),
        compiler_params=_cparams("arbitrary", "arbitrary"),
        name="s5_chunk_outputs",
    )(u_flat, h_in, m_intra, m_out)


def _s5_matrices(lam_re, lam_im, log_dt, b_re, b_im, c_re, c_im):
    hp = lax.Precision.HIGHEST
    t = S5_CHUNK
    g, p, h = S5_GROUPS, S5_STATE, S5_GROUP
    lam = lax.complex(jnp.minimum(lam_re, -1e-4), lam_im)
    dt = jnp.exp(log_dt)[..., None]
    steps = jnp.arange(t + 1, dtype=F32)[:, None, None, None]
    apow = jnp.exp(lam[None] * dt[None] * steps)
    bb = ((apow[1] - 1.0) / lam)[..., None] * lax.complex(b_re, b_im)
    cc = lax.complex(c_re, c_im)
    taps = jnp.einsum('dghp,jdgp,dgpk->djghk', cc, apow[:t], bb, precision=hp).real
    s_idx = jnp.arange(t)[:, None]
    t_idx = jnp.arange(t)[None, :]
    kf = taps[0][jnp.clip(t_idx - s_idx, 0, t - 1)] * (t_idx >= s_idx)[:, :, None, None, None].astype(F32)
    kb = taps[1][jnp.clip(s_idx - t_idx, 0, t - 1)] * (s_idx >= t_idx)[:, :, None, None, None].astype(F32)
    m_intra = (kf + kb).transpose(2, 0, 4, 1, 3).reshape(g, t * h, t * h)
    wf = apow[:t][::-1, 0][:, :, :, None] * bb[0][None]
    wb = apow[:t, 1][:, :, :, None] * bb[1][None]
    to_rows = lambda w: w.transpose(1, 0, 3, 2).reshape(g, t * h, p)
    m_state = jnp.concatenate([to_rows(wf.real), to_rows(wf.imag), to_rows(wb.real), to_rows(wb.imag)], axis=-1)
    of = cc[0][None] * apow[1:, 0][:, :, None, :]
    ob = cc[1][None] * apow[1:, 1][::-1][:, :, None, :]
    to_cols = lambda w: w.transpose(1, 3, 0, 2).reshape(g, p, t * h)
    m_out = jnp.concatenate([to_cols(of.real), -to_cols(of.imag), to_cols(ob.real), -to_cols(ob.imag)], axis=1)
    a_t = apow[t]
    a16 = jnp.concatenate([a_t[0].real, a_t[1].real, a_t[0].imag, a_t[1].imag], axis=-1)
    return m_intra.astype(BF16), m_state.astype(BF16), m_out.astype(BF16), a16


def _s5_mixer(u_ctx, u_lat, mats):
    m_intra, m_state, m_out, a16 = mats
    b, lc, _ = u_ctx.shape
    g, h, t = S5_GROUPS, S5_GROUP, S5_CHUNK

    def flat(u):
        n = u.shape[1] // t
        return u.astype(BF16).reshape(b, n, t, g, h).transpose(0, 3, 1, 2, 4).reshape(b, g, n, t * h)

    u_flat = jnp.concatenate([flat(u_ctx), flat(u_lat)], axis=2)
    ct = u_flat.shape[2]
    nc = lc // t
    s_loc = _s5_chunk_states(u_flat, m_state).transpose(0, 2, 1, 3)
    h_ctx, h_f, h_b = _s5_scan(s_loc[:, :nc], s_loc[:, nc:], a16)
    h_in = jnp.concatenate([h_ctx, jnp.concatenate([h_f, h_b], axis=-1)], axis=1)
    y = _s5_outputs(u_flat, h_in.transpose(0, 2, 1, 3).astype(BF16), m_intra, m_out)
    y = y.reshape(b, g, ct, t, h).transpose(0, 2, 3, 1, 4).reshape(b, ct * t, g * h)
    return y[:, :lc], y[:, lc:]


def _local_kernel(zc_ref, zcp_ref, zcn_ref, zp_ref, zpp_ref, zpn_ref, cw_ref, cb_ref, lg_ref, lb_ref, pw_ref, ps_ref,
                  conv_ref, pool_ref, ce_ref, pe_ref, *, seq_len):
    tm = zc_ref.shape[1]
    j = pl.program_id(1)
    first = j == 0
    last = j == pl.num_programs(1) - 1

    def glu(z):
        return z[:, :CONV_WIDTH] * jax.nn.sigmoid(z[:, CONV_WIDTH:])

    ce_ref[0:HALO] = jnp.where(first, 0.0, glu(zcp_ref[0]))
    ce_ref[HALO:HALO + tm] = glu(zc_ref[0])
    ce_ref[HALO + tm:] = jnp.where(last, 0.0, glu(zcn_ref[0]))
    pe_ref[0:HALO] = jnp.where(first, 0.0, zpp_ref[0])
    pe_ref[HALO:HALO + tm] = zp_ref[0]
    pe_ref[HALO + tm:] = jnp.where(last, 0.0, zpn_ref[0])

    acc = None
    for k in range(CONV_K):
        term = ce_ref[pl.ds(HALO + k - CONV_K // 2, tm), :] * cw_ref[k:k + 1, :]
        acc = term if acc is None else acc + term
    acc = acc + cb_ref[...]
    mu = jnp.mean(acc, axis=-1, keepdims=True)
    xc = acc - mu
    var = jnp.mean(xc * xc, axis=-1, keepdims=True)
    y = xc * lax.rsqrt(var + EPS) * lg_ref[...] + lb_ref[...]
    conv_ref[0] = (y * jax.nn.sigmoid(y)).astype(BF16)

    def tap(dlt):
        return pe_ref[pl.ds(HALO + dlt, tm), :]

    z = tap(0)
    sums = {}
    run = z + tap(1)
    sums[2] = run
    run = run + tap(-1) + tap(2)
    sums[4] = run
    run = run + tap(-3) + tap(-2) + tap(3) + tap(4)
    sums[8] = run
    for dlt in (-7, -6, -5, -4, 5, 6, 7, 8):
        run = run + tap(dlt)
    sums[16] = run
    pos = j * tm + lax.broadcasted_iota(jnp.int32, (tm, 1), 0)
    lane = lax.broadcasted_iota(jnp.int32, (tm, POOL_WIDTH), 1)
    mean = None
    for gi, w in enumerate(POOL_WINDOWS):
        lo = jnp.maximum(pos - (w - 1) // 2, 0)
        hi = jnp.minimum(pos + w // 2, seq_len - 1)
        m = sums[w] / (hi - lo + 1).astype(F32)
        mean = m if mean is None else jnp.where(lane >= gi * POOL_GROUP, m, mean)
    d = (mean - z).astype(BF16)
    pool_ref[0] = (_dot(d, pw_ref[...]) * ps_ref[...]).astype(BF16)


def _local_mixers(zc, zp, lw, tm):
    b, s, _ = zc.shape
    nh = tm // HALO
    nblk = s // HALO
    main = lambda w: pl.BlockSpec((1, tm, w), lambda i, j: (i, j, 0))
    prev = lambda w: pl.BlockSpec((1, HALO, w), lambda i, j: (i, jnp.maximum(j * nh - 1, 0), 0))
    nxt = lambda w: pl.BlockSpec((1, HALO, w), lambda i, j: (i, jnp.minimum((j + 1) * nh, nblk - 1), 0))
    cw2, pw = 2 * CONV_WIDTH, POOL_WIDTH
    return pl.pallas_call(
        functools.partial(_local_kernel, seq_len=s),
        grid=(b, s // tm),
        in_specs=[main(cw2), prev(cw2), nxt(cw2), main(pw), prev(pw), nxt(pw),
                  _const_spec((32, CONV_WIDTH)), _const_spec((1, CONV_WIDTH)), _const_spec((1, CONV_WIDTH)),
                  _const_spec((1, CONV_WIDTH)), _const_spec((pw, pw)), _const_spec((1, pw))],
        out_specs=[main(CONV_WIDTH), main(pw)],
        out_shape=[jax.ShapeDtypeStruct((b, s, CONV_WIDTH), BF16), jax.ShapeDtypeStruct((b, s, pw), BF16)],
        scratch_shapes=[pltpu.VMEM((tm + 2 * HALO, CONV_WIDTH), F32), pltpu.VMEM((tm + 2 * HALO, pw), F32)],
        compiler_params=_cparams("arbitrary", "arbitrary"),
        name="conv_pool",
    )(zc, zc, zc, zp, zp, zp, lw["conv_w"], lw["conv_b"], lw["conv_ln_g"], lw["conv_ln_b"], lw["pool_w"],
      lw["pool_scale"])


def _attn_kernel(*refs, tk, n_src):
    q_ref = refs[0]
    kv_refs = refs[1:1 + 2 * n_src]
    o_ref = refs[1 + 2 * n_src]
    tq = q_ref.shape[1]
    outs = []
    for hh in range(2):
        q = q_ref[0, :, hh * HEAD_PAD:(hh + 1) * HEAD_PAD]

        def step(k, v, carry):
            m, l, acc = carry
            s = lax.dot_general(q, k, (((1,), (1,)), ((), ())), preferred_element_type=F32)
            m_new = jnp.maximum(m, jnp.max(s, axis=1, keepdims=True))
            p = jnp.exp(s - m_new)
            alpha = jnp.exp(m - m_new)
            l = alpha * l + jnp.sum(p, axis=1, keepdims=True)
            acc = alpha * acc + _dot(p.astype(BF16), v)
            return m_new, l, acc

        carry = (jnp.full((tq, 1), -1e30, F32), jnp.zeros((tq, 1), F32), jnp.zeros((tq, 2 * MLA_V), F32))
        for si in range(n_src):
            k_ref, v_ref = kv_refs[2 * si], kv_refs[2 * si + 1]
            n_keys = k_ref.shape[1]
            t = min(tk, n_keys)

            def body(i, c, k_ref=k_ref, v_ref=v_ref, t=t):
                off = pl.multiple_of(i * t, t)
                return step(k_ref[0, pl.ds(off, t), hh * HEAD_PAD:(hh + 1) * HEAD_PAD], v_ref[0, pl.ds(off, t), :], c)

            carry = lax.fori_loop(0, n_keys // t, body, carry)
        _, l, acc = carry
        outs.append(acc / l)
    lane = lax.broadcasted_iota(jnp.int32, (tq, 2 * MLA_V), 1)
    o_ref[0] = jnp.where(lane < MLA_V, outs[0], outs[1]).astype(BF16)


def _attention(q, kv_sources, tq, tk):
    b, lq, _ = q.shape
    n_src = len(kv_sources)
    in_specs = [pl.BlockSpec((1, tq, 2 * HEAD_PAD), lambda i, p, j: (i, j, p))]
    args = [q]
    for k, v in kv_sources:
        lk = k.shape[1]
        in_specs.append(pl.BlockSpec((1, lk, 2 * HEAD_PAD), lambda i, p, j: (i, 0, p)))
        in_specs.append(pl.BlockSpec((1, lk, 2 * MLA_V), lambda i, p, j: (i, 0, p)))
        args += [k, v]
    return pl.pallas_call(
        functools.partial(_attn_kernel, tk=tk, n_src=n_src),
        grid=(b, MLA_HEADS // 2, lq // tq),
        in_specs=in_specs,
        out_specs=pl.BlockSpec((1, tq, 2 * MLA_V), lambda i, p, j: (i, j, p)),
        out_shape=jax.ShapeDtypeStruct((b, lq, MLA_WIDTH), BF16),
        compiler_params=_cparams("arbitrary", "arbitrary", "arbitrary"),
        name="mla_attention",
    )(*args)


def _mixout_kernel(x_ref, mod_ref, post_ref, y5_ref, u_ref, d_ref, wglu_ref, bglu_ref, conv_ref, pool_ref, att_ref,
                   wout_ref, o_ref):
    x = x_ref[0]
    _, _, gate = _mod_slices(mod_ref, 3)
    y5 = jax.nn.gelu(y5_ref[0] + d_ref[...] * u_ref[0])
    s5 = y5 * jax.nn.sigmoid(_dot(y5.astype(BF16), wglu_ref[...]) + bglu_ref[...])
    o1, o2, o3 = S5_WIDTH, S5_WIDTH + CONV_WIDTH, S5_WIDTH + CONV_WIDTH + POOL_WIDTH
    y = (_dot(s5.astype(BF16), wout_ref[0:o1, :]) + _dot(conv_ref[0], wout_ref[o1:o2, :])
         + _dot(pool_ref[0], wout_ref[o2:o3, :]) + _dot(att_ref[0], wout_ref[o3:, :]))
    o_ref[0] = x + gate * _rms(y, post_ref[...])


def _mix_out(x, mod_rows, post_g, y5, u, conv, pool, att, lw, tm):
    b, s, d = x.shape
    per_batch = mod_rows.shape[0] > 1
    row = lambda w: pl.BlockSpec((1, tm, w), lambda i, j: (i, j, 0))
    return pl.pallas_call(
        _mixout_kernel,
        grid=(b, s // tm),
        in_specs=[
            row(d),
            pl.BlockSpec((1, 1, N_MOD * d), (lambda i, j: (i, 0, 0)) if per_batch else (lambda i, j: (0, 0, 0))),
            _const_spec((1, d)),
            row(S5_WIDTH), row(S5_WIDTH), _const_spec((1, S5_WIDTH)), _const_spec((S5_WIDTH, S5_WIDTH)),
            _const_spec((1, S5_WIDTH)),
            row(CONV_WIDTH), row(POOL_WIDTH), row(MLA_WIDTH), _const_spec((D_MIX, d)),
        ],
        out_specs=row(d),
        out_shape=jax.ShapeDtypeStruct(x.shape, F32),
        compiler_params=_cparams("arbitrary", "arbitrary"),
        name="mixer_out_proj",
    )(x, mod_rows, post_g.reshape(1, d), y5, u, lw["s5_d"], lw["s5_w_glu"], lw["s5_b_glu"], conv, pool, att,
      lw["w_out"])


def _rope_partner(w):
    q = MLA_ROPE // 4
    return jnp.concatenate([-w[..., q:2 * q], w[..., 0:q], -w[..., 3 * q:4 * q], w[..., 2 * q:3 * q]], axis=-1)


def _rope_tables(n_lat):
    rows = n_lat // GRID_W
    row = jnp.repeat(jnp.arange(rows), GRID_W).astype(F32)
    col = (jnp.arange(rows * GRID_W) % GRID_W).astype(F32)
    axis_dim = MLA_ROPE // 2
    inv = ROPE_BASE ** (-jnp.arange(0, axis_dim, 2, dtype=F32) / axis_dim)
    ang_r = row[:, None] * inv
    ang_c = col[:, None] * inv
    ones = jnp.ones((n_lat, MLA_NOPE), F32)
    tail = jnp.ones((n_lat, HEAD_PAD - MLA_NOPE - MLA_ROPE), F32)
    cos = jnp.concatenate([ones, jnp.cos(ang_r), jnp.cos(ang_r), jnp.cos(ang_c), jnp.cos(ang_c), tail], axis=1)
    sin = jnp.concatenate([0 * ones, jnp.sin(ang_r), jnp.sin(ang_r), jnp.sin(ang_c), jnp.sin(ang_c), 0 * tail], axis=1)
    return cos, sin


def _pad_head(nope, rope):
    r = nope.shape[0]
    z = jnp.zeros((r, MLA_HEADS, HEAD_PAD - MLA_NOPE - MLA_ROPE), nope.dtype)
    return jnp.concatenate([nope, rope, z], axis=-1).reshape(r, MLA_HEADS * HEAD_PAD)


def _prep_layer(l, w_in, w_out, s5_d, s5_w_glu, s5_b_glu, conv_w, conv_b, conv_ln_g, conv_ln_b, pool_w, pool_scale,
                mla_q_norm, mla_w_uq, mla_kv_norm, mla_w_ukv):
    d = D_MODEL
    wi = w_in[l]
    kr = wi[:, IN_KR:D_IN]
    zpad_l = jnp.zeros((d, MLA_NOPE), F32)
    zpad_r = jnp.zeros((d, HEAD_PAD - MLA_NOPE - MLA_ROPE), F32)
    w_in_ext = jnp.concatenate([wi[:, :IN_KR], zpad_l, kr, zpad_r, zpad_l, _rope_partner(kr), zpad_r], axis=1)
    uq = mla_w_uq[l].reshape(MLA_Q_RANK, MLA_HEADS, MLA_NOPE + MLA_ROPE)
    q_nope, q_rope = uq[..., :MLA_NOPE], uq[..., MLA_NOPE:]
    wq = jnp.concatenate([_pad_head(q_nope, q_rope), _pad_head(0 * q_nope, _rope_partner(q_rope))], axis=1)
    ukv = mla_w_ukv[l].reshape(MLA_KV_RANK, MLA_HEADS, MLA_NOPE + MLA_V)
    wk = _pad_head(ukv[..., :MLA_NOPE], jnp.zeros((MLA_KV_RANK, MLA_HEADS, MLA_ROPE), F32))
    wv = ukv[..., MLA_NOPE:].reshape(MLA_KV_RANK, MLA_WIDTH)
    eye = jnp.eye(HEAD_PAD, dtype=F32) * ((jnp.arange(HEAD_PAD) >= MLA_NOPE)
                                          & (jnp.arange(HEAD_PAD) < MLA_NOPE + MLA_ROPE))[:, None]
    place = jnp.tile(eye, (1, MLA_HEADS))
    pool_bd = jnp.zeros((POOL_WIDTH, POOL_WIDTH), F32)
    for gi in range(len(POOL_WINDOWS)):
        sl = slice(gi * POOL_GROUP, (gi + 1) * POOL_GROUP)
        pool_bd = pool_bd.at[sl, sl].set(pool_w[l, gi])
    return {
        "w_in": w_in_ext.astype(BF16), "w_out": w_out[l].astype(BF16),
        "q_norm": mla_q_norm[l].reshape(1, -1), "kv_norm": mla_kv_norm[l].reshape(1, -1),
        "wq": wq.astype(BF16), "wk": wk.astype(BF16), "wv": wv.astype(BF16), "place": place.astype(BF16),
        "s5_d": s5_d[l].reshape(1, -1), "s5_w_glu": s5_w_glu[l].astype(BF16), "s5_b_glu": s5_b_glu[l].reshape(1, -1),
        "conv_w": jnp.concatenate([conv_w[l], jnp.zeros((1, CONV_WIDTH), F32)], axis=0),
        "conv_b": conv_b[l].reshape(1, -1), "conv_ln_g": conv_ln_g[l].reshape(1, -1),
        "conv_ln_b": conv_ln_b[l].reshape(1, -1),
        "pool_w": pool_bd.astype(BF16), "pool_scale": pool_scale[l].reshape(1, -1),
    }


def kernel(x, c, ctx, c_ctx, w_ada, b_ada, norm_pre, norm_post, ffn_w_gate, ffn_w_up, ffn_w_down, w_in, w_out,
           s5_lam_re, s5_lam_im, s5_log_dt, s5_b_re, s5_b_im, s5_c_re, s5_c_im, s5_d, s5_w_glu, s5_b_glu, conv_w,
           conv_b, conv_ln_g, conv_ln_b, pool_w, pool_scale, mla_q_norm, mla_w_uq, mla_kv_norm, mla_w_ukv):
    batch, n_lat, d = x.shape
    n_ctx = ctx.shape[1]
    depth = w_ada.shape[0]
    assert d == D_MODEL and batch + 1 <= 8
    tm_lat = min(512, n_lat)
    tm_ctx = min(512, n_ctx)
    tq_lat = min(512, n_lat)
    tk = 512
    assert n_lat % tm_lat == 0 and n_lat % tk == 0 and n_ctx % S5_CHUNK == 0 and n_lat % GRID_W == 0

    c_rows = jnp.zeros((8, d), F32).at[:batch].set(c).at[batch].set(c_ctx)
    mod = _modulation(c_rows, w_ada, b_ada)
    cos_lat, sin_lat = _rope_tables(n_lat)
    cos_ctx = jnp.ones((n_ctx, HEAD_PAD), F32)
    sin_ctx = jnp.zeros((n_ctx, HEAD_PAD), F32)
    wg = ffn_w_gate.astype(BF16)
    wu = ffn_w_up.astype(BF16)
    wd = ffn_w_down.astype(BF16)

    x_lat, x_ctx = x, ctx
    for l in range(depth):
        last = l == depth - 1
        mod_lat = mod[l, :batch].reshape(batch, 1, N_MOD * d)
        mod_ctx = mod[l, batch].reshape(1, 1, N_MOD * d)
        lw = _prep_layer(l, w_in, w_out, s5_d, s5_w_glu, s5_b_glu, conv_w, conv_b, conv_ln_g, conv_ln_b, pool_w,
                         pool_scale, mla_q_norm, mla_w_uq, mla_kv_norm, mla_w_ukv)
        s5_mats = _s5_matrices(s5_lam_re[l], s5_lam_im[l], s5_log_dt[l], s5_b_re[l], s5_b_im[l], s5_c_re[l],
                               s5_c_im[l])
        x_lat = _ffn(x_lat, mod_lat, 0, norm_pre[l, 0], norm_post[l, 0], wg[l, 0], wu[l, 0], wd[l, 0], tm_lat)
        x_ctx = _ffn(x_ctx, mod_ctx, 0, norm_pre[l, 0], norm_post[l, 0], wg[l, 0], wu[l, 0], wd[l, 0], tm_ctx)
        u_l, zc_l, zp_l, q_l, k_l, v_l = _inproj(x_lat, mod_lat, norm_pre[l, 1], lw, cos_lat, sin_lat, tm_lat)
        u_c, zc_c, zp_c, q_c, k_c, v_c = _inproj(x_ctx, mod_ctx, norm_pre[l, 1], lw, cos_ctx, sin_ctx, tm_ctx)
        y5_c, y5_l = _s5_mixer(u_c, u_l, s5_mats)
        conv_l, pool_l = _local_mixers(zc_l, zp_l, lw, tm_lat)
        att_l = _attention(q_l, [(k_l, v_l), (k_c, v_c)], tq_lat, tk)
        x_lat = _mix_out(x_lat, mod_lat, norm_post[l, 1], y5_l, u_l, conv_l, pool_l, att_l, lw, tm_lat)
        if not last:
            conv_c, pool_c = _local_mixers(zc_c, zp_c, lw, tm_ctx)
            att_c = _attention(q_c, [(k_c, v_c)], tm_ctx, tk)
            x_ctx = _mix_out(x_ctx, mod_ctx, norm_post[l, 1], y5_c, u_c, conv_c, pool_c, att_c, lw, tm_ctx)
        x_lat = _ffn(x_lat, mod_lat, 6, norm_pre[l, 2], norm_post[l, 2], wg[l, 1], wu[l, 1], wd[l, 1], tm_lat)
        if not last:
            x_ctx = _ffn(x_ctx, mod_ctx, 6, norm_pre[l, 2], norm_post[l, 2], wg[l, 1], wu[l, 1], wd[l, 1], tm_ctx)
    return x_lat
```

```python
import functools
import math

import jax
import jax.numpy as jnp
from jax import lax
from jax.experimental import pallas as pl
from jax.experimental.pallas import tpu as pltpu

F32 = jnp.float32
BF16 = jnp.bfloat16

D_MODEL = 1024
GRID_W = 64
EPS = 1e-6
N_MOD = 9
MACARON_WEIGHT = 0.5
D_FF = 2816
S5_WIDTH = 256
S5_GROUP = 16
S5_GROUPS = 16
S5_STATE = 64
CONV_WIDTH = 256
CONV_K = 31
POOL_WIDTH = 256
POOL_WINDOWS = (2, 4, 8, 16)
POOL_GROUP = 64
MLA_HEADS = 8
MLA_NOPE = 64
MLA_ROPE = 32
MLA_V = 64
MLA_Q_RANK = 256
MLA_KV_RANK = 128
MLA_WIDTH = MLA_HEADS * MLA_V
ROPE_BASE = 10000.0
D_MIX = S5_WIDTH + CONV_WIDTH + POOL_WIDTH + MLA_WIDTH
IN_S5 = 0
IN_CONV = IN_S5 + S5_WIDTH
IN_POOL = IN_CONV + 2 * CONV_WIDTH
IN_CQ = IN_POOL + POOL_WIDTH
IN_CKV = IN_CQ + MLA_Q_RANK
IN_KR = IN_CKV + MLA_KV_RANK
D_IN = IN_KR + MLA_ROPE

LANES = 128
HEAD_PAD = 128
S5_CHUNK = 16
S5_FLAT = S5_CHUNK * S5_GROUP
HALO = 16
D_IN_EXT = IN_KR + 2 * HEAD_PAD
FF_CHUNKS = ((0, 1024), (1024, 2048), (2048, 2816))
VMEM_LIMIT = 56 * 1024 * 1024
Q_SCALE = (MLA_NOPE + MLA_ROPE) ** -0.5 * math.log2(math.e)


def _cparams(*sem):
    return pltpu.CompilerParams(dimension_semantics=sem, vmem_limit_bytes=VMEM_LIMIT)


def _const_spec(shape):
    nd = len(shape)
    return pl.BlockSpec(shape, lambda *_: (0,) * nd, pipeline_mode=pl.Buffered(1))


def _rms(x, g):
    return x * lax.rsqrt(jnp.mean(x * x, axis=-1, keepdims=True) + EPS) * g


def _dot(a, b):
    return jnp.dot(a, b, preferred_element_type=F32)


def _mod_kernel(c_ref, w_ref, b_ref, o_ref):
    c = c_ref[...]
    h = (c * jax.nn.sigmoid(c)).astype(BF16)
    o_ref[0] = _dot(h, w_ref[0].astype(BF16)) + b_ref[0]


def _modulation(c_rows, w_ada, b_ada):
    depth = w_ada.shape[0]
    d = D_MODEL
    return pl.pallas_call(
        _mod_kernel,
        grid=(depth, N_MOD),
        in_specs=[
            pl.BlockSpec((8, d), lambda l, j: (0, 0)),
            pl.BlockSpec((1, d, d), lambda l, j: (l, 0, j)),
            pl.BlockSpec((1, 1, d), lambda l, j: (l, 0, j)),
        ],
        out_specs=pl.BlockSpec((1, 8, d), lambda l, j: (l, 0, j)),
        out_shape=jax.ShapeDtypeStruct((depth, 8, N_MOD * d), F32),
        compiler_params=_cparams("arbitrary", "arbitrary"),
        name="adaln_mod",
    )(c_rows, w_ada, b_ada.reshape(depth, 1, N_MOD * d))


def _mod_slices(mod_ref, base):
    d = D_MODEL
    return (mod_ref[0, :, base * d:(base + 1) * d], mod_ref[0, :, (base + 1) * d:(base + 2) * d],
            mod_ref[0, :, (base + 2) * d:(base + 3) * d])


def _ffn_kernel(x_ref, mod_ref, pre_ref, post_ref, wg_ref, wu_ref, wd_ref, o_ref, *, base):
    x = x_ref[0]
    shift, scale, gate = _mod_slices(mod_ref, base)
    h = (_rms(x, pre_ref[...]) * (1.0 + scale) + shift).astype(BF16)
    acc = None
    for lo, hi in FF_CHUNKS:
        g = _dot(h, wg_ref[:, lo:hi])
        u = _dot(h, wu_ref[:, lo:hi])
        a = (g * jax.nn.sigmoid(g) * u).astype(BF16)
        part = _dot(a, wd_ref[lo:hi, :])
        acc = part if acc is None else acc + part
    o_ref[0] = x + MACARON_WEIGHT * gate * _rms(acc, post_ref[...])


def _ffn(x, mod_rows, base, pre_g, post_g, wg, wu, wd, tm):
    b, s, d = x.shape
    per_batch = mod_rows.shape[0] > 1
    return pl.pallas_call(
        functools.partial(_ffn_kernel, base=base),
        grid=(b, s // tm),
        in_specs=[
            pl.BlockSpec((1, tm, d), lambda i, j: (i, j, 0)),
            pl.BlockSpec((1, 1, N_MOD * d), (lambda i, j: (i, 0, 0)) if per_batch else (lambda i, j: (0, 0, 0))),
            _const_spec((1, d)), _const_spec((1, d)),
            _const_spec((d, D_FF)), _const_spec((d, D_FF)), _const_spec((D_FF, d)),
        ],
        out_specs=pl.BlockSpec((1, tm, d), lambda i, j: (i, j, 0)),
        out_shape=jax.ShapeDtypeStruct(x.shape, F32),
        compiler_params=_cparams("arbitrary", "arbitrary"),
        name="ffn_half_step",
    )(x, mod_rows, pre_g.reshape(1, d), post_g.reshape(1, d), wg, wu, wd)


def _inproj_kernel(x_ref, mod_ref, pre_ref, win_ref, qn_ref, kvn_ref, wq_ref, wk_ref, wv_ref, place_ref,
                   cos_ref, sin_ref, u_ref, zc_ref, zp_ref, q_ref, k_ref, v_ref):
    x = x_ref[0]
    shift, scale, _ = _mod_slices(mod_ref, 3)
    h = (_rms(x, pre_ref[...]) * (1.0 + scale) + shift).astype(BF16)
    z = _dot(h, win_ref[...])
    u_ref[0] = z[:, IN_S5:IN_CONV]
    zc_ref[0] = z[:, IN_CONV:IN_POOL]
    zp_ref[0] = z[:, IN_POOL:IN_CQ]
    cos = cos_ref[...]
    sin = sin_ref[...]
    cos_h = jnp.concatenate([cos] * MLA_HEADS, axis=1)
    sin_h = jnp.concatenate([sin] * MLA_HEADS, axis=1)
    cqn = _rms(z[:, IN_CQ:IN_CKV], qn_ref[...]).astype(BF16)
    qq = _dot(cqn, wq_ref[...])
    hw = MLA_HEADS * HEAD_PAD
    q = qq[:, :hw] * cos_h + qq[:, hw:] * sin_h
    q_ref[0] = (q * Q_SCALE).astype(BF16)
    ckvn = _rms(z[:, IN_CKV:IN_KR], kvn_ref[...]).astype(BF16)
    kr = z[:, IN_KR:IN_KR + HEAD_PAD] * cos + z[:, IN_KR + HEAD_PAD:IN_KR + 2 * HEAD_PAD] * sin
    k = _dot(ckvn, wk_ref[...]) + _dot(kr.astype(BF16), place_ref[...])
    k_ref[0] = k.astype(BF16)
    v_ref[0] = _dot(ckvn, wv_ref[...]).astype(BF16)


def _inproj(x, mod_rows, pre_g, lw, cos_t, sin_t, tm):
    b, s, d = x.shape
    per_batch = mod_rows.shape[0] > 1
    hw = MLA_HEADS * HEAD_PAD
    row = lambda w: pl.BlockSpec((1, tm, w), lambda i, j: (i, j, 0))
    return pl.pallas_call(
        _inproj_kernel,
        grid=(b, s // tm),
        in_specs=[
            row(d),
            pl.BlockSpec((1, 1, N_MOD * d), (lambda i, j: (i, 0, 0)) if per_batch else (lambda i, j: (0, 0, 0))),
            _const_spec((1, d)),
            _const_spec((d, D_IN_EXT)),
            _const_spec((1, MLA_Q_RANK)), _const_spec((1, MLA_KV_RANK)),
            _const_spec((MLA_Q_RANK, 2 * hw)), _const_spec((MLA_KV_RANK, hw)),
            _const_spec((MLA_KV_RANK, MLA_WIDTH)), _const_spec((HEAD_PAD, hw)),
            pl.BlockSpec((tm, HEAD_PAD), lambda i, j: (j, 0)),
            pl.BlockSpec((tm, HEAD_PAD), lambda i, j: (j, 0)),
        ],
        out_specs=[row(S5_WIDTH), row(2 * CONV_WIDTH), row(POOL_WIDTH), row(hw), row(hw), row(MLA_WIDTH)],
        out_shape=[
            jax.ShapeDtypeStruct((b, s, S5_WIDTH), F32),
            jax.ShapeDtypeStruct((b, s, 2 * CONV_WIDTH), F32),
            jax.ShapeDtypeStruct((b, s, POOL_WIDTH), F32),
            jax.ShapeDtypeStruct((b, s, hw), BF16),
            jax.ShapeDtypeStruct((b, s, hw), BF16),
            jax.ShapeDtypeStruct((b, s, MLA_WIDTH), BF16),
        ],
        compiler_params=_cparams("arbitrary", "arbitrary"),
        name="mixer_in_proj",
    )(x, mod_rows, pre_g.reshape(1, d), lw["w_in"], lw["q_norm"], lw["kv_norm"], lw["wq"], lw["wk"], lw["wv"],
      lw["place"], cos_t, sin_t)


def _s5_state_kernel(u_ref, m_ref, s_ref):
    s_ref[0, 0] = _dot(u_ref[0, 0], m_ref[0])


def _s5_chunk_states(u_flat, m_state):
    b, g, ct, w = u_flat.shape
    return pl.pallas_call(
        _s5_state_kernel,
        grid=(b, g),
        in_specs=[pl.BlockSpec((1, 1, ct, w), lambda i, j: (i, j, 0, 0)),
                  pl.BlockSpec((1, w, w), lambda i, j: (j, 0, 0))],
        out_specs=pl.BlockSpec((1, 1, ct, w), lambda i, j: (i, j, 0, 0)),
        out_shape=jax.ShapeDtypeStruct((b, g, ct, w), F32),
        compiler_params=_cparams("arbitrary", "arbitrary"),
        name="s5_chunk_states",
    )(u_flat, m_state)


def _s5_scan_kernel(sc_ref, sf_ref, sb_ref, a_ref, hc_ref, hf_ref, hb_ref, st_ref):
    half = LANES // 2
    a_re = a_ref[:, :LANES]
    a_im = a_ref[:, LANES:]
    fwd_lane = lax.broadcasted_iota(jnp.int32, a_re.shape, 1) < half

    def swap(v):
        return pltpu.roll(v, half, 1)

    def run(n, load_f, load_b, store_f, store_b, carry):
        def body(i, c):
            h_re, h_im = c
            r = n - 1 - i
            sf = load_f(i)
            sb = load_b(r)
            store_f(i, jnp.where(fwd_lane, h_re, swap(h_im)))
            store_b(r, jnp.where(fwd_lane, swap(h_re), h_im))
            s_re = jnp.where(fwd_lane, sf, swap(sb))
            s_im = jnp.where(fwd_lane, swap(sf), sb)
            return (a_re * h_re - a_im * h_im + s_re, a_re * h_im + a_im * h_re + s_im)
        return lax.fori_loop(0, n, body, carry)

    @pl.when(pl.program_id(1) == 0)
    def _():
        def st_f(i, v):
            hc_ref[0, i, :, 0:LANES] = v

        def st_b(i, v):
            hc_ref[0, i, :, LANES:2 * LANES] = v

        zero = jnp.zeros(a_re.shape, F32)
        c = run(sc_ref.shape[1], lambda i: sc_ref[0, i, :, 0:LANES], lambda i: sc_ref[0, i, :, LANES:2 * LANES],
                st_f, st_b, (zero, zero))
        st_ref[0] = c[0]
        st_ref[1] = c[1]

    def st_f(i, v):
        hf_ref[0, i] = v

    def st_b(i, v):
        hb_ref[0, i] = v

    c = run(sf_ref.shape[1], lambda i: sf_ref[0, i], lambda i: sb_ref[0, i], st_f, st_b, (st_ref[0], st_ref[1]))
    st_ref[0] = c[0]
    st_ref[1] = c[1]


def _s5_scan(s_ctx, s_lat, a16):
    b, nc, g, w = s_ctx.shape
    c = s_lat.shape[1]
    cb = min(128, c)
    nb = c // cb
    return pl.pallas_call(
        _s5_scan_kernel,
        grid=(b, nb),
        in_specs=[pl.BlockSpec((1, nc, g, w), lambda i, j: (i, 0, 0, 0)),
                  pl.BlockSpec((1, cb, g, LANES), lambda i, j: (i, j, 0, 0)),
                  pl.BlockSpec((1, cb, g, LANES), lambda i, j: (i, nb - 1 - j, 0, 1)),
                  pl.BlockSpec((g, w), lambda i, j: (0, 0))],
        out_specs=[pl.BlockSpec((1, nc, g, w), lambda i, j: (i, 0, 0, 0)),
                   pl.BlockSpec((1, cb, g, LANES), lambda i, j: (i, j, 0, 0)),
                   pl.BlockSpec((1, cb, g, LANES), lambda i, j: (i, nb - 1 - j, 0, 0))],
        out_shape=[jax.ShapeDtypeStruct((b, nc, g, w), F32), jax.ShapeDtypeStruct((b, c, g, LANES), F32),
                   jax.ShapeDtypeStruct((b, c, g, LANES), F32)],
        scratch_shapes=[pltpu.VMEM((2, g, LANES), F32)],
        compiler_params=_cparams("arbitrary", "arbitrary"),
        name="s5_chunk_scan",
    )(s_ctx, s_lat, s_lat, a16)


def _s5_out_kernel(u_ref, h_ref, mi_ref, mo_ref, y_ref):
    y_ref[0, 0] = _dot(u_ref[0, 0], mi_ref[0]) + _dot(h_ref[0, 0], mo_ref[0])


def _s5_outputs(u_flat, h_in, m_intra, m_out):
    b, g, ct, w = u_flat.shape
    blk = pl.BlockSpec((1, 1, ct, w), lambda i, j: (i, j, 0, 0))
    mat = pl.BlockSpec((1, w, w), lambda i, j: (j, 0, 0))
    return pl.pallas_call(
        _s5_out_kernel,
        grid=(b, g),
        in_specs=[blk, blk, mat, mat],
        out_specs=blk,
        out_shape=jax.ShapeDtypeStruct((b, g, ct, w), F32),
        compiler_params=_cparams("arbitrary", "arbitrary"),
        name="s5_chunk_outputs",
    )(u_flat, h_in, m_intra, m_out)


def _s5_matrices(lam_re, lam_im, log_dt, b_re, b_im, c_re, c_im):
    hp = lax.Precision.HIGHEST
    t = S5_CHUNK
    g, p, h = S5_GROUPS, S5_STATE, S5_GROUP
    lam = lax.complex(jnp.minimum(lam_re, -1e-4), lam_im)
    dt = jnp.exp(log_dt)[..., None]
    steps = jnp.arange(t + 1, dtype=F32)[:, None, None, None]
    apow = jnp.exp(lam[None] * dt[None] * steps)
    bb = ((apow[1] - 1.0) / lam)[..., None] * lax.complex(b_re, b_im)
    cc = lax.complex(c_re, c_im)
    taps = jnp.einsum('dghp,jdgp,dgpk->djghk', cc, apow[:t], bb, precision=hp).real
    s_idx = jnp.arange(t)[:, None]
    t_idx = jnp.arange(t)[None, :]
    kf = taps[0][jnp.clip(t_idx - s_idx, 0, t - 1)] * (t_idx >= s_idx)[:, :, None, None, None].astype(F32)
    kb = taps[1][jnp.clip(s_idx - t_idx, 0, t - 1)] * (s_idx >= t_idx)[:, :, None, None, None].astype(F32)
    m_intra = (kf + kb).transpose(2, 0, 4, 1, 3).reshape(g, t * h, t * h)
    wf = apow[:t][::-1, 0][:, :, :, None] * bb[0][None]
    wb = apow[:t, 1][:, :, :, None] * bb[1][None]
    to_rows = lambda w: w.transpose(1, 0, 3, 2).reshape(g, t * h, p)
    m_state = jnp.concatenate([to_rows(wf.real), to_rows(wf.imag), to_rows(wb.real), to_rows(wb.imag)], axis=-1)
    of = cc[0][None] * apow[1:, 0][:, :, None, :]
    ob = cc[1][None] * apow[1:, 1][::-1][:, :, None, :]
    to_cols = lambda w: w.transpose(1, 3, 0, 2).reshape(g, p, t * h)
    m_out = jnp.concatenate([to_cols(of.real), -to_cols(of.imag), to_cols(ob.real), -to_cols(ob.imag)], axis=1)
    a_t = apow[t]
    a16 = jnp.concatenate([a_t[0].real, a_t[1].real, a_t[0].imag, a_t[1].imag], axis=-1)
    return m_intra.astype(BF16), m_state.astype(BF16), m_out.astype(BF16), a16


def _s5_mixer(u_ctx, u_lat, mats):
    m_intra, m_state, m_out, a16 = mats
    b, lc, _ = u_ctx.shape
    g, h, t = S5_GROUPS, S5_GROUP, S5_CHUNK

    def flat(u):
        n = u.shape[1] // t
        return u.astype(BF16).reshape(b, n, t, g, h).transpose(0, 3, 1, 2, 4).reshape(b, g, n, t * h)

    u_flat = jnp.concatenate([flat(u_ctx), flat(u_lat)], axis=2)
    ct = u_flat.shape[2]
    nc = lc // t
    s_loc = _s5_chunk_states(u_flat, m_state).transpose(0, 2, 1, 3)
    h_ctx, h_f, h_b = _s5_scan(s_loc[:, :nc], s_loc[:, nc:], a16)
    h_in = jnp.concatenate([h_ctx, jnp.concatenate([h_f, h_b], axis=-1)], axis=1)
    y = _s5_outputs(u_flat, h_in.transpose(0, 2, 1, 3).astype(BF16), m_intra, m_out)
    y = y.reshape(b, g, ct, t, h).transpose(0, 2, 3, 1, 4).reshape(b, ct * t, g * h)
    return y[:, :lc], y[:, lc:]


def _local_kernel(zc_ref, zcp_ref, zcn_ref, zp_ref, zpp_ref, zpn_ref, cw_ref, cb_ref, lg_ref, lb_ref, pw_ref, ps_ref,
                  conv_ref, pool_ref, ce_ref, pe_ref, *, seq_len):
    tm = zc_ref.shape[1]
    j = pl.program_id(1)
    first = j == 0
    last = j == pl.num_programs(1) - 1

    def glu(z):
        return z[:, :CONV_WIDTH] * jax.nn.sigmoid(z[:, CONV_WIDTH:])

    ce_ref[0:HALO] = jnp.where(first, 0.0, glu(zcp_ref[0]))
    ce_ref[HALO:HALO + tm] = glu(zc_ref[0])
    ce_ref[HALO + tm:] = jnp.where(last, 0.0, glu(zcn_ref[0]))
    pe_ref[0:HALO] = jnp.where(first, 0.0, zpp_ref[0])
    pe_ref[HALO:HALO + tm] = zp_ref[0]
    pe_ref[HALO + tm:] = jnp.where(last, 0.0, zpn_ref[0])

    acc = None
    for k in range(CONV_K):
        term = ce_ref[pl.ds(HALO + k - CONV_K // 2, tm), :] * cw_ref[k:k + 1, :]
        acc = term if acc is None else acc + term
    acc = acc + cb_ref[...]
    mu = jnp.mean(acc, axis=-1, keepdims=True)
    xc = acc - mu
    var = jnp.mean(xc * xc, axis=-1, keepdims=True)
    y = xc * lax.rsqrt(var + EPS) * lg_ref[...] + lb_ref[...]
    conv_ref[0] = (y * jax.nn.sigmoid(y)).astype(BF16)

    def tap(dlt):
        return pe_ref[pl.ds(HALO + dlt, tm), :]

    z = tap(0)
    sums = {}
    run = z + tap(1)
    sums[2] = run
    run = run + tap(-1) + tap(2)
    sums[4] = run
    run = run + tap(-3) + tap(-2) + tap(3) + tap(4)
    sums[8] = run
    for dlt in (-7, -6, -5, -4, 5, 6, 7, 8):
        run = run + tap(dlt)
    sums[16] = run
    pos = j * tm + lax.broadcasted_iota(jnp.int32, (tm, 1), 0)
    lane = lax.broadcasted_iota(jnp.int32, (tm, POOL_WIDTH), 1)
    mean = None
    for gi, w in enumerate(POOL_WINDOWS):
        lo = jnp.maximum(pos - (w - 1) // 2, 0)
        hi = jnp.minimum(pos + w // 2, seq_len - 1)
        m = sums[w] / (hi - lo + 1).astype(F32)
        mean = m if mean is None else jnp.where(lane >= gi * POOL_GROUP, m, mean)
    d = (mean - z).astype(BF16)
    pool_ref[0] = (_dot(d, pw_ref[...]) * ps_ref[...]).astype(BF16)


def _local_mixers(zc, zp, lw, tm):
    b, s, _ = zc.shape
    nh = tm // HALO
    nblk = s // HALO
    main = lambda w: pl.BlockSpec((1, tm, w), lambda i, j: (i, j, 0))
    prev = lambda w: pl.BlockSpec((1, HALO, w), lambda i, j: (i, jnp.maximum(j * nh - 1, 0), 0))
    nxt = lambda w: pl.BlockSpec((1, HALO, w), lambda i, j: (i, jnp.minimum((j + 1) * nh, nblk - 1), 0))
    cw2, pw = 2 * CONV_WIDTH, POOL_WIDTH
    return pl.pallas_call(
        functools.partial(_local_kernel, seq_len=s),
        grid=(b, s // tm),
        in_specs=[main(cw2), prev(cw2), nxt(cw2), main(pw), prev(pw), nxt(pw),
                  _const_spec((32, CONV_WIDTH)), _const_spec((1, CONV_WIDTH)), _const_spec((1, CONV_WIDTH)),
                  _const_spec((1, CONV_WIDTH)), _const_spec((pw, pw)), _const_spec((1, pw))],
        out_specs=[main(CONV_WIDTH), main(pw)],
        out_shape=[jax.ShapeDtypeStruct((b, s, CONV_WIDTH), BF16), jax.ShapeDtypeStruct((b, s, pw), BF16)],
        scratch_shapes=[pltpu.VMEM((tm + 2 * HALO, CONV_WIDTH), F32), pltpu.VMEM((tm + 2 * HALO, pw), F32)],
        compiler_params=_cparams("arbitrary", "arbitrary"),
        name="conv_pool",
    )(zc, zc, zc, zp, zp, zp, lw["conv_w"], lw["conv_b"], lw["conv_ln_g"], lw["conv_ln_b"], lw["pool_w"],
      lw["pool_scale"])


def _attn_kernel(*refs, tk, has_lat):
    if has_lat:
        q_ref, kc_ref, vtc_ref, kl_ref, vtl_ref, o_ref, s_ref = refs
    else:
        q_ref, kc_ref, vtc_ref, o_ref = refs
    tq = q_ref.shape[1]
    heads = range(2)
    qs = [q_ref[0, :, h * HEAD_PAD:(h + 1) * HEAD_PAD] for h in heads]

    def scores(h, k):
        return lax.dot_general(k, qs[h], (((1,), (1,)), ((), ())), preferred_element_type=F32)

    def consume(s, mt, vt, state):
        m, l, acc = state
        m_new = jnp.maximum(m, mt)
        p = jnp.exp2(s - m_new)
        alpha = jnp.exp2(m - m_new)
        l = alpha * l + jnp.sum(p, axis=0, keepdims=True)
        acc = alpha * acc + _dot(vt, p.astype(BF16))
        return m_new, l, acc

    init = (jnp.full((1, tq), -1e30, F32), jnp.zeros((1, tq), F32), jnp.zeros((MLA_V, tq), F32))
    state = []
    for h in heads:
        s = scores(h, kc_ref[0, :, h * HEAD_PAD:(h + 1) * HEAD_PAD])
        state.append(consume(s, jnp.max(s, axis=0, keepdims=True), vtc_ref[0, h * MLA_V:(h + 1) * MLA_V, :], init))

    if has_lat:
        n_tiles = kl_ref.shape[1] // tk

        def produce(h, slot, i):
            off = pl.multiple_of(i * tk, tk)
            s = scores(h, kl_ref[0, pl.ds(off, tk), h * HEAD_PAD:(h + 1) * HEAD_PAD])
            s_ref[h, slot] = s
            return jnp.max(s, axis=0, keepdims=True)

        def use(h, slot, i, mt, st):
            off = pl.multiple_of(i * tk, tk)
            return consume(s_ref[h, slot], mt, vtl_ref[0, h * MLA_V:(h + 1) * MLA_V, pl.ds(off, tk)], st)

        def body(j, carry):
            st, mt0 = carry
            i0 = 2 * j
            mt1 = [produce(h, 1, i0 + 1) for h in heads]
            st = [use(h, 0, i0, mt0[h], st[h]) for h in heads]
            nxt = jnp.minimum(i0 + 2, n_tiles - 1)
            mt0 = [produce(h, 0, nxt) for h in heads]
            st = [use(h, 1, i0 + 1, mt1[h], st[h]) for h in heads]
            return st, mt0

        mt0 = [produce(h, 0, 0) for h in heads]
        state, _ = lax.fori_loop(0, n_tiles // 2, body, (state, mt0))
    o_ref[0] = jnp.concatenate([acc / l for _, l, acc in state], axis=0).astype(BF16)


def _attention(q, k_ctx, vt_ctx, k_lat, vt_lat, tq, tk):
    b, lq, _ = q.shape
    lc = k_ctx.shape[1]
    has_lat = k_lat is not None
    in_specs = [pl.BlockSpec((1, tq, 2 * HEAD_PAD), lambda i, p, j: (i, j, p)),
                pl.BlockSpec((1, lc, 2 * HEAD_PAD), lambda i, p, j: (i, 0, p)),
                pl.BlockSpec((1, 2 * MLA_V, lc), lambda i, p, j: (i, p, 0))]
    args = [q, k_ctx, vt_ctx]
    scratch = []
    if has_lat:
        ll = k_lat.shape[1]
        assert ll % (2 * tk) == 0
        in_specs += [pl.BlockSpec((1, ll, 2 * HEAD_PAD), lambda i, p, j: (i, 0, p)),
                     pl.BlockSpec((1, 2 * MLA_V, ll), lambda i, p, j: (i, p, 0))]
        args += [k_lat, vt_lat]
        scratch = [pltpu.VMEM((2, 2, tk, tq), F32)]
    return pl.pallas_call(
        functools.partial(_attn_kernel, tk=tk, has_lat=has_lat),
        grid=(b, MLA_HEADS // 2, lq // tq),
        in_specs=in_specs,
        out_specs=pl.BlockSpec((1, 2 * MLA_V, tq), lambda i, p, j: (i, p, j)),
        out_shape=jax.ShapeDtypeStruct((b, MLA_WIDTH, lq), BF16),
        scratch_shapes=scratch,
        compiler_params=_cparams("arbitrary", "arbitrary", "arbitrary"),
        name="mla_attention",
    )(*args)


def _mixout_kernel(x_ref, mod_ref, post_ref, y5_ref, u_ref, d_ref, wglu_ref, bglu_ref, conv_ref, pool_ref, att_ref,
                   wout_ref, o_ref):
    x = x_ref[0]
    _, _, gate = _mod_slices(mod_ref, 3)
    y5 = jax.nn.gelu(y5_ref[0] + d_ref[...] * u_ref[0])
    s5 = y5 * jax.nn.sigmoid(_dot(y5.astype(BF16), wglu_ref[...]) + bglu_ref[...])
    o1, o2, o3 = S5_WIDTH, S5_WIDTH + CONV_WIDTH, S5_WIDTH + CONV_WIDTH + POOL_WIDTH
    y = (_dot(s5.astype(BF16), wout_ref[0:o1, :]) + _dot(conv_ref[0], wout_ref[o1:o2, :])
         + _dot(pool_ref[0], wout_ref[o2:o3, :]) + _dot(att_ref[0], wout_ref[o3:, :]))
    o_ref[0] = x + gate * _rms(y, post_ref[...])


def _mix_out(x, mod_rows, post_g, y5, u, conv, pool, att, lw, tm):
    b, s, d = x.shape
    per_batch = mod_rows.shape[0] > 1
    row = lambda w: pl.BlockSpec((1, tm, w), lambda i, j: (i, j, 0))
    return pl.pallas_call(
        _mixout_kernel,
        grid=(b, s // tm),
        in_specs=[
            row(d),
            pl.BlockSpec((1, 1, N_MOD * d), (lambda i, j: (i, 0, 0)) if per_batch else (lambda i, j: (0, 0, 0))),
            _const_spec((1, d)),
            row(S5_WIDTH), row(S5_WIDTH), _const_spec((1, S5_WIDTH)), _const_spec((S5_WIDTH, S5_WIDTH)),
            _const_spec((1, S5_WIDTH)),
            row(CONV_WIDTH), row(POOL_WIDTH), row(MLA_WIDTH), _const_spec((D_MIX, d)),
        ],
        out_specs=row(d),
        out_shape=jax.ShapeDtypeStruct(x.shape, F32),
        compiler_params=_cparams("arbitrary", "arbitrary"),
        name="mixer_out_proj",
    )(x, mod_rows, post_g.reshape(1, d), y5, u, lw["s5_d"], lw["s5_w_glu"], lw["s5_b_glu"], conv, pool, att,
      lw["w_out"])


def _rope_partner(w):
    q = MLA_ROPE // 4
    return jnp.concatenate([-w[..., q:2 * q], w[..., 0:q], -w[..., 3 * q:4 * q], w[..., 2 * q:3 * q]], axis=-1)


def _rope_tables(n_lat):
    rows = n_lat // GRID_W
    row = jnp.repeat(jnp.arange(rows), GRID_W).astype(F32)
    col = (jnp.arange(rows * GRID_W) % GRID_W).astype(F32)
    axis_dim = MLA_ROPE // 2
    inv = ROPE_BASE ** (-jnp.arange(0, axis_dim, 2, dtype=F32) / axis_dim)
    ang_r = row[:, None] * inv
    ang_c = col[:, None] * inv
    ones = jnp.ones((n_lat, MLA_NOPE), F32)
    tail = jnp.ones((n_lat, HEAD_PAD - MLA_NOPE - MLA_ROPE), F32)
    cos = jnp.concatenate([ones, jnp.cos(ang_r), jnp.cos(ang_r), jnp.cos(ang_c), jnp.cos(ang_c), tail], axis=1)
    sin = jnp.concatenate([0 * ones, jnp.sin(ang_r), jnp.sin(ang_r), jnp.sin(ang_c), jnp.sin(ang_c), 0 * tail], axis=1)
    return cos, sin


def _pad_head(nope, rope):
    r = nope.shape[0]
    z = jnp.zeros((r, MLA_HEADS, HEAD_PAD - MLA_NOPE - MLA_ROPE), nope.dtype)
    return jnp.concatenate([nope, rope, z], axis=-1).reshape(r, MLA_HEADS * HEAD_PAD)


def _prep_layer(l, w_in, w_out, s5_d, s5_w_glu, s5_b_glu, conv_w, conv_b, conv_ln_g, conv_ln_b, pool_w, pool_scale,
                mla_q_norm, mla_w_uq, mla_kv_norm, mla_w_ukv):
    d = D_MODEL
    wi = w_in[l]
    kr = wi[:, IN_KR:D_IN]
    zpad_l = jnp.zeros((d, MLA_NOPE), F32)
    zpad_r = jnp.zeros((d, HEAD_PAD - MLA_NOPE - MLA_ROPE), F32)
    w_in_ext = jnp.concatenate([wi[:, :IN_KR], zpad_l, kr, zpad_r, zpad_l, _rope_partner(kr), zpad_r], axis=1)
    uq = mla_w_uq[l].reshape(MLA_Q_RANK, MLA_HEADS, MLA_NOPE + MLA_ROPE)
    q_nope, q_rope = uq[..., :MLA_NOPE], uq[..., MLA_NOPE:]
    wq = jnp.concatenate([_pad_head(q_nope, q_rope), _pad_head(0 * q_nope, _rope_partner(q_rope))], axis=1)
    ukv = mla_w_ukv[l].reshape(MLA_KV_RANK, MLA_HEADS, MLA_NOPE + MLA_V)
    wk = _pad_head(ukv[..., :MLA_NOPE], jnp.zeros((MLA_KV_RANK, MLA_HEADS, MLA_ROPE), F32))
    wv = ukv[..., MLA_NOPE:].reshape(MLA_KV_RANK, MLA_WIDTH)
    eye = jnp.eye(HEAD_PAD, dtype=F32) * ((jnp.arange(HEAD_PAD) >= MLA_NOPE)
                                          & (jnp.arange(HEAD_PAD) < MLA_NOPE + MLA_ROPE))[:, None]
    place = jnp.tile(eye, (1, MLA_HEADS))
    pool_bd = jnp.zeros((POOL_WIDTH, POOL_WIDTH), F32)
    for gi in range(len(POOL_WINDOWS)):
        sl = slice(gi * POOL_GROUP, (gi + 1) * POOL_GROUP)
        pool_bd = pool_bd.at[sl, sl].set(pool_w[l, gi])
    return {
        "w_in": w_in_ext.astype(BF16), "w_out": w_out[l].astype(BF16),
        "q_norm": mla_q_norm[l].reshape(1, -1), "kv_norm": mla_kv_norm[l].reshape(1, -1),
        "wq": wq.astype(BF16), "wk": wk.astype(BF16), "wv": wv.astype(BF16), "place": place.astype(BF16),
        "s5_d": s5_d[l].reshape(1, -1), "s5_w_glu": s5_w_glu[l].astype(BF16), "s5_b_glu": s5_b_glu[l].reshape(1, -1),
        "conv_w": jnp.concatenate([conv_w[l], jnp.zeros((1, CONV_WIDTH), F32)], axis=0),
        "conv_b": conv_b[l].reshape(1, -1), "conv_ln_g": conv_ln_g[l].reshape(1, -1),
        "conv_ln_b": conv_ln_b[l].reshape(1, -1),
        "pool_w": pool_bd.astype(BF16), "pool_scale": pool_scale[l].reshape(1, -1),
    }


def kernel(x, c, ctx, c_ctx, w_ada, b_ada, norm_pre, norm_post, ffn_w_gate, ffn_w_up, ffn_w_down, w_in, w_out,
           s5_lam_re, s5_lam_im, s5_log_dt, s5_b_re, s5_b_im, s5_c_re, s5_c_im, s5_d, s5_w_glu, s5_b_glu, conv_w,
           conv_b, conv_ln_g, conv_ln_b, pool_w, pool_scale, mla_q_norm, mla_w_uq, mla_kv_norm, mla_w_ukv):
    batch, n_lat, d = x.shape
    n_ctx = ctx.shape[1]
    depth = w_ada.shape[0]
    assert d == D_MODEL and batch + 1 <= 8
    tm_lat = min(512, n_lat)
    tm_ctx = min(512, n_ctx)
    tq_lat = min(512, n_lat)
    tk = min(512, n_lat // 2)
    assert n_lat % tm_lat == 0 and n_lat % (2 * tk) == 0 and n_ctx % S5_CHUNK == 0 and n_lat % GRID_W == 0

    c_rows = jnp.zeros((8, d), F32).at[:batch].set(c).at[batch].set(c_ctx)
    mod = _modulation(c_rows, w_ada, b_ada)
    cos_lat, sin_lat = _rope_tables(n_lat)
    cos_ctx = jnp.ones((n_ctx, HEAD_PAD), F32)
    sin_ctx = jnp.zeros((n_ctx, HEAD_PAD), F32)
    wg = ffn_w_gate.astype(BF16)
    wu = ffn_w_up.astype(BF16)
    wd = ffn_w_down.astype(BF16)

    x_lat, x_ctx = x, ctx
    for l in range(depth):
        last = l == depth - 1
        mod_lat = mod[l, :batch].reshape(batch, 1, N_MOD * d)
        mod_ctx = mod[l, batch].reshape(1, 1, N_MOD * d)
        lw = _prep_layer(l, w_in, w_out, s5_d, s5_w_glu, s5_b_glu, conv_w, conv_b, conv_ln_g, conv_ln_b, pool_w,
                         pool_scale, mla_q_norm, mla_w_uq, mla_kv_norm, mla_w_ukv)
        s5_mats = _s5_matrices(s5_lam_re[l], s5_lam_im[l], s5_log_dt[l], s5_b_re[l], s5_b_im[l], s5_c_re[l],
                               s5_c_im[l])
        x_lat = _ffn(x_lat, mod_lat, 0, norm_pre[l, 0], norm_post[l, 0], wg[l, 0], wu[l, 0], wd[l, 0], tm_lat)
        x_ctx = _ffn(x_ctx, mod_ctx, 0, norm_pre[l, 0], norm_post[l, 0], wg[l, 0], wu[l, 0], wd[l, 0], tm_ctx)
        u_l, zc_l, zp_l, q_l, k_l, v_l = _inproj(x_lat, mod_lat, norm_pre[l, 1], lw, cos_lat, sin_lat, tm_lat)
        u_c, zc_c, zp_c, q_c, k_c, v_c = _inproj(x_ctx, mod_ctx, norm_pre[l, 1], lw, cos_ctx, sin_ctx, tm_ctx)
        y5_c, y5_l = _s5_mixer(u_c, u_l, s5_mats)
        conv_l, pool_l = _local_mixers(zc_l, zp_l, lw, tm_lat)
        vt_l, vt_c = v_l.transpose(0, 2, 1), v_c.transpose(0, 2, 1)
        att_l = _attention(q_l, k_c, vt_c, k_l, vt_l, tq_lat, tk).transpose(0, 2, 1)
        x_lat = _mix_out(x_lat, mod_lat, norm_post[l, 1], y5_l, u_l, conv_l, pool_l, att_l, lw, tm_lat)
        if not last:
            conv_c, pool_c = _local_mixers(zc_c, zp_c, lw, tm_ctx)
            att_c = _attention(q_c, k_c, vt_c, None, None, tm_ctx, tk).transpose(0, 2, 1)
            x_ctx = _mix_out(x_ctx, mod_ctx, norm_post[l, 1], y5_c, u_c, conv_c, pool_c, att_c, lw, tm_ctx)
        x_lat = _ffn(x_lat, mod_lat, 6, norm_pre[l, 2], norm_post[l, 2], wg[l, 1], wu[l, 1], wd[l, 1], tm_lat)
        if not last:
            x_ctx = _ffn(x_ctx, mod_ctx, 6, norm_pre[l, 2], norm_post[l, 2], wg[l, 1], wu[l, 1], wd[l, 1], tm_ctx)
    return x_lat
```

```python
import functools
import math

import jax
import jax.numpy as jnp
from jax import lax
from jax.experimental import pallas as pl
from jax.experimental.pallas import tpu as pltpu

F32 = jnp.float32
BF16 = jnp.bfloat16

D_MODEL = 1024
GRID_W = 64
EPS = 1e-6
N_MOD = 9
MACARON_WEIGHT = 0.5
D_FF = 2816
S5_WIDTH = 256
S5_GROUP = 16
S5_GROUPS = 16
S5_STATE = 64
CONV_WIDTH = 256
CONV_K = 31
POOL_WIDTH = 256
POOL_WINDOWS = (2, 4, 8, 16)
POOL_GROUP = 64
MLA_HEADS = 8
MLA_NOPE = 64
MLA_ROPE = 32
MLA_V = 64
MLA_Q_RANK = 256
MLA_KV_RANK = 128
MLA_WIDTH = MLA_HEADS * MLA_V
ROPE_BASE = 10000.0
D_MIX = S5_WIDTH + CONV_WIDTH + POOL_WIDTH + MLA_WIDTH
IN_S5 = 0
IN_CONV = IN_S5 + S5_WIDTH
IN_POOL = IN_CONV + 2 * CONV_WIDTH
IN_CQ = IN_POOL + POOL_WIDTH
IN_CKV = IN_CQ + MLA_Q_RANK
IN_KR = IN_CKV + MLA_KV_RANK
D_IN = IN_KR + MLA_ROPE

LANES = 128
HEAD_PAD = 128
S5_CHUNK = 16
S5_FLAT = S5_CHUNK * S5_GROUP
HALO = 16
D_IN_EXT = IN_KR + 2 * HEAD_PAD
FF_CHUNKS = ((0, 1024), (1024, 2048), (2048, 2816))
VMEM_LIMIT = 56 * 1024 * 1024
QK_ROWS = 128
ATTN_AHEAD = 1
DENOM_ROWS = 16
VT_ROWS = MLA_V + DENOM_ROWS
Q_SCALE = (MLA_NOPE + MLA_ROPE) ** -0.5 * math.log2(math.e)


def _cparams(*sem, flags=None):
    return pltpu.CompilerParams(dimension_semantics=sem, vmem_limit_bytes=VMEM_LIMIT, flags=flags)


def _const_spec(shape):
    nd = len(shape)
    return pl.BlockSpec(shape, lambda *_: (0,) * nd, pipeline_mode=pl.Buffered(1))


def _rms(x, g):
    return x * lax.rsqrt(jnp.mean(x * x, axis=-1, keepdims=True) + EPS) * g


def _dot(a, b):
    return jnp.dot(a, b, preferred_element_type=F32)


def _mod_kernel(c_ref, w_ref, b_ref, o_ref):
    c = c_ref[...]
    h = (c * jax.nn.sigmoid(c)).astype(BF16)
    o_ref[0] = _dot(h, w_ref[0].astype(BF16)) + b_ref[0]


def _modulation(c_rows, w_ada, b_ada):
    depth = w_ada.shape[0]
    d = D_MODEL
    return pl.pallas_call(
        _mod_kernel,
        grid=(depth, N_MOD),
        in_specs=[
            pl.BlockSpec((8, d), lambda l, j: (0, 0)),
            pl.BlockSpec((1, d, d), lambda l, j: (l, 0, j)),
            pl.BlockSpec((1, 1, d), lambda l, j: (l, 0, j)),
        ],
        out_specs=pl.BlockSpec((1, 8, d), lambda l, j: (l, 0, j)),
        out_shape=jax.ShapeDtypeStruct((depth, 8, N_MOD * d), F32),
        compiler_params=_cparams("arbitrary", "arbitrary"),
        name="adaln_mod",
    )(c_rows, w_ada, b_ada.reshape(depth, 1, N_MOD * d))


def _mod_slices(mod_ref, base):
    d = D_MODEL
    return (mod_ref[0, :, base * d:(base + 1) * d], mod_ref[0, :, (base + 1) * d:(base + 2) * d],
            mod_ref[0, :, (base + 2) * d:(base + 3) * d])


def _ffn_kernel(x_ref, mod_ref, pre_ref, post_ref, wg_ref, wu_ref, wd_ref, o_ref, *, base):
    x = x_ref[0]
    shift, scale, gate = _mod_slices(mod_ref, base)
    h = (_rms(x, pre_ref[...]) * (1.0 + scale) + shift).astype(BF16)
    acc = None
    for lo, hi in FF_CHUNKS:
        g = _dot(h, wg_ref[:, lo:hi])
        u = _dot(h, wu_ref[:, lo:hi])
        a = (g * jax.nn.sigmoid(g) * u).astype(BF16)
        part = _dot(a, wd_ref[lo:hi, :])
        acc = part if acc is None else acc + part
    o_ref[0] = x + MACARON_WEIGHT * gate * _rms(acc, post_ref[...])


def _ffn(x, mod_rows, base, pre_g, post_g, wg, wu, wd, tm):
    b, s, d = x.shape
    per_batch = mod_rows.shape[0] > 1
    return pl.pallas_call(
        functools.partial(_ffn_kernel, base=base),
        grid=(b, s // tm),
        in_specs=[
            pl.BlockSpec((1, tm, d), lambda i, j: (i, j, 0)),
            pl.BlockSpec((1, 1, N_MOD * d), (lambda i, j: (i, 0, 0)) if per_batch else (lambda i, j: (0, 0, 0))),
            _const_spec((1, d)), _const_spec((1, d)),
            _const_spec((d, D_FF)), _const_spec((d, D_FF)), _const_spec((D_FF, d)),
        ],
        out_specs=pl.BlockSpec((1, tm, d), lambda i, j: (i, j, 0)),
        out_shape=jax.ShapeDtypeStruct(x.shape, F32),
        compiler_params=_cparams("arbitrary", "arbitrary"),
        name="ffn_half_step",
    )(x, mod_rows, pre_g.reshape(1, d), post_g.reshape(1, d), wg, wu, wd)


def _inproj_kernel(x_ref, mod_ref, pre_ref, win_ref, qn_ref, kvn_ref, wq_ref, wk_ref, wv_ref, place_ref,
                   cos_ref, sin_ref, u_ref, zc_ref, zp_ref, q_ref, k_ref, v_ref):
    x = x_ref[0]
    shift, scale, _ = _mod_slices(mod_ref, 3)
    h = (_rms(x, pre_ref[...]) * (1.0 + scale) + shift).astype(BF16)
    z = _dot(h, win_ref[...])
    u_ref[0] = z[:, IN_S5:IN_CONV]
    zc_ref[0] = z[:, IN_CONV:IN_POOL]
    zp_ref[0] = z[:, IN_POOL:IN_CQ]
    cos = cos_ref[...]
    sin = sin_ref[...]
    cos_h = jnp.concatenate([cos] * MLA_HEADS, axis=1)
    sin_h = jnp.concatenate([sin] * MLA_HEADS, axis=1)
    cqn = _rms(z[:, IN_CQ:IN_CKV], qn_ref[...]).astype(BF16)
    qq = _dot(cqn, wq_ref[...])
    hw = MLA_HEADS * HEAD_PAD
    q = qq[:, :hw] * cos_h + qq[:, hw:] * sin_h
    q_ref[0] = (q * Q_SCALE).astype(BF16)
    ckvn = _rms(z[:, IN_CKV:IN_KR], kvn_ref[...]).astype(BF16)
    kr = z[:, IN_KR:IN_KR + HEAD_PAD] * cos + z[:, IN_KR + HEAD_PAD:IN_KR + 2 * HEAD_PAD] * sin
    k = _dot(ckvn, wk_ref[...]) + _dot(kr.astype(BF16), place_ref[...])
    k_ref[0] = k.astype(BF16)
    v_ref[0] = _dot(ckvn, wv_ref[...]).astype(BF16)


def _inproj(x, mod_rows, pre_g, lw, cos_t, sin_t, tm):
    b, s, d = x.shape
    per_batch = mod_rows.shape[0] > 1
    hw = MLA_HEADS * HEAD_PAD
    row = lambda w: pl.BlockSpec((1, tm, w), lambda i, j: (i, j, 0))
    return pl.pallas_call(
        _inproj_kernel,
        grid=(b, s // tm),
        in_specs=[
            row(d),
            pl.BlockSpec((1, 1, N_MOD * d), (lambda i, j: (i, 0, 0)) if per_batch else (lambda i, j: (0, 0, 0))),
            _const_spec((1, d)),
            _const_spec((d, D_IN_EXT)),
            _const_spec((1, MLA_Q_RANK)), _const_spec((1, MLA_KV_RANK)),
            _const_spec((MLA_Q_RANK, 2 * hw)), _const_spec((MLA_KV_RANK, hw)),
            _const_spec((MLA_KV_RANK, MLA_WIDTH)), _const_spec((HEAD_PAD, hw)),
            pl.BlockSpec((tm, HEAD_PAD), lambda i, j: (j, 0)),
            pl.BlockSpec((tm, HEAD_PAD), lambda i, j: (j, 0)),
        ],
        out_specs=[row(S5_WIDTH), row(2 * CONV_WIDTH), row(POOL_WIDTH), row(hw), row(hw), row(MLA_WIDTH)],
        out_shape=[
            jax.ShapeDtypeStruct((b, s, S5_WIDTH), F32),
            jax.ShapeDtypeStruct((b, s, 2 * CONV_WIDTH), F32),
            jax.ShapeDtypeStruct((b, s, POOL_WIDTH), F32),
            jax.ShapeDtypeStruct((b, s, hw), BF16),
            jax.ShapeDtypeStruct((b, s, hw), BF16),
            jax.ShapeDtypeStruct((b, s, MLA_WIDTH), BF16),
        ],
        compiler_params=_cparams("arbitrary", "arbitrary"),
        name="mixer_in_proj",
    )(x, mod_rows, pre_g.reshape(1, d), lw["w_in"], lw["q_norm"], lw["kv_norm"], lw["wq"], lw["wk"], lw["wv"],
      lw["place"], cos_t, sin_t)


def _s5_state_kernel(u_ref, m_ref, s_ref):
    s_ref[0, 0] = _dot(u_ref[0, 0], m_ref[0])


def _s5_chunk_states(u_flat, m_state):
    b, g, ct, w = u_flat.shape
    return pl.pallas_call(
        _s5_state_kernel,
        grid=(b, g),
        in_specs=[pl.BlockSpec((1, 1, ct, w), lambda i, j: (i, j, 0, 0)),
                  pl.BlockSpec((1, w, w), lambda i, j: (j, 0, 0))],
        out_specs=pl.BlockSpec((1, 1, ct, w), lambda i, j: (i, j, 0, 0)),
        out_shape=jax.ShapeDtypeStruct((b, g, ct, w), F32),
        compiler_params=_cparams("arbitrary", "arbitrary"),
        name="s5_chunk_states",
    )(u_flat, m_state)


def _s5_scan_kernel(sc_ref, sf_ref, sb_ref, a_ref, hc_ref, hf_ref, hb_ref, st_ref):
    half = LANES // 2
    a_re = a_ref[:, :LANES]
    a_im = a_ref[:, LANES:]
    fwd_lane = lax.broadcasted_iota(jnp.int32, a_re.shape, 1) < half

    def swap(v):
        return pltpu.roll(v, half, 1)

    def run(n, load_f, load_b, store_f, store_b, carry):
        def body(i, c):
            h_re, h_im = c
            r = n - 1 - i
            sf = load_f(i)
            sb = load_b(r)
            store_f(i, jnp.where(fwd_lane, h_re, swap(h_im)))
            store_b(r, jnp.where(fwd_lane, swap(h_re), h_im))
            s_re = jnp.where(fwd_lane, sf, swap(sb))
            s_im = jnp.where(fwd_lane, swap(sf), sb)
            return (a_re * h_re - a_im * h_im + s_re, a_re * h_im + a_im * h_re + s_im)
        return lax.fori_loop(0, n, body, carry)

    @pl.when(pl.program_id(1) == 0)
    def _():
        def st_f(i, v):
            hc_ref[0, i, :, 0:LANES] = v

        def st_b(i, v):
            hc_ref[0, i, :, LANES:2 * LANES] = v

        zero = jnp.zeros(a_re.shape, F32)
        c = run(sc_ref.shape[1], lambda i: sc_ref[0, i, :, 0:LANES], lambda i: sc_ref[0, i, :, LANES:2 * LANES],
                st_f, st_b, (zero, zero))
        st_ref[0] = c[0]
        st_ref[1] = c[1]

    def st_f(i, v):
        hf_ref[0, i] = v

    def st_b(i, v):
        hb_ref[0, i] = v

    c = run(sf_ref.shape[1], lambda i: sf_ref[0, i], lambda i: sb_ref[0, i], st_f, st_b, (st_ref[0], st_ref[1]))
    st_ref[0] = c[0]
    st_ref[1] = c[1]


def _s5_scan(s_ctx, s_lat, a16):
    b, nc, g, w = s_ctx.shape
    c = s_lat.shape[1]
    cb = min(128, c)
    nb = c // cb
    return pl.pallas_call(
        _s5_scan_kernel,
        grid=(b, nb),
        in_specs=[pl.BlockSpec((1, nc, g, w), lambda i, j: (i, 0, 0, 0)),
                  pl.BlockSpec((1, cb, g, LANES), lambda i, j: (i, j, 0, 0)),
                  pl.BlockSpec((1, cb, g, LANES), lambda i, j: (i, nb - 1 - j, 0, 1)),
                  pl.BlockSpec((g, w), lambda i, j: (0, 0))],
        out_specs=[pl.BlockSpec((1, nc, g, w), lambda i, j: (i, 0, 0, 0)),
                   pl.BlockSpec((1, cb, g, LANES), lambda i, j: (i, j, 0, 0)),
                   pl.BlockSpec((1, cb, g, LANES), lambda i, j: (i, nb - 1 - j, 0, 0))],
        out_shape=[jax.ShapeDtypeStruct((b, nc, g, w), F32), jax.ShapeDtypeStruct((b, c, g, LANES), F32),
                   jax.ShapeDtypeStruct((b, c, g, LANES), F32)],
        scratch_shapes=[pltpu.VMEM((2, g, LANES), F32)],
        compiler_params=_cparams("arbitrary", "arbitrary"),
        name="s5_chunk_scan",
    )(s_ctx, s_lat, s_lat, a16)


def _s5_out_kernel(u_ref, h_ref, mi_ref, mo_ref, y_ref):
    y_ref[0, 0] = _dot(u_ref[0, 0], mi_ref[0]) + _dot(h_ref[0, 0], mo_ref[0])


def _s5_outputs(u_flat, h_in, m_intra, m_out):
    b, g, ct, w = u_flat.shape
    blk = pl.BlockSpec((1, 1, ct, w), lambda i, j: (i, j, 0, 0))
    mat = pl.BlockSpec((1, w, w), lambda i, j: (j, 0, 0))
    return pl.pallas_call(
        _s5_out_kernel,
        grid=(b, g),
        in_specs=[blk, blk, mat, mat],
        out_specs=blk,
        out_shape=jax.ShapeDtypeStruct((b, g, ct, w), F32),
        compiler_params=_cparams("arbitrary", "arbitrary"),
        name="s5_chunk_outputs",
    )(u_flat, h_in, m_intra, m_out)


def _s5_matrices(lam_re, lam_im, log_dt, b_re, b_im, c_re, c_im):
    hp = lax.Precision.HIGHEST
    t = S5_CHUNK
    g, p, h = S5_GROUPS, S5_STATE, S5_GROUP
    lam = lax.complex(jnp.minimum(lam_re, -1e-4), lam_im)
    dt = jnp.exp(log_dt)[..., None]
    steps = jnp.arange(t + 1, dtype=F32)[:, None, None, None]
    apow = jnp.exp(lam[None] * dt[None] * steps)
    bb = ((apow[1] - 1.0) / lam)[..., None] * lax.complex(b_re, b_im)
    cc = lax.complex(c_re, c_im)
    taps = jnp.einsum('dghp,jdgp,dgpk->djghk', cc, apow[:t], bb, precision=hp).real
    s_idx = jnp.arange(t)[:, None]
    t_idx = jnp.arange(t)[None, :]
    kf = taps[0][jnp.clip(t_idx - s_idx, 0, t - 1)] * (t_idx >= s_idx)[:, :, None, None, None].astype(F32)
    kb = taps[1][jnp.clip(s_idx - t_idx, 0, t - 1)] * (s_idx >= t_idx)[:, :, None, None, None].astype(F32)
    m_intra = (kf + kb).transpose(2, 0, 4, 1, 3).reshape(g, t * h, t * h)
    wf = apow[:t][::-1, 0][:, :, :, None] * bb[0][None]
    wb = apow[:t, 1][:, :, :, None] * bb[1][None]
    to_rows = lambda w: w.transpose(1, 0, 3, 2).reshape(g, t * h, p)
    m_state = jnp.concatenate([to_rows(wf.real), to_rows(wf.imag), to_rows(wb.real), to_rows(wb.imag)], axis=-1)
    of = cc[0][None] * apow[1:, 0][:, :, None, :]
    ob = cc[1][None] * apow[1:, 1][::-1][:, :, None, :]
    to_cols = lambda w: w.transpose(1, 3, 0, 2).reshape(g, p, t * h)
    m_out = jnp.concatenate([to_cols(of.real), -to_cols(of.imag), to_cols(ob.real), -to_cols(ob.imag)], axis=1)
    a_t = apow[t]
    a16 = jnp.concatenate([a_t[0].real, a_t[1].real, a_t[0].imag, a_t[1].imag], axis=-1)
    return m_intra.astype(BF16), m_state.astype(BF16), m_out.astype(BF16), a16


def _s5_mixer(u_ctx, u_lat, mats):
    m_intra, m_state, m_out, a16 = mats
    b, lc, _ = u_ctx.shape
    g, h, t = S5_GROUPS, S5_GROUP, S5_CHUNK

    def flat(u):
        n = u.shape[1] // t
        return u.astype(BF16).reshape(b, n, t, g, h).transpose(0, 3, 1, 2, 4).reshape(b, g, n, t * h)

    u_flat = jnp.concatenate([flat(u_ctx), flat(u_lat)], axis=2)
    ct = u_flat.shape[2]
    nc = lc // t
    s_loc = _s5_chunk_states(u_flat, m_state).transpose(0, 2, 1, 3)
    h_ctx, h_f, h_b = _s5_scan(s_loc[:, :nc], s_loc[:, nc:], a16)
    h_in = jnp.concatenate([h_ctx, jnp.concatenate([h_f, h_b], axis=-1)], axis=1)
    y = _s5_outputs(u_flat, h_in.transpose(0, 2, 1, 3).astype(BF16), m_intra, m_out)
    y = y.reshape(b, g, ct, t, h).transpose(0, 2, 3, 1, 4).reshape(b, ct * t, g * h)
    return y[:, :lc], y[:, lc:]


def _local_kernel(zc_ref, zcp_ref, zcn_ref, zp_ref, zpp_ref, zpn_ref, cw_ref, cb_ref, lg_ref, lb_ref, pw_ref, ps_ref,
                  conv_ref, pool_ref, ce_ref, pe_ref, *, seq_len):
    tm = zc_ref.shape[1]
    j = pl.program_id(1)
    first = j == 0
    last = j == pl.num_programs(1) - 1

    def glu(z):
        return z[:, :CONV_WIDTH] * jax.nn.sigmoid(z[:, CONV_WIDTH:])

    ce_ref[0:HALO] = jnp.where(first, 0.0, glu(zcp_ref[0]))
    ce_ref[HALO:HALO + tm] = glu(zc_ref[0])
    ce_ref[HALO + tm:] = jnp.where(last, 0.0, glu(zcn_ref[0]))
    pe_ref[0:HALO] = jnp.where(first, 0.0, zpp_ref[0])
    pe_ref[HALO:HALO + tm] = zp_ref[0]
    pe_ref[HALO + tm:] = jnp.where(last, 0.0, zpn_ref[0])

    acc = None
    for k in range(CONV_K):
        term = ce_ref[pl.ds(HALO + k - CONV_K // 2, tm), :] * cw_ref[k:k + 1, :]
        acc = term if acc is None else acc + term
    acc = acc + cb_ref[...]
    mu = jnp.mean(acc, axis=-1, keepdims=True)
    xc = acc - mu
    var = jnp.mean(xc * xc, axis=-1, keepdims=True)
    y = xc * lax.rsqrt(var + EPS) * lg_ref[...] + lb_ref[...]
    conv_ref[0] = (y * jax.nn.sigmoid(y)).astype(BF16)

    def tap(dlt):
        return pe_ref[pl.ds(HALO + dlt, tm), :]

    z = tap(0)
    sums = {}
    run = z + tap(1)
    sums[2] = run
    run = run + tap(-1) + tap(2)
    sums[4] = run
    run = run + tap(-3) + tap(-2) + tap(3) + tap(4)
    sums[8] = run
    for dlt in (-7, -6, -5, -4, 5, 6, 7, 8):
        run = run + tap(dlt)
    sums[16] = run
    pos = j * tm + lax.broadcasted_iota(jnp.int32, (tm, 1), 0)
    lane = lax.broadcasted_iota(jnp.int32, (tm, POOL_WIDTH), 1)
    mean = None
    for gi, w in enumerate(POOL_WINDOWS):
        lo = jnp.maximum(pos - (w - 1) // 2, 0)
        hi = jnp.minimum(pos + w // 2, seq_len - 1)
        m = sums[w] / (hi - lo + 1).astype(F32)
        mean = m if mean is None else jnp.where(lane >= gi * POOL_GROUP, m, mean)
    d = (mean - z).astype(BF16)
    pool_ref[0] = (_dot(d, pw_ref[...]) * ps_ref[...]).astype(BF16)


def _local_mixers(zc, zp, lw, tm):
    b, s, _ = zc.shape
    nh = tm // HALO
    nblk = s // HALO
    main = lambda w: pl.BlockSpec((1, tm, w), lambda i, j: (i, j, 0))
    prev = lambda w: pl.BlockSpec((1, HALO, w), lambda i, j: (i, jnp.maximum(j * nh - 1, 0), 0))
    nxt = lambda w: pl.BlockSpec((1, HALO, w), lambda i, j: (i, jnp.minimum((j + 1) * nh, nblk - 1), 0))
    cw2, pw = 2 * CONV_WIDTH, POOL_WIDTH
    return pl.pallas_call(
        functools.partial(_local_kernel, seq_len=s),
        grid=(b, s // tm),
        in_specs=[main(cw2), prev(cw2), nxt(cw2), main(pw), prev(pw), nxt(pw),
                  _const_spec((32, CONV_WIDTH)), _const_spec((1, CONV_WIDTH)), _const_spec((1, CONV_WIDTH)),
                  _const_spec((1, CONV_WIDTH)), _const_spec((pw, pw)), _const_spec((1, pw))],
        out_specs=[main(CONV_WIDTH), main(pw)],
        out_shape=[jax.ShapeDtypeStruct((b, s, CONV_WIDTH), BF16), jax.ShapeDtypeStruct((b, s, pw), BF16)],
        scratch_shapes=[pltpu.VMEM((tm + 2 * HALO, CONV_WIDTH), F32), pltpu.VMEM((tm + 2 * HALO, pw), F32)],
        compiler_params=_cparams("arbitrary", "arbitrary"),
        name="conv_pool",
    )(zc, zc, zc, zp, zp, zp, lw["conv_w"], lw["conv_b"], lw["conv_ln_g"], lw["conv_ln_b"], lw["pool_w"],
      lw["pool_scale"])


def _attn_kernel(*refs, tk, has_lat):
    if has_lat:
        q_ref, kc_ref, vtc_ref, kl_ref, vtl_ref, o_ref, s_ref = refs
    else:
        q_ref, kc_ref, vtc_ref, o_ref = refs
    tq = q_ref.shape[1]
    heads = range(2)
    qs = [q_ref[0, :, h * HEAD_PAD:(h + 1) * HEAD_PAD] for h in heads]

    def scores(h, k):
        return lax.dot_general(k, qs[h], (((1,), (1,)), ((), ())), preferred_element_type=F32)

    def consume(s, mt, vt, state):
        m, l, acc = state
        m_new = jnp.maximum(m, mt)
        p = jnp.exp2(s - m_new)
        alpha = jnp.exp2(m - m_new)
        l = alpha * l + jnp.sum(p, axis=0, keepdims=True)
        acc = alpha * acc + _dot(vt, p.astype(BF16))
        return m_new, l, acc

    init = (jnp.full((1, tq), -1e30, F32), jnp.zeros((1, tq), F32), jnp.zeros((MLA_V, tq), F32))
    state = []
    for h in heads:
        s = scores(h, kc_ref[0, :, h * HEAD_PAD:(h + 1) * HEAD_PAD])
        state.append(consume(s, jnp.max(s, axis=0, keepdims=True), vtc_ref[0, h * MLA_V:(h + 1) * MLA_V, :], init))

    if has_lat:
        n_tiles = kl_ref.shape[1] // tk

        def produce(h, slot, i):
            off = pl.multiple_of(i * tk, tk)
            s = scores(h, kl_ref[0, pl.ds(off, tk), h * HEAD_PAD:(h + 1) * HEAD_PAD])
            s_ref[h, slot] = s
            return jnp.max(s, axis=0, keepdims=True)

        def use(h, slot, i, mt, st):
            off = pl.multiple_of(i * tk, tk)
            return consume(s_ref[h, slot], mt, vtl_ref[0, h * MLA_V:(h + 1) * MLA_V, pl.ds(off, tk)], st)

        def body(j, carry):
            st, mt0 = carry
            i0 = 2 * j
            mt1 = [produce(h, 1, i0 + 1) for h in heads]
            st = [use(h, 0, i0, mt0[h], st[h]) for h in heads]
            nxt = jnp.minimum(i0 + 2, n_tiles - 1)
            mt0 = [produce(h, 0, nxt) for h in heads]
            st = [use(h, 1, i0 + 1, mt1[h], st[h]) for h in heads]
            return st, mt0

        mt0 = [produce(h, 0, 0) for h in heads]
        state, _ = lax.fori_loop(0, n_tiles // 2, body, (state, mt0))
    o_ref[0] = jnp.concatenate([acc / l for _, l, acc in state], axis=0).astype(BF16)


def _attention(q, k_ctx, vt_ctx, k_lat, vt_lat, tq, tk):
    b, lq, _ = q.shape
    lc = k_ctx.shape[1]
    has_lat = k_lat is not None
    in_specs = [pl.BlockSpec((1, tq, 2 * HEAD_PAD), lambda i, p, j: (i, j, p)),
                pl.BlockSpec((1, lc, 2 * HEAD_PAD), lambda i, p, j: (i, 0, p)),
                pl.BlockSpec((1, 2 * MLA_V, lc), lambda i, p, j: (i, p, 0))]
    args = [q, k_ctx, vt_ctx]
    scratch = []
    if has_lat:
        ll = k_lat.shape[1]
        assert ll % (2 * tk) == 0
        in_specs += [pl.BlockSpec((1, ll, 2 * HEAD_PAD), lambda i, p, j: (i, 0, p)),
                     pl.BlockSpec((1, 2 * MLA_V, ll), lambda i, p, j: (i, p, 0))]
        args += [k_lat, vt_lat]
        scratch = [pltpu.VMEM((2, 2, tk, tq), F32)]
    return pl.pallas_call(
        functools.partial(_attn_kernel, tk=tk, has_lat=has_lat),
        grid=(b, MLA_HEADS // 2, lq // tq),
        in_specs=in_specs,
        out_specs=pl.BlockSpec((1, 2 * MLA_V, tq), lambda i, p, j: (i, p, j)),
        out_shape=jax.ShapeDtypeStruct((b, MLA_WIDTH, lq), BF16),
        scratch_shapes=scratch,
        compiler_params=_cparams("arbitrary", "arbitrary", "arbitrary"),
        name="mla_attention",
    )(*args)


def _mixout_kernel(x_ref, mod_ref, post_ref, y5_ref, u_ref, d_ref, wglu_ref, bglu_ref, conv_ref, pool_ref, att_ref,
                   wout_ref, o_ref):
    x = x_ref[0]
    _, _, gate = _mod_slices(mod_ref, 3)
    y5 = jax.nn.gelu(y5_ref[0] + d_ref[...] * u_ref[0])
    s5 = y5 * jax.nn.sigmoid(_dot(y5.astype(BF16), wglu_ref[...]) + bglu_ref[...])
    o1, o2, o3 = S5_WIDTH, S5_WIDTH + CONV_WIDTH, S5_WIDTH + CONV_WIDTH + POOL_WIDTH
    y = (_dot(s5.astype(BF16), wout_ref[0:o1, :]) + _dot(conv_ref[0], wout_ref[o1:o2, :])
         + _dot(pool_ref[0], wout_ref[o2:o3, :]) + _dot(att_ref[0], wout_ref[o3:, :]))
    o_ref[0] = x + gate * _rms(y, post_ref[...])


def _mix_out(x, mod_rows, post_g, y5, u, conv, pool, att, lw, tm):
    b, s, d = x.shape
    per_batch = mod_rows.shape[0] > 1
    row = lambda w: pl.BlockSpec((1, tm, w), lambda i, j: (i, j, 0))
    return pl.pallas_call(
        _mixout_kernel,
        grid=(b, s // tm),
        in_specs=[
            row(d),
            pl.BlockSpec((1, 1, N_MOD * d), (lambda i, j: (i, 0, 0)) if per_batch else (lambda i, j: (0, 0, 0))),
            _const_spec((1, d)),
            row(S5_WIDTH), row(S5_WIDTH), _const_spec((1, S5_WIDTH)), _const_spec((S5_WIDTH, S5_WIDTH)),
            _const_spec((1, S5_WIDTH)),
            row(CONV_WIDTH), row(POOL_WIDTH), row(MLA_WIDTH), _const_spec((D_MIX, d)),
        ],
        out_specs=row(d),
        out_shape=jax.ShapeDtypeStruct(x.shape, F32),
        compiler_params=_cparams("arbitrary", "arbitrary"),
        name="mixer_out_proj",
    )(x, mod_rows, post_g.reshape(1, d), y5, u, lw["s5_d"], lw["s5_w_glu"], lw["s5_b_glu"], conv, pool, att,
      lw["w_out"])


def _rope_partner(w):
    q = MLA_ROPE // 4
    return jnp.concatenate([-w[..., q:2 * q], w[..., 0:q], -w[..., 3 * q:4 * q], w[..., 2 * q:3 * q]], axis=-1)


def _rope_tables(n_lat):
    rows = n_lat // GRID_W
    axis_dim = MLA_ROPE // 2
    inv = ROPE_BASE ** (-jnp.arange(0, axis_dim, 2, dtype=F32) / axis_dim)
    ang_r = jnp.arange(rows, dtype=F32)[:, None] * inv
    ang_c = jnp.arange(GRID_W, dtype=F32)[:, None] * inv
    nq = MLA_ROPE // 4

    def table(fn, fill):
        by_row = jnp.broadcast_to(fn(ang_r)[:, None, :], (rows, GRID_W, nq))
        by_col = jnp.broadcast_to(fn(ang_c)[None, :, :], (rows, GRID_W, nq))
        head = jnp.full((rows, GRID_W, MLA_NOPE), fill, F32)
        tail = jnp.full((rows, GRID_W, HEAD_PAD - MLA_NOPE - MLA_ROPE), fill, F32)
        return jnp.concatenate([head, by_row, by_row, by_col, by_col, tail], axis=-1).reshape(n_lat, HEAD_PAD)

    return table(jnp.cos, 1.0), table(jnp.sin, 0.0)


def _pad_head(nope, rope):
    r = nope.shape[0]
    z = jnp.zeros((r, MLA_HEADS, HEAD_PAD - MLA_NOPE - MLA_ROPE), nope.dtype)
    return jnp.concatenate([nope, rope, z], axis=-1).reshape(r, MLA_HEADS * HEAD_PAD)


def _prep_layer(l, w_in, w_out, s5_d, s5_w_glu, s5_b_glu, conv_w, conv_b, conv_ln_g, conv_ln_b, pool_w, pool_scale,
                mla_q_norm, mla_w_uq, mla_kv_norm, mla_w_ukv):
    d = D_MODEL
    wi = w_in[l]
    kr = wi[:, IN_KR:D_IN]
    zpad_l = jnp.zeros((d, MLA_NOPE), F32)
    zpad_r = jnp.zeros((d, HEAD_PAD - MLA_NOPE - MLA_ROPE), F32)
    w_in_ext = jnp.concatenate([wi[:, :IN_KR], zpad_l, kr, zpad_r, zpad_l, _rope_partner(kr), zpad_r], axis=1)
    uq = mla_w_uq[l].reshape(MLA_Q_RANK, MLA_HEADS, MLA_NOPE + MLA_ROPE)
    q_nope, q_rope = uq[..., :MLA_NOPE], uq[..., MLA_NOPE:]
    wq = jnp.concatenate([_pad_head(q_nope, q_rope), _pad_head(0 * q_nope, _rope_partner(q_rope))], axis=1)
    ukv = mla_w_ukv[l].reshape(MLA_KV_RANK, MLA_HEADS, MLA_NOPE + MLA_V)
    wk = _pad_head(ukv[..., :MLA_NOPE], jnp.zeros((MLA_KV_RANK, MLA_HEADS, MLA_ROPE), F32))
    wv = ukv[..., MLA_NOPE:].reshape(MLA_KV_RANK, MLA_WIDTH)
    eye = jnp.eye(HEAD_PAD, dtype=F32) * ((jnp.arange(HEAD_PAD) >= MLA_NOPE)
                                          & (jnp.arange(HEAD_PAD) < MLA_NOPE + MLA_ROPE))[:, None]
    place = jnp.tile(eye, (1, MLA_HEADS))
    pool_bd = jnp.zeros((POOL_WIDTH, POOL_WIDTH), F32)
    for gi in range(len(POOL_WINDOWS)):
        sl = slice(gi * POOL_GROUP, (gi + 1) * POOL_GROUP)
        pool_bd = pool_bd.at[sl, sl].set(pool_w[l, gi])
    return {
        "w_in": w_in_ext.astype(BF16), "w_out": w_out[l].astype(BF16),
        "q_norm": mla_q_norm[l].reshape(1, -1), "kv_norm": mla_kv_norm[l].reshape(1, -1),
        "wq": wq.astype(BF16), "wk": wk.astype(BF16), "wv": wv.astype(BF16), "place": place.astype(BF16),
        "s5_d": s5_d[l].reshape(1, -1), "s5_w_glu": s5_w_glu[l].astype(BF16), "s5_b_glu": s5_b_glu[l].reshape(1, -1),
        "conv_w": jnp.concatenate([conv_w[l], jnp.zeros((1, CONV_WIDTH), F32)], axis=0),
        "conv_b": conv_b[l].reshape(1, -1), "conv_ln_g": conv_ln_g[l].reshape(1, -1),
        "conv_ln_b": conv_ln_b[l].reshape(1, -1),
        "pool_w": pool_bd.astype(BF16), "pool_scale": pool_scale[l].reshape(1, -1),
    }


def kernel(x, c, ctx, c_ctx, w_ada, b_ada, norm_pre, norm_post, ffn_w_gate, ffn_w_up, ffn_w_down, w_in, w_out,
           s5_lam_re, s5_lam_im, s5_log_dt, s5_b_re, s5_b_im, s5_c_re, s5_c_im, s5_d, s5_w_glu, s5_b_glu, conv_w,
           conv_b, conv_ln_g, conv_ln_b, pool_w, pool_scale, mla_q_norm, mla_w_uq, mla_kv_norm, mla_w_ukv):
    batch, n_lat, d = x.shape
    n_ctx = ctx.shape[1]
    depth = w_ada.shape[0]
    assert d == D_MODEL and batch + 1 <= 8
    tm_lat = min(512, n_lat)
    tm_ctx = min(512, n_ctx)
    tq_lat = min(1024, n_lat)
    tk = min(1024, n_lat // 2)
    assert n_lat % tm_lat == 0 and n_ctx % S5_CHUNK == 0 and n_lat % GRID_W == 0

    c_rows = jnp.zeros((8, d), F32).at[:batch].set(c).at[batch].set(c_ctx)
    mod = _modulation(c_rows, w_ada, b_ada)
    cos_lat, sin_lat = _rope_tables(n_lat)
    cos_ctx = jnp.ones((n_ctx, HEAD_PAD), F32)
    sin_ctx = jnp.zeros((n_ctx, HEAD_PAD), F32)
    wg = ffn_w_gate.astype(BF16)
    wu = ffn_w_up.astype(BF16)
    wd = ffn_w_down.astype(BF16)

    x_lat, x_ctx = x, ctx
    for l in range(depth):
        last = l == depth - 1
        mod_lat = mod[l, :batch].reshape(batch, 1, N_MOD * d)
        mod_ctx = mod[l, batch].reshape(1, 1, N_MOD * d)
        lw = _prep_layer(l, w_in, w_out, s5_d, s5_w_glu, s5_b_glu, conv_w, conv_b, conv_ln_g, conv_ln_b, pool_w,
                         pool_scale, mla_q_norm, mla_w_uq, mla_kv_norm, mla_w_ukv)
        s5_mats = _s5_matrices(s5_lam_re[l], s5_lam_im[l], s5_log_dt[l], s5_b_re[l], s5_b_im[l], s5_c_re[l],
                               s5_c_im[l])
        x_lat = _ffn(x_lat, mod_lat, 0, norm_pre[l, 0], norm_post[l, 0], wg[l, 0], wu[l, 0], wd[l, 0], tm_lat)
        x_ctx = _ffn(x_ctx, mod_ctx, 0, norm_pre[l, 0], norm_post[l, 0], wg[l, 0], wu[l, 0], wd[l, 0], tm_ctx)
        u_l, zc_l, zp_l, q_l, k_l, v_l = _inproj(x_lat, mod_lat, norm_pre[l, 1], lw, cos_lat, sin_lat, tm_lat)
        u_c, zc_c, zp_c, q_c, k_c, v_c = _inproj(x_ctx, mod_ctx, norm_pre[l, 1], lw, cos_ctx, sin_ctx, tm_ctx)
        y5_c, y5_l = _s5_mixer(u_c, u_l, s5_mats)
        conv_l, pool_l = _local_mixers(zc_l, zp_l, lw, tm_lat)
        vt_l, vt_c = v_l.transpose(0, 2, 1), v_c.transpose(0, 2, 1)
        att_l = _attention(q_l, k_c, vt_c, k_l, vt_l, tq_lat, tk).transpose(0, 2, 1)
        x_lat = _mix_out(x_lat, mod_lat, norm_post[l, 1], y5_l, u_l, conv_l, pool_l, att_l, lw, tm_lat)
        if not last:
            conv_c, pool_c = _local_mixers(zc_c, zp_c, lw, tm_ctx)
            att_c = _attention(q_c, k_c, vt_c, None, None, tm_ctx, tk).transpose(0, 2, 1)
            x_ctx = _mix_out(x_ctx, mod_ctx, norm_post[l, 1], y5_c, u_c, conv_c, pool_c, att_c, lw, tm_ctx)
        x_lat = _ffn(x_lat, mod_lat, 6, norm_pre[l, 2], norm_post[l, 2], wg[l, 1], wu[l, 1], wd[l, 1], tm_lat)
        if not last:
            x_ctx = _ffn(x_ctx, mod_ctx, 6, norm_pre[l, 2], norm_post[l, 2], wg[l, 1], wu[l, 1], wd[l, 1], tm_ctx)
    return x_lat
```

```python
import functools
import math

import jax
import jax.numpy as jnp
from jax import lax
from jax.experimental import pallas as pl
from jax.experimental.pallas import tpu as pltpu

F32 = jnp.float32
BF16 = jnp.bfloat16

D_MODEL = 1024
GRID_W = 64
EPS = 1e-6
N_MOD = 9
MACARON_WEIGHT = 0.5
D_FF = 2816
S5_WIDTH = 256
S5_GROUP = 16
S5_GROUPS = 16
S5_STATE = 64
CONV_WIDTH = 256
CONV_K = 31
POOL_WIDTH = 256
POOL_WINDOWS = (2, 4, 8, 16)
POOL_GROUP = 64
MLA_HEADS = 8
MLA_NOPE = 64
MLA_ROPE = 32
MLA_V = 64
MLA_Q_RANK = 256
MLA_KV_RANK = 128
MLA_WIDTH = MLA_HEADS * MLA_V
ROPE_BASE = 10000.0
D_MIX = S5_WIDTH + CONV_WIDTH + POOL_WIDTH + MLA_WIDTH
IN_S5 = 0
IN_CONV = IN_S5 + S5_WIDTH
IN_POOL = IN_CONV + 2 * CONV_WIDTH
IN_CQ = IN_POOL + POOL_WIDTH
IN_CKV = IN_CQ + MLA_Q_RANK
IN_KR = IN_CKV + MLA_KV_RANK
D_IN = IN_KR + MLA_ROPE

LANES = 128
HEAD_PAD = 128
S5_CHUNK = 16
S5_FLAT = S5_CHUNK * S5_GROUP
HALO = 16
D_IN_EXT = IN_KR + 2 * HEAD_PAD
FF_CHUNKS = ((0, 1024), (1024, 2048), (2048, 2816))
VMEM_LIMIT = 56 * 1024 * 1024
ATTN_HEADS = 2
Q_SCALE = (MLA_NOPE + MLA_ROPE) ** -0.5 * math.log2(math.e)


def _cparams(*sem, flags=None):
    return pltpu.CompilerParams(dimension_semantics=sem, vmem_limit_bytes=VMEM_LIMIT, flags=flags)


def _const_spec(shape):
    nd = len(shape)
    return pl.BlockSpec(shape, lambda *_: (0,) * nd, pipeline_mode=pl.Buffered(1))


def _rms(x, g):
    return x * lax.rsqrt(jnp.mean(x * x, axis=-1, keepdims=True) + EPS) * g


def _dot(a, b):
    return jnp.dot(a, b, preferred_element_type=F32)


def _mod_kernel(c_ref, w_ref, b_ref, o_ref):
    c = c_ref[...]
    h = (c * jax.nn.sigmoid(c)).astype(BF16)
    o_ref[0] = _dot(h, w_ref[0].astype(BF16)) + b_ref[0]


def _modulation(c_rows, w_ada, b_ada):
    depth = w_ada.shape[0]
    d = D_MODEL
    return pl.pallas_call(
        _mod_kernel,
        grid=(depth, N_MOD),
        in_specs=[
            pl.BlockSpec((8, d), lambda l, j: (0, 0)),
            pl.BlockSpec((1, d, d), lambda l, j: (l, 0, j)),
            pl.BlockSpec((1, 1, d), lambda l, j: (l, 0, j)),
        ],
        out_specs=pl.BlockSpec((1, 8, d), lambda l, j: (l, 0, j)),
        out_shape=jax.ShapeDtypeStruct((depth, 8, N_MOD * d), F32),
        compiler_params=_cparams("arbitrary", "arbitrary"),
        name="adaln_mod",
    )(c_rows, w_ada, b_ada.reshape(depth, 1, N_MOD * d))


def _mod_slices(mod_ref, base):
    d = D_MODEL
    return (mod_ref[0, :, base * d:(base + 1) * d], mod_ref[0, :, (base + 1) * d:(base + 2) * d],
            mod_ref[0, :, (base + 2) * d:(base + 3) * d])


def _ffn_kernel(x_ref, mod_ref, pre_ref, post_ref, wg_ref, wu_ref, wd_ref, o_ref, *, base):
    x = x_ref[0]
    shift, scale, gate = _mod_slices(mod_ref, base)
    h = (_rms(x, pre_ref[...]) * (1.0 + scale) + shift).astype(BF16)
    acc = None
    for lo, hi in FF_CHUNKS:
        g = _dot(h, wg_ref[:, lo:hi])
        u = _dot(h, wu_ref[:, lo:hi])
        a = (g * jax.nn.sigmoid(g) * u).astype(BF16)
        part = _dot(a, wd_ref[lo:hi, :])
        acc = part if acc is None else acc + part
    o_ref[0] = x + MACARON_WEIGHT * gate * _rms(acc, post_ref[...])


def _ffn(x, mod_rows, base, pre_g, post_g, wg, wu, wd, tm):
    b, s, d = x.shape
    per_batch = mod_rows.shape[0] > 1
    return pl.pallas_call(
        functools.partial(_ffn_kernel, base=base),
        grid=(b, s // tm),
        in_specs=[
            pl.BlockSpec((1, tm, d), lambda i, j: (i, j, 0)),
            pl.BlockSpec((1, 1, N_MOD * d), (lambda i, j: (i, 0, 0)) if per_batch else (lambda i, j: (0, 0, 0))),
            _const_spec((1, d)), _const_spec((1, d)),
            _const_spec((d, D_FF)), _const_spec((d, D_FF)), _const_spec((D_FF, d)),
        ],
        out_specs=pl.BlockSpec((1, tm, d), lambda i, j: (i, j, 0)),
        out_shape=jax.ShapeDtypeStruct(x.shape, F32),
        compiler_params=_cparams("arbitrary", "arbitrary"),
        name="ffn_half_step",
    )(x, mod_rows, pre_g.reshape(1, d), post_g.reshape(1, d), wg, wu, wd)


def _inproj_kernel(x_ref, mod_ref, pre_ref, win_ref, qn_ref, kvn_ref, wq_ref, wk_ref, wv_ref, place_ref,
                   cos_ref, sin_ref, u_ref, zc_ref, zp_ref, q_ref, k_ref, v_ref):
    x = x_ref[0]
    shift, scale, _ = _mod_slices(mod_ref, 3)
    h = (_rms(x, pre_ref[...]) * (1.0 + scale) + shift).astype(BF16)
    z = _dot(h, win_ref[...])
    u_ref[0] = z[:, IN_S5:IN_CONV]
    zc_ref[0] = z[:, IN_CONV:IN_POOL]
    zp_ref[0] = z[:, IN_POOL:IN_CQ]
    cos = cos_ref[...]
    sin = sin_ref[...]
    cos_h = jnp.concatenate([cos] * MLA_HEADS, axis=1)
    sin_h = jnp.concatenate([sin] * MLA_HEADS, axis=1)
    cqn = _rms(z[:, IN_CQ:IN_CKV], qn_ref[...]).astype(BF16)
    qq = _dot(cqn, wq_ref[...])
    hw = MLA_HEADS * HEAD_PAD
    q = qq[:, :hw] * cos_h + qq[:, hw:] * sin_h
    q_ref[0] = (q * Q_SCALE).astype(BF16)
    ckvn = _rms(z[:, IN_CKV:IN_KR], kvn_ref[...]).astype(BF16)
    kr = z[:, IN_KR:IN_KR + HEAD_PAD] * cos + z[:, IN_KR + HEAD_PAD:IN_KR + 2 * HEAD_PAD] * sin
    k = _dot(ckvn, wk_ref[...]) + _dot(kr.astype(BF16), place_ref[...])
    k_ref[0] = k.astype(BF16)
    v_ref[0] = _dot(ckvn, wv_ref[...]).astype(BF16)


def _inproj(x, mod_rows, pre_g, lw, cos_t, sin_t, tm):
    b, s, d = x.shape
    per_batch = mod_rows.shape[0] > 1
    hw = MLA_HEADS * HEAD_PAD
    row = lambda w: pl.BlockSpec((1, tm, w), lambda i, j: (i, j, 0))
    return pl.pallas_call(
        _inproj_kernel,
        grid=(b, s // tm),
        in_specs=[
            row(d),
            pl.BlockSpec((1, 1, N_MOD * d), (lambda i, j: (i, 0, 0)) if per_batch else (lambda i, j: (0, 0, 0))),
            _const_spec((1, d)),
            _const_spec((d, D_IN_EXT)),
            _const_spec((1, MLA_Q_RANK)), _const_spec((1, MLA_KV_RANK)),
            _const_spec((MLA_Q_RANK, 2 * hw)), _const_spec((MLA_KV_RANK, hw)),
            _const_spec((MLA_KV_RANK, MLA_WIDTH)), _const_spec((HEAD_PAD, hw)),
            pl.BlockSpec((tm, HEAD_PAD), lambda i, j: (j, 0)),
            pl.BlockSpec((tm, HEAD_PAD), lambda i, j: (j, 0)),
        ],
        out_specs=[row(S5_WIDTH), row(2 * CONV_WIDTH), row(POOL_WIDTH), row(hw), row(hw), row(MLA_WIDTH)],
        out_shape=[
            jax.ShapeDtypeStruct((b, s, S5_WIDTH), F32),
            jax.ShapeDtypeStruct((b, s, 2 * CONV_WIDTH), F32),
            jax.ShapeDtypeStruct((b, s, POOL_WIDTH), F32),
            jax.ShapeDtypeStruct((b, s, hw), BF16),
            jax.ShapeDtypeStruct((b, s, hw), BF16),
            jax.ShapeDtypeStruct((b, s, MLA_WIDTH), BF16),
        ],
        compiler_params=_cparams("arbitrary", "arbitrary"),
        name="mixer_in_proj",
    )(x, mod_rows, pre_g.reshape(1, d), lw["w_in"], lw["q_norm"], lw["kv_norm"], lw["wq"], lw["wk"], lw["wv"],
      lw["place"], cos_t, sin_t)


def _block_transpose(lo, hi):
    n = S5_CHUNK
    half = n // 2
    halves = [[lo[i] for i in range(half)] + [hi[i] for i in range(half)],
              [lo[i + half] for i in range(half)] + [hi[i + half] for i in range(half)]]
    blk = lax.broadcasted_iota(jnp.int32, lo[0].shape, 1) // S5_GROUP
    for d in (4, 2, 1):
        keep = (blk & d) == 0
        for hv in halves:
            for i in range(n):
                if i & d:
                    continue
                x, y = hv[i], hv[i + d]
                hv[i] = jnp.where(keep, x, pltpu.roll(y, d * S5_GROUP, 1))
                hv[i + d] = jnp.where(keep, pltpu.roll(x, LANES - d * S5_GROUP, 1), y)
    return halves


def _strided_rows(ref, phase, period):
    return ref[0, pl.ds(phase, ref.shape[1] // period, stride=period), :]


def _chunk_flat(ua_ref, ub_ref):
    lo, hi = _block_transpose([_strided_rows(ua_ref, t, S5_CHUNK) for t in range(S5_CHUNK)],
                              [_strided_rows(ub_ref, t, S5_CHUNK) for t in range(S5_CHUNK)])
    return [jnp.concatenate([lo[g], hi[g]], axis=1).astype(BF16) for g in range(S5_GROUPS)]


def _s5_state_kernel(ua_ref, ub_ref, m_ref, sf_ref, sb_ref):
    r = ua_ref.shape[1] // S5_CHUNK
    for g, f in enumerate(_chunk_flat(ua_ref, ub_ref)):
        s = _dot(f, m_ref[g])
        sf_ref[0, pl.ds(g, r, stride=S5_GROUPS), :] = s[:, :LANES]
        sb_ref[0, pl.ds(g, r, stride=S5_GROUPS), :] = s[:, LANES:]


def _lane_half(tm, lane_block):
    return pl.BlockSpec((1, tm, LANES), lambda i, j: (i, j, lane_block))


def _s5_chunk_states(u, m_state, tm):
    b, s, _ = u.shape
    out = jax.ShapeDtypeStruct((b, s, LANES), F32)
    return pl.pallas_call(
        _s5_state_kernel,
        grid=(b, s // tm),
        in_specs=[_lane_half(tm, 0), _lane_half(tm, 1), _const_spec(m_state.shape)],
        out_specs=[_lane_half(tm, 0), _lane_half(tm, 0)],
        out_shape=[out, out],
        compiler_params=_cparams("arbitrary", "arbitrary"),
        name="s5_chunk_states",
    )(u, u, m_state)


def _s5_scan_kernel(scf_ref, scb_ref, sf_ref, sb_ref, a_ref, hcf_ref, hcb_ref, hf_ref, hb_ref, st_ref, wf_ref, wb_ref,
                    hrf_ref, hif_ref, hrb_ref, hib_ref):
    half = LANES // 2
    a_re = a_ref[:, :LANES]
    a_im = a_ref[:, LANES:]
    fwd_lane = lax.broadcasted_iota(jnp.int32, a_re.shape, 1) < half

    def swap(v):
        n, g, w = v.shape
        return pltpu.roll(v.reshape(n * g, w), half, 1).reshape(n, g, w)

    def run(in_f, in_b, out_f, out_b, carry):
        n = in_f.shape[1]
        wf_ref[0:n] = swap(in_f[0])
        wb_ref[0:n] = swap(in_b[0])

        def body(i, c):
            h_re, h_im = c
            r = n - 1 - i
            hrf_ref[i] = h_re
            hif_ref[i] = h_im
            hrb_ref[r] = h_re
            hib_ref[r] = h_im
            s_re = jnp.where(fwd_lane, in_f[0, i], wb_ref[r])
            s_im = jnp.where(fwd_lane, wf_ref[i], in_b[0, r])
            return (a_re * h_re - a_im * h_im + s_re, a_re * h_im + a_im * h_re + s_im)

        carry = lax.fori_loop(0, n, body, carry)
        out_f[0] = jnp.where(fwd_lane, hrf_ref[0:n], swap(hif_ref[0:n]))
        out_b[0] = jnp.where(fwd_lane, swap(hrb_ref[0:n]), hib_ref[0:n])
        return carry

    @pl.when(pl.program_id(1) == 0)
    def _():
        zero = jnp.zeros(a_re.shape, F32)
        c = run(scf_ref, scb_ref, hcf_ref, hcb_ref, (zero, zero))
        st_ref[0] = c[0]
        st_ref[1] = c[1]

    c = run(sf_ref, sb_ref, hf_ref, hb_ref, (st_ref[0], st_ref[1]))
    st_ref[0] = c[0]
    st_ref[1] = c[1]


def _s5_scan(sc_f, sc_b, sl_f, sl_b, a16):
    b, nc, g, _ = sc_f.shape
    c = sl_f.shape[1]
    cb = min(128, c)
    nb = c // cb
    ctx = pl.BlockSpec((1, nc, g, LANES), lambda i, j: (i, 0, 0, 0))
    up = pl.BlockSpec((1, cb, g, LANES), lambda i, j: (i, j, 0, 0))
    down = pl.BlockSpec((1, cb, g, LANES), lambda i, j: (i, nb - 1 - j, 0, 0))
    return pl.pallas_call(
        _s5_scan_kernel,
        grid=(b, nb),
        in_specs=[ctx, ctx, up, down, pl.BlockSpec((g, 2 * LANES), lambda i, j: (0, 0))],
        out_specs=[ctx, ctx, up, down],
        out_shape=[jax.ShapeDtypeStruct(sc_f.shape, F32), jax.ShapeDtypeStruct(sc_f.shape, F32),
                   jax.ShapeDtypeStruct(sl_f.shape, F32), jax.ShapeDtypeStruct(sl_f.shape, F32)],
        scratch_shapes=[pltpu.VMEM((2, g, LANES), F32)] + [pltpu.VMEM((max(cb, nc), g, LANES), F32)] * 6,
        compiler_params=_cparams("arbitrary", "arbitrary"),
        name="s5_chunk_scan",
    )(sc_f, sc_b, sl_f, sl_b, a16)


def _s5_out_kernel(ua_ref, ub_ref, hf_ref, hb_ref, mi_ref, mo_ref, ya_ref, yb_ref):
    lo, hi = [], []
    for g, f in enumerate(_chunk_flat(ua_ref, ub_ref)):
        hf = _strided_rows(hf_ref, g, S5_GROUPS).astype(BF16)
        hb = _strided_rows(hb_ref, g, S5_GROUPS).astype(BF16)
        y = _dot(f, mi_ref[g]) + _dot(hf, mo_ref[g, :LANES, :]) + _dot(hb, mo_ref[g, LANES:, :])
        lo.append(y[:, :LANES])
        hi.append(y[:, LANES:])
    lo, hi = _block_transpose(lo, hi)
    r = ua_ref.shape[1] // S5_CHUNK
    for t in range(S5_CHUNK):
        ya_ref[0, pl.ds(t, r, stride=S5_CHUNK), :] = lo[t]
        yb_ref[0, pl.ds(t, r, stride=S5_CHUNK), :] = hi[t]


def _s5_outputs(u, h_f, h_b, m_intra, m_out, tm):
    b, s, _ = u.shape
    out = jax.ShapeDtypeStruct((b, s, LANES), F32)
    return pl.pallas_call(
        _s5_out_kernel,
        grid=(b, s // tm),
        in_specs=[_lane_half(tm, 0), _lane_half(tm, 1), _lane_half(tm, 0), _lane_half(tm, 0),
                  _const_spec(m_intra.shape), _const_spec(m_out.shape)],
        out_specs=[_lane_half(tm, 0), _lane_half(tm, 0)],
        out_shape=[out, out],
        compiler_params=_cparams("arbitrary", "arbitrary"),
        name="s5_chunk_outputs",
    )(u, u, h_f, h_b, m_intra, m_out)


def _s5_matrices(lam_re, lam_im, log_dt, b_re, b_im, c_re, c_im):
    hp = lax.Precision.HIGHEST
    t = S5_CHUNK
    g, p, h = S5_GROUPS, S5_STATE, S5_GROUP
    lam = lax.complex(jnp.minimum(lam_re, -1e-4), lam_im)
    dt = jnp.exp(log_dt)[..., None]
    steps = jnp.arange(t + 1, dtype=F32)[:, None, None, None]
    apow = jnp.exp(lam[None] * dt[None] * steps)
    bb = ((apow[1] - 1.0) / lam)[..., None] * lax.complex(b_re, b_im)
    cc = lax.complex(c_re, c_im)
    taps = jnp.einsum('dghp,jdgp,dgpk->djghk', cc, apow[:t], bb, precision=hp).real
    s_idx = jnp.arange(t)[:, None]
    t_idx = jnp.arange(t)[None, :]
    kf = taps[0][jnp.clip(t_idx - s_idx, 0, t - 1)] * (t_idx >= s_idx)[:, :, None, None, None].astype(F32)
    kb = taps[1][jnp.clip(s_idx - t_idx, 0, t - 1)] * (s_idx >= t_idx)[:, :, None, None, None].astype(F32)
    m_intra = (kf + kb).transpose(2, 0, 4, 1, 3).reshape(g, t * h, t * h)
    wf = apow[:t][::-1, 0][:, :, :, None] * bb[0][None]
    wb = apow[:t, 1][:, :, :, None] * bb[1][None]
    to_rows = lambda w: w.transpose(1, 0, 3, 2).reshape(g, t * h, p)
    m_state = jnp.concatenate([to_rows(wf.real), to_rows(wf.imag), to_rows(wb.real), to_rows(wb.imag)], axis=-1)
    of = cc[0][None] * apow[1:, 0][:, :, None, :]
    ob = cc[1][None] * apow[1:, 1][::-1][:, :, None, :]
    to_cols = lambda w: w.transpose(1, 3, 0, 2).reshape(g, p, t * h)
    m_out = jnp.concatenate([to_cols(of.real), -to_cols(of.imag), to_cols(ob.real), -to_cols(ob.imag)], axis=1)
    a_t = apow[t]
    a16 = jnp.concatenate([a_t[0].real, a_t[1].real, a_t[0].imag, a_t[1].imag], axis=-1)
    return m_intra.astype(BF16), m_state.astype(BF16), m_out.astype(BF16), a16


def _s5_mixer(u_ctx, u_lat, mats):
    m_intra, m_state, m_out, a16 = mats
    b, lc, w = u_ctx.shape
    ll = u_lat.shape[1]
    g, t = S5_GROUPS, S5_CHUNK
    tm_c, tm_l = min(1024, lc), min(1024, ll)
    chunks = lambda x: x.reshape(b, x.shape[1] // g, g, LANES)
    rows = lambda x: x.reshape(b, x.shape[1] * g, LANES)
    sc_f, sc_b = _s5_chunk_states(u_ctx, m_state, tm_c)
    sl_f, sl_b = _s5_chunk_states(u_lat, m_state, tm_l)
    hc_f, hc_b, hl_f, hl_b = _s5_scan(chunks(sc_f), chunks(sc_b), chunks(sl_f), chunks(sl_b), a16)
    y_ctx = _s5_outputs(u_ctx, rows(hc_f), rows(hc_b), m_intra, m_out, tm_c)
    y_lat = _s5_outputs(u_lat, rows(hl_f), rows(hl_b), m_intra, m_out, tm_l)
    return y_ctx, y_lat


def _local_kernel(zc_ref, zcp_ref, zcn_ref, zp_ref, zpp_ref, zpn_ref, cw_ref, cb_ref, lg_ref, lb_ref, pw_ref, ps_ref,
                  conv_ref, pool_ref, ce_ref, pe_ref, *, seq_len):
    tm = zc_ref.shape[1]
    j = pl.program_id(1)
    first = j == 0
    last = j == pl.num_programs(1) - 1

    def glu(z):
        return z[:, :CONV_WIDTH] * jax.nn.sigmoid(z[:, CONV_WIDTH:])

    ce_ref[0:HALO] = jnp.where(first, 0.0, glu(zcp_ref[0]))
    ce_ref[HALO:HALO + tm] = glu(zc_ref[0])
    ce_ref[HALO + tm:] = jnp.where(last, 0.0, glu(zcn_ref[0]))
    pe_ref[0:HALO] = jnp.where(first, 0.0, zpp_ref[0])
    pe_ref[HALO:HALO + tm] = zp_ref[0]
    pe_ref[HALO + tm:] = jnp.where(last, 0.0, zpn_ref[0])

    acc = None
    for k in range(CONV_K):
        term = ce_ref[pl.ds(HALO + k - CONV_K // 2, tm), :] * cw_ref[k:k + 1, :]
        acc = term if acc is None else acc + term
    acc = acc + cb_ref[...]
    mu = jnp.mean(acc, axis=-1, keepdims=True)
    xc = acc - mu
    var = jnp.mean(xc * xc, axis=-1, keepdims=True)
    y = xc * lax.rsqrt(var + EPS) * lg_ref[...] + lb_ref[...]
    conv_ref[0] = (y * jax.nn.sigmoid(y)).astype(BF16)

    def tap(dlt):
        return pe_ref[pl.ds(HALO + dlt, tm), :]

    z = tap(0)
    sums = {}
    run = z + tap(1)
    sums[2] = run
    run = run + tap(-1) + tap(2)
    sums[4] = run
    run = run + tap(-3) + tap(-2) + tap(3) + tap(4)
    sums[8] = run
    for dlt in (-7, -6, -5, -4, 5, 6, 7, 8):
        run = run + tap(dlt)
    sums[16] = run
    pos = j * tm + lax.broadcasted_iota(jnp.int32, (tm, 1), 0)
    lane = lax.broadcasted_iota(jnp.int32, (tm, POOL_WIDTH), 1)
    mean = None
    for gi, w in enumerate(POOL_WINDOWS):
        lo = jnp.maximum(pos - (w - 1) // 2, 0)
        hi = jnp.minimum(pos + w // 2, seq_len - 1)
        m = sums[w] / (hi - lo + 1).astype(F32)
        mean = m if mean is None else jnp.where(lane >= gi * POOL_GROUP, m, mean)
    d = (mean - z).astype(BF16)
    pool_ref[0] = (_dot(d, pw_ref[...]) * ps_ref[...]).astype(BF16)


def _local_mixers(zc, zp, lw, tm):
    b, s, _ = zc.shape
    nh = tm // HALO
    nblk = s // HALO
    main = lambda w: pl.BlockSpec((1, tm, w), lambda i, j: (i, j, 0))
    prev = lambda w: pl.BlockSpec((1, HALO, w), lambda i, j: (i, jnp.maximum(j * nh - 1, 0), 0))
    nxt = lambda w: pl.BlockSpec((1, HALO, w), lambda i, j: (i, jnp.minimum((j + 1) * nh, nblk - 1), 0))
    cw2, pw = 2 * CONV_WIDTH, POOL_WIDTH
    return pl.pallas_call(
        functools.partial(_local_kernel, seq_len=s),
        grid=(b, s // tm),
        in_specs=[main(cw2), prev(cw2), nxt(cw2), main(pw), prev(pw), nxt(pw),
                  _const_spec((32, CONV_WIDTH)), _const_spec((1, CONV_WIDTH)), _const_spec((1, CONV_WIDTH)),
                  _const_spec((1, CONV_WIDTH)), _const_spec((pw, pw)), _const_spec((1, pw))],
        out_specs=[main(CONV_WIDTH), main(pw)],
        out_shape=[jax.ShapeDtypeStruct((b, s, CONV_WIDTH), BF16), jax.ShapeDtypeStruct((b, s, pw), BF16)],
        scratch_shapes=[pltpu.VMEM((tm + 2 * HALO, CONV_WIDTH), F32), pltpu.VMEM((tm + 2 * HALO, pw), F32)],
        compiler_params=_cparams("arbitrary", "arbitrary"),
        name="conv_pool",
    )(zc, zc, zc, zp, zp, zp, lw["conv_w"], lw["conv_b"], lw["conv_ln_g"], lw["conv_ln_b"], lw["pool_w"],
      lw["pool_scale"])


def _attn_kernel(*refs, tk, has_lat):
    if has_lat:
        q_ref, kc_ref, vtc_ref, kl_ref, vtl_ref, o_ref, s_ref = refs
    else:
        q_ref, kc_ref, vtc_ref, o_ref = refs
    tq = q_ref.shape[1]
    heads = range(ATTN_HEADS)
    qs = [q_ref[0, :, h * HEAD_PAD:(h + 1) * HEAD_PAD] for h in heads]

    def scores(h, k):
        return lax.dot_general(k, qs[h], (((1,), (1,)), ((), ())), preferred_element_type=F32)

    def consume(s, mt, vt, state):
        m, l, acc = state
        m_new = jnp.maximum(m, mt)
        p = jnp.exp2(s - m_new)
        alpha = jnp.exp2(m - m_new)
        l = alpha * l + jnp.sum(p, axis=0, keepdims=True)
        acc = alpha * acc + _dot(vt, p.astype(BF16))
        return m_new, l, acc

    init = (jnp.full((1, tq), -1e30, F32), jnp.zeros((1, tq), F32), jnp.zeros((MLA_V, tq), F32))
    state = []
    for h in heads:
        s = scores(h, kc_ref[0, :, h * HEAD_PAD:(h + 1) * HEAD_PAD])
        state.append(consume(s, jnp.max(s, axis=0, keepdims=True), vtc_ref[0, h * MLA_V:(h + 1) * MLA_V, :], init))

    if has_lat:
        n_tiles = kl_ref.shape[1] // tk

        def produce(h, slot, i):
            off = pl.multiple_of(i * tk, tk)
            s = scores(h, kl_ref[0, pl.ds(off, tk), h * HEAD_PAD:(h + 1) * HEAD_PAD])
            s_ref[h, slot] = s
            return jnp.max(s, axis=0, keepdims=True)

        def use(h, slot, i, mt, st):
            off = pl.multiple_of(i * tk, tk)
            return consume(s_ref[h, slot], mt, vtl_ref[0, h * MLA_V:(h + 1) * MLA_V, pl.ds(off, tk)], st)

        def body(j, carry):
            st, mt0 = carry
            i0 = 2 * j
            mt1 = [produce(h, 1, i0 + 1) for h in heads]
            st = [use(h, 0, i0, mt0[h], st[h]) for h in heads]
            nxt = jnp.minimum(i0 + 2, n_tiles - 1)
            mt0 = [produce(h, 0, nxt) for h in heads]
            st = [use(h, 1, i0 + 1, mt1[h], st[h]) for h in heads]
            return st, mt0

        mt0 = [produce(h, 0, 0) for h in heads]
        state, _ = lax.fori_loop(0, n_tiles // 2, body, (state, mt0))
    o_ref[0] = jnp.concatenate([acc / l for _, l, acc in state], axis=0).astype(BF16)


def _attention(q, k_ctx, vt_ctx, k_lat, vt_lat, tq, tk):
    b, lq, _ = q.shape
    lc = k_ctx.shape[1]
    has_lat = k_lat is not None
    nh = ATTN_HEADS
    in_specs = [pl.BlockSpec((1, tq, nh * HEAD_PAD), lambda i, p, j: (i, j, p)),
                pl.BlockSpec((1, lc, nh * HEAD_PAD), lambda i, p, j: (i, 0, p)),
                pl.BlockSpec((1, nh * MLA_V, lc), lambda i, p, j: (i, p, 0))]
    args = [q, k_ctx, vt_ctx]
    scratch = []
    if has_lat:
        ll = k_lat.shape[1]
        assert ll % (2 * tk) == 0
        in_specs += [pl.BlockSpec((1, ll, nh * HEAD_PAD), lambda i, p, j: (i, 0, p)),
                     pl.BlockSpec((1, nh * MLA_V, ll), lambda i, p, j: (i, p, 0))]
        args += [k_lat, vt_lat]
        scratch = [pltpu.VMEM((nh, 2, tk, tq), F32)]
    return pl.pallas_call(
        functools.partial(_attn_kernel, tk=tk, has_lat=has_lat),
        grid=(b, MLA_HEADS // nh, lq // tq),
        in_specs=in_specs,
        out_specs=pl.BlockSpec((1, nh * MLA_V, tq), lambda i, p, j: (i, p, j)),
        out_shape=jax.ShapeDtypeStruct((b, MLA_WIDTH, lq), BF16),
        scratch_shapes=scratch,
        compiler_params=_cparams("arbitrary", "arbitrary", "arbitrary"),
        name="mla_attention",
    )(*args)


def _mixout_kernel(x_ref, mod_ref, post_ref, y5a_ref, y5b_ref, u_ref, d_ref, wglu_ref, bglu_ref, conv_ref, pool_ref,
                   att_ref, wout_ref, o_ref):
    x = x_ref[0]
    _, _, gate = _mod_slices(mod_ref, 3)
    y5 = jax.nn.gelu(jnp.concatenate([y5a_ref[0], y5b_ref[0]], axis=1) + d_ref[...] * u_ref[0])
    s5 = y5 * jax.nn.sigmoid(_dot(y5.astype(BF16), wglu_ref[...]) + bglu_ref[...])
    o1, o2, o3 = S5_WIDTH, S5_WIDTH + CONV_WIDTH, S5_WIDTH + CONV_WIDTH + POOL_WIDTH
    y = (_dot(s5.astype(BF16), wout_ref[0:o1, :]) + _dot(conv_ref[0], wout_ref[o1:o2, :])
         + _dot(pool_ref[0], wout_ref[o2:o3, :]) + _dot(att_ref[0], wout_ref[o3:, :]))
    o_ref[0] = x + gate * _rms(y, post_ref[...])


def _mix_out(x, mod_rows, post_g, y5, u, conv, pool, att, lw, tm):
    b, s, d = x.shape
    per_batch = mod_rows.shape[0] > 1
    row = lambda w: pl.BlockSpec((1, tm, w), lambda i, j: (i, j, 0))
    return pl.pallas_call(
        _mixout_kernel,
        grid=(b, s // tm),
        in_specs=[
            row(d),
            pl.BlockSpec((1, 1, N_MOD * d), (lambda i, j: (i, 0, 0)) if per_batch else (lambda i, j: (0, 0, 0))),
            _const_spec((1, d)),
            row(LANES), row(LANES), row(S5_WIDTH), _const_spec((1, S5_WIDTH)), _const_spec((S5_WIDTH, S5_WIDTH)),
            _const_spec((1, S5_WIDTH)),
            row(CONV_WIDTH), row(POOL_WIDTH), row(MLA_WIDTH), _const_spec((D_MIX, d)),
        ],
        out_specs=row(d),
        out_shape=jax.ShapeDtypeStruct(x.shape, F32),
        compiler_params=_cparams("arbitrary", "arbitrary"),
        name="mixer_out_proj",
    )(x, mod_rows, post_g.reshape(1, d), y5[0], y5[1], u, lw["s5_d"], lw["s5_w_glu"], lw["s5_b_glu"], conv, pool,
      att, lw["w_out"])


def _rope_partner(w):
    q = MLA_ROPE // 4
    return jnp.concatenate([-w[..., q:2 * q], w[..., 0:q], -w[..., 3 * q:4 * q], w[..., 2 * q:3 * q]], axis=-1)


def _rope_tables(n_lat):
    rows = n_lat // GRID_W
    axis_dim = MLA_ROPE // 2
    inv = ROPE_BASE ** (-jnp.arange(0, axis_dim, 2, dtype=F32) / axis_dim)
    ang_r = jnp.arange(rows, dtype=F32)[:, None] * inv
    ang_c = jnp.arange(GRID_W, dtype=F32)[:, None] * inv
    nq = MLA_ROPE // 4

    def table(fn, fill):
        by_row = jnp.broadcast_to(fn(ang_r)[:, None, :], (rows, GRID_W, nq))
        by_col = jnp.broadcast_to(fn(ang_c)[None, :, :], (rows, GRID_W, nq))
        head = jnp.full((rows, GRID_W, MLA_NOPE), fill, F32)
        tail = jnp.full((rows, GRID_W, HEAD_PAD - MLA_NOPE - MLA_ROPE), fill, F32)
        return jnp.concatenate([head, by_row, by_row, by_col, by_col, tail], axis=-1).reshape(n_lat, HEAD_PAD)

    return table(jnp.cos, 1.0), table(jnp.sin, 0.0)


def _pad_head(nope, rope):
    r = nope.shape[0]
    z = jnp.zeros((r, MLA_HEADS, HEAD_PAD - MLA_NOPE - MLA_ROPE), nope.dtype)
    return jnp.concatenate([nope, rope, z], axis=-1).reshape(r, MLA_HEADS * HEAD_PAD)


def _prep_layer(l, w_in, w_out, s5_d, s5_w_glu, s5_b_glu, conv_w, conv_b, conv_ln_g, conv_ln_b, pool_w, pool_scale,
                mla_q_norm, mla_w_uq, mla_kv_norm, mla_w_ukv):
    d = D_MODEL
    wi = w_in[l]
    kr = wi[:, IN_KR:D_IN]
    zpad_l = jnp.zeros((d, MLA_NOPE), F32)
    zpad_r = jnp.zeros((d, HEAD_PAD - MLA_NOPE - MLA_ROPE), F32)
    w_in_ext = jnp.concatenate([wi[:, :IN_KR], zpad_l, kr, zpad_r, zpad_l, _rope_partner(kr), zpad_r], axis=1)
    uq = mla_w_uq[l].reshape(MLA_Q_RANK, MLA_HEADS, MLA_NOPE + MLA_ROPE)
    q_nope, q_rope = uq[..., :MLA_NOPE], uq[..., MLA_NOPE:]
    wq = jnp.concatenate([_pad_head(q_nope, q_rope), _pad_head(0 * q_nope, _rope_partner(q_rope))], axis=1)
    ukv = mla_w_ukv[l].reshape(MLA_KV_RANK, MLA_HEADS, MLA_NOPE + MLA_V)
    wk = _pad_head(ukv[..., :MLA_NOPE], jnp.zeros((MLA_KV_RANK, MLA_HEADS, MLA_ROPE), F32))
    wv = ukv[..., MLA_NOPE:].reshape(MLA_KV_RANK, MLA_WIDTH)
    eye = jnp.eye(HEAD_PAD, dtype=F32) * ((jnp.arange(HEAD_PAD) >= MLA_NOPE)
                                          & (jnp.arange(HEAD_PAD) < MLA_NOPE + MLA_ROPE))[:, None]
    place = jnp.tile(eye, (1, MLA_HEADS))
    pool_bd = jnp.zeros((POOL_WIDTH, POOL_WIDTH), F32)
    for gi in range(len(POOL_WINDOWS)):
        sl = slice(gi * POOL_GROUP, (gi + 1) * POOL_GROUP)
        pool_bd = pool_bd.at[sl, sl].set(pool_w[l, gi])
    return {
        "w_in": w_in_ext.astype(BF16), "w_out": w_out[l].astype(BF16),
        "q_norm": mla_q_norm[l].reshape(1, -1), "kv_norm": mla_kv_norm[l].reshape(1, -1),
        "wq": wq.astype(BF16), "wk": wk.astype(BF16), "wv": wv.astype(BF16), "place": place.astype(BF16),
        "s5_d": s5_d[l].reshape(1, -1), "s5_w_glu": s5_w_glu[l].astype(BF16), "s5_b_glu": s5_b_glu[l].reshape(1, -1),
        "conv_w": jnp.concatenate([conv_w[l], jnp.zeros((1, CONV_WIDTH), F32)], axis=0),
        "conv_b": conv_b[l].reshape(1, -1), "conv_ln_g": conv_ln_g[l].reshape(1, -1),
        "conv_ln_b": conv_ln_b[l].reshape(1, -1),
        "pool_w": pool_bd.astype(BF16), "pool_scale": pool_scale[l].reshape(1, -1),
    }


def kernel(x, c, ctx, c_ctx, w_ada, b_ada, norm_pre, norm_post, ffn_w_gate, ffn_w_up, ffn_w_down, w_in, w_out,
           s5_lam_re, s5_lam_im, s5_log_dt, s5_b_re, s5_b_im, s5_c_re, s5_c_im, s5_d, s5_w_glu, s5_b_glu, conv_w,
           conv_b, conv_ln_g, conv_ln_b, pool_w, pool_scale, mla_q_norm, mla_w_uq, mla_kv_norm, mla_w_ukv):
    batch, n_lat, d = x.shape
    n_ctx = ctx.shape[1]
    depth = w_ada.shape[0]
    assert d == D_MODEL and batch + 1 <= 8
    tm_lat = min(512, n_lat)
    tm_ctx = min(512, n_ctx)
    tq_lat = min(1024, n_lat)
    tk = min(1024, n_lat // 2)
    assert n_lat % tm_lat == 0 and n_ctx % S5_CHUNK == 0 and n_lat % GRID_W == 0

    c_rows = jnp.zeros((8, d), F32).at[:batch].set(c).at[batch].set(c_ctx)
    mod = _modulation(c_rows, w_ada, b_ada)
    cos_lat, sin_lat = _rope_tables(n_lat)
    cos_ctx = jnp.ones((n_ctx, HEAD_PAD), F32)
    sin_ctx = jnp.zeros((n_ctx, HEAD_PAD), F32)
    wg = ffn_w_gate.astype(BF16)
    wu = ffn_w_up.astype(BF16)
    wd = ffn_w_down.astype(BF16)

    x_lat, x_ctx = x, ctx
    for l in range(depth):
        last = l == depth - 1
        mod_lat = mod[l, :batch].reshape(batch, 1, N_MOD * d)
        mod_ctx = mod[l, batch].reshape(1, 1, N_MOD * d)
        lw = _prep_layer(l, w_in, w_out, s5_d, s5_w_glu, s5_b_glu, conv_w, conv_b, conv_ln_g, conv_ln_b, pool_w,
                         pool_scale, mla_q_norm, mla_w_uq, mla_kv_norm, mla_w_ukv)
        s5_mats = _s5_matrices(s5_lam_re[l], s5_lam_im[l], s5_log_dt[l], s5_b_re[l], s5_b_im[l], s5_c_re[l],
                               s5_c_im[l])
        x_lat = _ffn(x_lat, mod_lat, 0, norm_pre[l, 0], norm_post[l, 0], wg[l, 0], wu[l, 0], wd[l, 0], tm_lat)
        x_ctx = _ffn(x_ctx, mod_ctx, 0, norm_pre[l, 0], norm_post[l, 0], wg[l, 0], wu[l, 0], wd[l, 0], tm_ctx)
        u_l, zc_l, zp_l, q_l, k_l, v_l = _inproj(x_lat, mod_lat, norm_pre[l, 1], lw, cos_lat, sin_lat, tm_lat)
        u_c, zc_c, zp_c, q_c, k_c, v_c = _inproj(x_ctx, mod_ctx, norm_pre[l, 1], lw, cos_ctx, sin_ctx, tm_ctx)
        y5_c, y5_l = _s5_mixer(u_c, u_l, s5_mats)
        conv_l, pool_l = _local_mixers(zc_l, zp_l, lw, tm_lat)
        vt_l, vt_c = v_l.transpose(0, 2, 1), v_c.transpose(0, 2, 1)
        att_l = _attention(q_l, k_c, vt_c, k_l, vt_l, tq_lat, tk).transpose(0, 2, 1)
        x_lat = _mix_out(x_lat, mod_lat, norm_post[l, 1], y5_l, u_l, conv_l, pool_l, att_l, lw, tm_lat)
        if not last:
            conv_c, pool_c = _local_mixers(zc_c, zp_c, lw, tm_ctx)
            att_c = _attention(q_c, k_c, vt_c, None, None, tm_ctx, tk).transpose(0, 2, 1)
            x_ctx = _mix_out(x_ctx, mod_ctx, norm_post[l, 1], y5_c, u_c, conv_c, pool_c, att_c, lw, tm_ctx)
        x_lat = _ffn(x_lat, mod_lat, 6, norm_pre[l, 2], norm_post[l, 2], wg[l, 1], wu[l, 1], wd[l, 1], tm_lat)
        if not last:
            x_ctx = _ffn(x_ctx, mod_ctx, 6, norm_pre[l, 2], norm_post[l, 2], wg[l, 1], wu[l, 1], wd[l, 1], tm_ctx)
    return x_lat
```

```python
import functools
import math

import jax
import jax.numpy as jnp
from jax import lax
from jax.experimental import pallas as pl
from jax.experimental.pallas import tpu as pltpu

F32 = jnp.float32
BF16 = jnp.bfloat16

D_MODEL = 1024
GRID_W = 64
EPS = 1e-6
N_MOD = 9
MACARON_WEIGHT = 0.5
D_FF = 2816
S5_WIDTH = 256
S5_GROUP = 16
S5_GROUPS = 16
S5_STATE = 64
CONV_WIDTH = 256
CONV_K = 31
POOL_WIDTH = 256
POOL_WINDOWS = (2, 4, 8, 16)
POOL_GROUP = 64
MLA_HEADS = 8
MLA_NOPE = 64
MLA_ROPE = 32
MLA_V = 64
MLA_Q_RANK = 256
MLA_KV_RANK = 128
MLA_WIDTH = MLA_HEADS * MLA_V
ROPE_BASE = 10000.0
D_MIX = S5_WIDTH + CONV_WIDTH + POOL_WIDTH + MLA_WIDTH
IN_S5 = 0
IN_CONV = IN_S5 + S5_WIDTH
IN_POOL = IN_CONV + 2 * CONV_WIDTH
IN_CQ = IN_POOL + POOL_WIDTH
IN_CKV = IN_CQ + MLA_Q_RANK
IN_KR = IN_CKV + MLA_KV_RANK
D_IN = IN_KR + MLA_ROPE

LANES = 128
HEAD_PAD = 128
S5_CHUNK = 16
S5_FLAT = S5_CHUNK * S5_GROUP
HALO = 16
D_IN_EXT = IN_KR + 2 * HEAD_PAD
FF_CHUNKS = ((0, 1024), (1024, 2048), (2048, 2816))
VMEM_LIMIT = 56 * 1024 * 1024
ATTN_HEADS = 2
Q_SCALE = (MLA_NOPE + MLA_ROPE) ** -0.5 * math.log2(math.e)


def _cparams(*sem, flags=None):
    return pltpu.CompilerParams(dimension_semantics=sem, vmem_limit_bytes=VMEM_LIMIT, flags=flags)


def _const_spec(shape):
    nd = len(shape)
    return pl.BlockSpec(shape, lambda *_: (0,) * nd, pipeline_mode=pl.Buffered(1))


def _rms(x, g):
    return x * lax.rsqrt(jnp.mean(x * x, axis=-1, keepdims=True) + EPS) * g


def _dot(a, b):
    return jnp.dot(a, b, preferred_element_type=F32)


def _mod_kernel(c_ref, w_ref, b_ref, o_ref):
    c = c_ref[...]
    h = (c * jax.nn.sigmoid(c)).astype(BF16)
    o_ref[0] = _dot(h, w_ref[0].astype(BF16)) + b_ref[0]


def _modulation(c_rows, w_ada, b_ada):
    depth = w_ada.shape[0]
    d = D_MODEL
    return pl.pallas_call(
        _mod_kernel,
        grid=(depth, N_MOD),
        in_specs=[
            pl.BlockSpec((8, d), lambda l, j: (0, 0)),
            pl.BlockSpec((1, d, d), lambda l, j: (l, 0, j)),
            pl.BlockSpec((1, 1, d), lambda l, j: (l, 0, j)),
        ],
        out_specs=pl.BlockSpec((1, 8, d), lambda l, j: (l, 0, j)),
        out_shape=jax.ShapeDtypeStruct((depth, 8, N_MOD * d), F32),
        compiler_params=_cparams("arbitrary", "arbitrary"),
        name="adaln_mod",
    )(c_rows, w_ada, b_ada.reshape(depth, 1, N_MOD * d))


def _mod_slices(mod_ref, base):
    d = D_MODEL
    return (mod_ref[0, :, base * d:(base + 1) * d], mod_ref[0, :, (base + 1) * d:(base + 2) * d],
            mod_ref[0, :, (base + 2) * d:(base + 3) * d])


def _ffn_kernel(x_ref, mod_ref, pre_ref, post_ref, wg_ref, wu_ref, wd_ref, o_ref, *, base):
    x = x_ref[0]
    shift, scale, gate = _mod_slices(mod_ref, base)
    h = (_rms(x, pre_ref[...]) * (1.0 + scale) + shift).astype(BF16)
    acc = None
    for lo, hi in FF_CHUNKS:
        g = _dot(h, wg_ref[:, lo:hi])
        u = _dot(h, wu_ref[:, lo:hi])
        a = (g * jax.nn.sigmoid(g) * u).astype(BF16)
        part = _dot(a, wd_ref[lo:hi, :])
        acc = part if acc is None else acc + part
    o_ref[0] = x + MACARON_WEIGHT * gate * _rms(acc, post_ref[...])


def _ffn(x, mod_rows, base, pre_g, post_g, wg, wu, wd, tm):
    b, s, d = x.shape
    per_batch = mod_rows.shape[0] > 1
    return pl.pallas_call(
        functools.partial(_ffn_kernel, base=base),
        grid=(b, s // tm),
        in_specs=[
            pl.BlockSpec((1, tm, d), lambda i, j: (i, j, 0)),
            pl.BlockSpec((1, 1, N_MOD * d), (lambda i, j: (i, 0, 0)) if per_batch else (lambda i, j: (0, 0, 0))),
            _const_spec((1, d)), _const_spec((1, d)),
            _const_spec((d, D_FF)), _const_spec((d, D_FF)), _const_spec((D_FF, d)),
        ],
        out_specs=pl.BlockSpec((1, tm, d), lambda i, j: (i, j, 0)),
        out_shape=jax.ShapeDtypeStruct(x.shape, F32),
        compiler_params=_cparams("arbitrary", "arbitrary"),
        name="ffn_half_step",
    )(x, mod_rows, pre_g.reshape(1, d), post_g.reshape(1, d), wg, wu, wd)


def _inproj_kernel(x_ref, mod_ref, pre_ref, win_ref, qn_ref, kvn_ref, wq_ref, wk_ref, wv_ref, place_ref,
                   cos_ref, sin_ref, u_ref, zc_ref, zp_ref, q_ref, k_ref, v_ref):
    x = x_ref[0]
    shift, scale, _ = _mod_slices(mod_ref, 3)
    h = (_rms(x, pre_ref[...]) * (1.0 + scale) + shift).astype(BF16)
    z = _dot(h, win_ref[...])
    u_ref[0] = z[:, IN_S5:IN_CONV]
    zc_ref[0] = z[:, IN_CONV:IN_POOL]
    zp_ref[0] = z[:, IN_POOL:IN_CQ]
    cos = cos_ref[...]
    sin = sin_ref[...]
    cos_h = jnp.concatenate([cos] * MLA_HEADS, axis=1)
    sin_h = jnp.concatenate([sin] * MLA_HEADS, axis=1)
    cqn = _rms(z[:, IN_CQ:IN_CKV], qn_ref[...]).astype(BF16)
    qq = _dot(cqn, wq_ref[...])
    hw = MLA_HEADS * HEAD_PAD
    q = qq[:, :hw] * cos_h + qq[:, hw:] * sin_h
    q_ref[0] = (q * Q_SCALE).astype(BF16)
    ckvn = _rms(z[:, IN_CKV:IN_KR], kvn_ref[...]).astype(BF16)
    kr = z[:, IN_KR:IN_KR + HEAD_PAD] * cos + z[:, IN_KR + HEAD_PAD:IN_KR + 2 * HEAD_PAD] * sin
    k = _dot(ckvn, wk_ref[...]) + _dot(kr.astype(BF16), place_ref[...])
    k_ref[0] = k.astype(BF16)
    v_ref[0] = _dot(ckvn, wv_ref[...]).astype(BF16)


def _inproj(x, mod_rows, pre_g, lw, cos_t, sin_t, tm):
    b, s, d = x.shape
    per_batch = mod_rows.shape[0] > 1
    hw = MLA_HEADS * HEAD_PAD
    row = lambda w: pl.BlockSpec((1, tm, w), lambda i, j: (i, j, 0))
    return pl.pallas_call(
        _inproj_kernel,
        grid=(b, s // tm),
        in_specs=[
            row(d),
            pl.BlockSpec((1, 1, N_MOD * d), (lambda i, j: (i, 0, 0)) if per_batch else (lambda i, j: (0, 0, 0))),
            _const_spec((1, d)),
            _const_spec((d, D_IN_EXT)),
            _const_spec((1, MLA_Q_RANK)), _const_spec((1, MLA_KV_RANK)),
            _const_spec((MLA_Q_RANK, 2 * hw)), _const_spec((MLA_KV_RANK, hw)),
            _const_spec((MLA_KV_RANK, MLA_WIDTH)), _const_spec((HEAD_PAD, hw)),
            pl.BlockSpec((tm, HEAD_PAD), lambda i, j: (j, 0)),
            pl.BlockSpec((tm, HEAD_PAD), lambda i, j: (j, 0)),
        ],
        out_specs=[row(S5_WIDTH), row(2 * CONV_WIDTH), row(POOL_WIDTH), row(hw), row(hw), row(MLA_WIDTH)],
        out_shape=[
            jax.ShapeDtypeStruct((b, s, S5_WIDTH), F32),
            jax.ShapeDtypeStruct((b, s, 2 * CONV_WIDTH), F32),
            jax.ShapeDtypeStruct((b, s, POOL_WIDTH), F32),
            jax.ShapeDtypeStruct((b, s, hw), BF16),
            jax.ShapeDtypeStruct((b, s, hw), BF16),
            jax.ShapeDtypeStruct((b, s, MLA_WIDTH), BF16),
        ],
        compiler_params=_cparams("arbitrary", "arbitrary"),
        name="mixer_in_proj",
    )(x, mod_rows, pre_g.reshape(1, d), lw["w_in"], lw["q_norm"], lw["kv_norm"], lw["wq"], lw["wk"], lw["wv"],
      lw["place"], cos_t, sin_t)


def _block_transpose(lo, hi):
    n = S5_CHUNK
    half = n // 2
    halves = [[lo[i] for i in range(half)] + [hi[i] for i in range(half)],
              [lo[i + half] for i in range(half)] + [hi[i + half] for i in range(half)]]
    blk = lax.broadcasted_iota(jnp.int32, lo[0].shape, 1) // S5_GROUP
    for d in (4, 2, 1):
        keep = (blk & d) == 0
        for hv in halves:
            for i in range(n):
                if i & d:
                    continue
                x, y = hv[i], hv[i + d]
                hv[i] = jnp.where(keep, x, pltpu.roll(y, d * S5_GROUP, 1))
                hv[i + d] = jnp.where(keep, pltpu.roll(x, LANES - d * S5_GROUP, 1), y)
    return halves


def _strided_rows(ref, phase, period):
    return ref[0, pl.ds(phase, ref.shape[1] // period, stride=period), :]


def _chunk_flat(ua_ref, ub_ref):
    lo, hi = _block_transpose([_strided_rows(ua_ref, t, S5_CHUNK) for t in range(S5_CHUNK)],
                              [_strided_rows(ub_ref, t, S5_CHUNK) for t in range(S5_CHUNK)])
    return [jnp.concatenate([lo[g], hi[g]], axis=1).astype(BF16) for g in range(S5_GROUPS)]


def _s5_state_kernel(ua_ref, ub_ref, m_ref, sf_ref, sb_ref):
    r = ua_ref.shape[1] // S5_CHUNK
    for g, f in enumerate(_chunk_flat(ua_ref, ub_ref)):
        s = _dot(f, m_ref[g])
        sf_ref[0, pl.ds(g, r, stride=S5_GROUPS), :] = s[:, :LANES]
        sb_ref[0, pl.ds(g, r, stride=S5_GROUPS), :] = s[:, LANES:]


def _lane_half(tm, lane_block):
    return pl.BlockSpec((1, tm, LANES), lambda i, j: (i, j, lane_block))


def _s5_chunk_states(u, m_state, tm):
    b, s, _ = u.shape
    out = jax.ShapeDtypeStruct((b, s, LANES), F32)
    return pl.pallas_call(
        _s5_state_kernel,
        grid=(b, s // tm),
        in_specs=[_lane_half(tm, 0), _lane_half(tm, 1), _const_spec(m_state.shape)],
        out_specs=[_lane_half(tm, 0), _lane_half(tm, 0)],
        out_shape=[out, out],
        compiler_params=_cparams("arbitrary", "arbitrary"),
        name="s5_chunk_states",
    )(u, u, m_state)


def _s5_scan_kernel(scf_ref, scb_ref, sf_ref, sb_ref, a_ref, hcf_ref, hcb_ref, hf_ref, hb_ref, st_ref, wf_ref, wb_ref,
                    hrf_ref, hif_ref, hrb_ref, hib_ref):
    half = LANES // 2
    a_re = a_ref[:, :LANES]
    a_im = a_ref[:, LANES:]
    fwd_lane = lax.broadcasted_iota(jnp.int32, a_re.shape, 1) < half

    def swap(v):
        n, g, w = v.shape
        return pltpu.roll(v.reshape(n * g, w), half, 1).reshape(n, g, w)

    def run(in_f, in_b, out_f, out_b, carry):
        n = in_f.shape[1]
        wf_ref[0:n] = swap(in_f[0])
        wb_ref[0:n] = swap(in_b[0])

        def body(i, c):
            h_re, h_im = c
            r = n - 1 - i
            hrf_ref[i] = h_re
            hif_ref[i] = h_im
            hrb_ref[r] = h_re
            hib_ref[r] = h_im
            s_re = jnp.where(fwd_lane, in_f[0, i], wb_ref[r])
            s_im = jnp.where(fwd_lane, wf_ref[i], in_b[0, r])
            return (a_re * h_re - a_im * h_im + s_re, a_re * h_im + a_im * h_re + s_im)

        carry = lax.fori_loop(0, n, body, carry)
        out_f[0] = jnp.where(fwd_lane, hrf_ref[0:n], swap(hif_ref[0:n]))
        out_b[0] = jnp.where(fwd_lane, swap(hrb_ref[0:n]), hib_ref[0:n])
        return carry

    @pl.when(pl.program_id(1) == 0)
    def _():
        zero = jnp.zeros(a_re.shape, F32)
        c = run(scf_ref, scb_ref, hcf_ref, hcb_ref, (zero, zero))
        st_ref[0] = c[0]
        st_ref[1] = c[1]

    c = run(sf_ref, sb_ref, hf_ref, hb_ref, (st_ref[0], st_ref[1]))
    st_ref[0] = c[0]
    st_ref[1] = c[1]


def _s5_scan(sc_f, sc_b, sl_f, sl_b, a16):
    b, nc, g, _ = sc_f.shape
    c = sl_f.shape[1]
    cb = min(128, c)
    nb = c // cb
    ctx = pl.BlockSpec((1, nc, g, LANES), lambda i, j: (i, 0, 0, 0))
    up = pl.BlockSpec((1, cb, g, LANES), lambda i, j: (i, j, 0, 0))
    down = pl.BlockSpec((1, cb, g, LANES), lambda i, j: (i, nb - 1 - j, 0, 0))
    return pl.pallas_call(
        _s5_scan_kernel,
        grid=(b, nb),
        in_specs=[ctx, ctx, up, down, pl.BlockSpec((g, 2 * LANES), lambda i, j: (0, 0))],
        out_specs=[ctx, ctx, up, down],
        out_shape=[jax.ShapeDtypeStruct(sc_f.shape, F32), jax.ShapeDtypeStruct(sc_f.shape, F32),
                   jax.ShapeDtypeStruct(sl_f.shape, F32), jax.ShapeDtypeStruct(sl_f.shape, F32)],
        scratch_shapes=[pltpu.VMEM((2, g, LANES), F32)] + [pltpu.VMEM((max(cb, nc), g, LANES), F32)] * 6,
        compiler_params=_cparams("arbitrary", "arbitrary"),
        name="s5_chunk_scan",
    )(sc_f, sc_b, sl_f, sl_b, a16)


def _s5_out_kernel(ua_ref, ub_ref, hf_ref, hb_ref, mi_ref, mo_ref, ya_ref, yb_ref):
    lo, hi = [], []
    for g, f in enumerate(_chunk_flat(ua_ref, ub_ref)):
        hf = _strided_rows(hf_ref, g, S5_GROUPS).astype(BF16)
        hb = _strided_rows(hb_ref, g, S5_GROUPS).astype(BF16)
        y = _dot(f, mi_ref[g]) + _dot(hf, mo_ref[g, :LANES, :]) + _dot(hb, mo_ref[g, LANES:, :])
        lo.append(y[:, :LANES])
        hi.append(y[:, LANES:])
    lo, hi = _block_transpose(lo, hi)
    r = ua_ref.shape[1] // S5_CHUNK
    for t in range(S5_CHUNK):
        ya_ref[0, pl.ds(t, r, stride=S5_CHUNK), :] = lo[t]
        yb_ref[0, pl.ds(t, r, stride=S5_CHUNK), :] = hi[t]


def _s5_outputs(u, h_f, h_b, m_intra, m_out, tm):
    b, s, _ = u.shape
    out = jax.ShapeDtypeStruct((b, s, LANES), F32)
    return pl.pallas_call(
        _s5_out_kernel,
        grid=(b, s // tm),
        in_specs=[_lane_half(tm, 0), _lane_half(tm, 1), _lane_half(tm, 0), _lane_half(tm, 0),
                  _const_spec(m_intra.shape), _const_spec(m_out.shape)],
        out_specs=[_lane_half(tm, 0), _lane_half(tm, 0)],
        out_shape=[out, out],
        compiler_params=_cparams("arbitrary", "arbitrary"),
        name="s5_chunk_outputs",
    )(u, u, h_f, h_b, m_intra, m_out)


def _s5_matrices(lam_re, lam_im, log_dt, b_re, b_im, c_re, c_im):
    hp = lax.Precision.HIGHEST
    t = S5_CHUNK
    g, p, h = S5_GROUPS, S5_STATE, S5_GROUP
    lam = lax.complex(jnp.minimum(lam_re, -1e-4), lam_im)
    dt = jnp.exp(log_dt)[..., None]
    steps = jnp.arange(t + 1, dtype=F32)[:, None, None, None]
    apow = jnp.exp(lam[None] * dt[None] * steps)
    bb = ((apow[1] - 1.0) / lam)[..., None] * lax.complex(b_re, b_im)
    cc = lax.complex(c_re, c_im)
    taps = jnp.einsum('dghp,jdgp,dgpk->djghk', cc, apow[:t], bb, precision=hp).real
    s_idx = jnp.arange(t)[:, None]
    t_idx = jnp.arange(t)[None, :]
    kf = taps[0][jnp.clip(t_idx - s_idx, 0, t - 1)] * (t_idx >= s_idx)[:, :, None, None, None].astype(F32)
    kb = taps[1][jnp.clip(s_idx - t_idx, 0, t - 1)] * (s_idx >= t_idx)[:, :, None, None, None].astype(F32)
    m_intra = (kf + kb).transpose(2, 0, 4, 1, 3).reshape(g, t * h, t * h)
    wf = apow[:t][::-1, 0][:, :, :, None] * bb[0][None]
    wb = apow[:t, 1][:, :, :, None] * bb[1][None]
    to_rows = lambda w: w.transpose(1, 0, 3, 2).reshape(g, t * h, p)
    m_state = jnp.concatenate([to_rows(wf.real), to_rows(wf.imag), to_rows(wb.real), to_rows(wb.imag)], axis=-1)
    of = cc[0][None] * apow[1:, 0][:, :, None, :]
    ob = cc[1][None] * apow[1:, 1][::-1][:, :, None, :]
    to_cols = lambda w: w.transpose(1, 3, 0, 2).reshape(g, p, t * h)
    m_out = jnp.concatenate([to_cols(of.real), -to_cols(of.imag), to_cols(ob.real), -to_cols(ob.imag)], axis=1)
    a_t = apow[t]
    a16 = jnp.concatenate([a_t[0].real, a_t[1].real, a_t[0].imag, a_t[1].imag], axis=-1)
    return m_intra.astype(BF16), m_state.astype(BF16), m_out.astype(BF16), a16


def _s5_mixer(u_ctx, u_lat, mats):
    m_intra, m_state, m_out, a16 = mats
    b, lc, w = u_ctx.shape
    ll = u_lat.shape[1]
    g, t = S5_GROUPS, S5_CHUNK
    tm_c, tm_l = min(1024, lc), min(1024, ll)
    chunks = lambda x: x.reshape(b, x.shape[1] // g, g, LANES)
    rows = lambda x: x.reshape(b, x.shape[1] * g, LANES)
    sc_f, sc_b = _s5_chunk_states(u_ctx, m_state, tm_c)
    sl_f, sl_b = _s5_chunk_states(u_lat, m_state, tm_l)
    hc_f, hc_b, hl_f, hl_b = _s5_scan(chunks(sc_f), chunks(sc_b), chunks(sl_f), chunks(sl_b), a16)
    y_ctx = _s5_outputs(u_ctx, rows(hc_f), rows(hc_b), m_intra, m_out, tm_c)
    y_lat = _s5_outputs(u_lat, rows(hl_f), rows(hl_b), m_intra, m_out, tm_l)
    return y_ctx, y_lat


def _banded_time_filter(xa_ref, xb_ref, m_ref, ya_ref, yb_ref, split):
    r = ya_ref.shape[0] // S5_CHUNK
    flo, fhi = _block_transpose([xa_ref[pl.ds(t, r + 2, stride=S5_CHUNK), :] for t in range(S5_CHUNK)],
                                [xb_ref[pl.ds(t, r + 2, stride=S5_CHUNK), :] for t in range(S5_CHUNK)])
    out_lo, out_hi = [], []
    for g in range(len(flo)):
        f = jnp.concatenate([flo[g], fhi[g]], axis=1)
        y = None
        for c in range(3):
            fc = f[c:c + r]
            top = fc.astype(BF16)
            parts = [top, (fc - top.astype(F32)).astype(BF16)] if split else [top]
            for p in parts:
                term = _dot(p, m_ref[g, c])
                y = term if y is None else y + term
        out_lo.append(y[:, :LANES])
        out_hi.append(y[:, LANES:])
    out_lo, out_hi = _block_transpose(out_lo, out_hi)
    for t in range(S5_CHUNK):
        ya_ref[pl.ds(t, r, stride=S5_CHUNK), :] = out_lo[t]
        yb_ref[pl.ds(t, r, stride=S5_CHUNK), :] = out_hi[t]
    return jnp.concatenate([ya_ref[...], yb_ref[...]], axis=1)


def _local_kernel(zc_ref, zcp_ref, zcn_ref, zp_ref, zpp_ref, zpn_ref, cm_ref, cb_ref, lg_ref, lb_ref, pw_ref, ps_ref,
                  conv_ref, pool_ref, xa_ref, xb_ref, ya_ref, yb_ref, pe_ref, *, seq_len):
    tm = zc_ref.shape[1]
    j = pl.program_id(1)
    first = j == 0
    last = j == pl.num_programs(1) - 1

    def fill(prev, main, nxt):
        for rows, val in ((slice(0, HALO), jnp.where(first, 0.0, prev)), (slice(HALO, HALO + tm), main),
                          (slice(HALO + tm, HALO + tm + HALO), jnp.where(last, 0.0, nxt))):
            xa_ref[rows] = val[:, :LANES]
            xb_ref[rows] = val[:, LANES:]

    def glu(z):
        return z[:, :CONV_WIDTH] * jax.nn.sigmoid(z[:, CONV_WIDTH:])

    fill(glu(zcp_ref[0]), glu(zc_ref[0]), glu(zcn_ref[0]))
    acc = _banded_time_filter(xa_ref, xb_ref, cm_ref, ya_ref, yb_ref, split=False) + cb_ref[...]
    mu = jnp.mean(acc, axis=-1, keepdims=True)
    xc = acc - mu
    var = jnp.mean(xc * xc, axis=-1, keepdims=True)
    y = xc * lax.rsqrt(var + EPS) * lg_ref[...] + lb_ref[...]
    conv_ref[0] = (y * jax.nn.sigmoid(y)).astype(BF16)

    pe_ref[0:HALO] = jnp.where(first, 0.0, zpp_ref[0])
    pe_ref[HALO:HALO + tm] = zp_ref[0]
    pe_ref[HALO + tm:] = jnp.where(last, 0.0, zpn_ref[0])

    def tap(dlt):
        return pe_ref[pl.ds(HALO + dlt, tm), :]

    z = tap(0)
    sums = {}
    run = z + tap(1)
    sums[2] = run
    run = run + tap(-1) + tap(2)
    sums[4] = run
    run = run + tap(-3) + tap(-2) + tap(3) + tap(4)
    sums[8] = run
    for dlt in (-7, -6, -5, -4, 5, 6, 7, 8):
        run = run + tap(dlt)
    sums[16] = run
    pos = j * tm + lax.broadcasted_iota(jnp.int32, (tm, 1), 0)
    lane = lax.broadcasted_iota(jnp.int32, (tm, POOL_WIDTH), 1)
    mean = None
    for gi, w in enumerate(POOL_WINDOWS):
        lo = jnp.maximum(pos - (w - 1) // 2, 0)
        hi = jnp.minimum(pos + w // 2, seq_len - 1)
        m = sums[w] / (hi - lo + 1).astype(F32)
        mean = m if mean is None else jnp.where(lane >= gi * POOL_GROUP, m, mean)
    d = (mean - z).astype(BF16)
    pool_ref[0] = (_dot(d, pw_ref[...]) * ps_ref[...]).astype(BF16)


def _local_mixers(zc, zp, lw, tm):
    b, s, _ = zc.shape
    nh = tm // HALO
    nblk = s // HALO
    main = lambda w: pl.BlockSpec((1, tm, w), lambda i, j: (i, j, 0))
    prev = lambda w: pl.BlockSpec((1, HALO, w), lambda i, j: (i, jnp.maximum(j * nh - 1, 0), 0))
    nxt = lambda w: pl.BlockSpec((1, HALO, w), lambda i, j: (i, jnp.minimum((j + 1) * nh, nblk - 1), 0))
    cw2, pw = 2 * CONV_WIDTH, POOL_WIDTH
    return pl.pallas_call(
        functools.partial(_local_kernel, seq_len=s),
        grid=(b, s // tm),
        in_specs=[main(cw2), prev(cw2), nxt(cw2), main(pw), prev(pw), nxt(pw),
                  _const_spec(lw["conv_m"].shape), _const_spec((1, CONV_WIDTH)), _const_spec((1, CONV_WIDTH)),
                  _const_spec((1, CONV_WIDTH)), _const_spec((pw, pw)), _const_spec((1, pw))],
        out_specs=[main(CONV_WIDTH), main(pw)],
        out_shape=[jax.ShapeDtypeStruct((b, s, CONV_WIDTH), BF16), jax.ShapeDtypeStruct((b, s, pw), BF16)],
        scratch_shapes=[pltpu.VMEM((tm + 2 * HALO, LANES), F32), pltpu.VMEM((tm + 2 * HALO, LANES), F32),
                        pltpu.VMEM((tm, LANES), F32), pltpu.VMEM((tm, LANES), F32),
                        pltpu.VMEM((tm + 2 * HALO, pw), F32)],
        compiler_params=_cparams("arbitrary", "arbitrary"),
        name="conv_pool",
    )(zc, zc, zc, zp, zp, zp, lw["conv_m"], lw["conv_b"], lw["conv_ln_g"], lw["conv_ln_b"], lw["pool_w"],
      lw["pool_scale"])


def _attn_kernel(*refs, tk, has_lat):
    if has_lat:
        q_ref, kc_ref, vtc_ref, kl_ref, vtl_ref, o_ref, s_ref = refs
    else:
        q_ref, kc_ref, vtc_ref, o_ref = refs
    tq = q_ref.shape[1]
    heads = range(ATTN_HEADS)
    qs = [q_ref[0, :, h * HEAD_PAD:(h + 1) * HEAD_PAD] for h in heads]

    def scores(h, k):
        return lax.dot_general(k, qs[h], (((1,), (1,)), ((), ())), preferred_element_type=F32)

    def consume(s, mt, vt, state):
        m, l, acc = state
        m_new = jnp.maximum(m, mt)
        p = jnp.exp2(s - m_new)
        alpha = jnp.exp2(m - m_new)
        l = alpha * l + jnp.sum(p, axis=0, keepdims=True)
        acc = alpha * acc + _dot(vt, p.astype(BF16))
        return m_new, l, acc

    init = (jnp.full((1, tq), -1e30, F32), jnp.zeros((1, tq), F32), jnp.zeros((MLA_V, tq), F32))
    state = []
    for h in heads:
        s = scores(h, kc_ref[0, :, h * HEAD_PAD:(h + 1) * HEAD_PAD])
        state.append(consume(s, jnp.max(s, axis=0, keepdims=True), vtc_ref[0, h * MLA_V:(h + 1) * MLA_V, :], init))

    if has_lat:
        n_tiles = kl_ref.shape[1] // tk

        def produce(h, slot, i):
            off = pl.multiple_of(i * tk, tk)
            s = scores(h, kl_ref[0, pl.ds(off, tk), h * HEAD_PAD:(h + 1) * HEAD_PAD])
            s_ref[h, slot] = s
            return jnp.max(s, axis=0, keepdims=True)

        def use(h, slot, i, mt, st):
            off = pl.multiple_of(i * tk, tk)
            return consume(s_ref[h, slot], mt, vtl_ref[0, h * MLA_V:(h + 1) * MLA_V, pl.ds(off, tk)], st)

        def pair(i0, st, mt0, more):
            mt1 = [produce(h, 1, i0 + 1) for h in heads]
            st = [use(h, 0, i0, mt0[h], st[h]) for h in heads]
            if more:
                mt0 = [produce(h, 0, i0 + 2) for h in heads]
            st = [use(h, 1, i0 + 1, mt1[h], st[h]) for h in heads]
            return st, mt0

        mt0 = [produce(h, 0, 0) for h in heads]
        state, mt0 = lax.fori_loop(0, n_tiles // 2 - 1, lambda j, c: pair(2 * j, c[0], c[1], True), (state, mt0))
        state, _ = pair(n_tiles - 2, state, mt0, False)
    out_t = jnp.concatenate([acc / l for _, l, acc in state], axis=0)
    o_ref[0] = out_t.T.astype(BF16)


def _attention(q, k_ctx, vt_ctx, k_lat, vt_lat, tq, tk):
    b, lq, _ = q.shape
    lc = k_ctx.shape[1]
    has_lat = k_lat is not None
    nh = ATTN_HEADS
    in_specs = [pl.BlockSpec((1, tq, nh * HEAD_PAD), lambda i, p, j: (i, j, p)),
                pl.BlockSpec((1, lc, nh * HEAD_PAD), lambda i, p, j: (i, 0, p)),
                pl.BlockSpec((1, nh * MLA_V, lc), lambda i, p, j: (i, p, 0))]
    args = [q, k_ctx, vt_ctx]
    scratch = []
    if has_lat:
        ll = k_lat.shape[1]
        assert ll % (2 * tk) == 0
        in_specs += [pl.BlockSpec((1, ll, nh * HEAD_PAD), lambda i, p, j: (i, 0, p)),
                     pl.BlockSpec((1, nh * MLA_V, ll), lambda i, p, j: (i, p, 0))]
        args += [k_lat, vt_lat]
        scratch = [pltpu.VMEM((nh, 2, tk, tq), F32)]
    return pl.pallas_call(
        functools.partial(_attn_kernel, tk=tk, has_lat=has_lat),
        grid=(b, MLA_HEADS // nh, lq // tq),
        in_specs=in_specs,
        out_specs=pl.BlockSpec((1, tq, nh * MLA_V), lambda i, p, j: (i, j, p)),
        out_shape=jax.ShapeDtypeStruct((b, lq, MLA_WIDTH), BF16),
        scratch_shapes=scratch,
        compiler_params=_cparams("arbitrary", "arbitrary", "arbitrary"),
        name="mla_attention",
    )(*args)


def _mixout_kernel(x_ref, mod_ref, post_ref, y5a_ref, y5b_ref, u_ref, d_ref, wglu_ref, bglu_ref, conv_ref, pool_ref,
                   att_ref, wout_ref, o_ref):
    x = x_ref[0]
    _, _, gate = _mod_slices(mod_ref, 3)
    y5 = jax.nn.gelu(jnp.concatenate([y5a_ref[0], y5b_ref[0]], axis=1) + d_ref[...] * u_ref[0])
    s5 = y5 * jax.nn.sigmoid(_dot(y5.astype(BF16), wglu_ref[...]) + bglu_ref[...])
    o1, o2, o3 = S5_WIDTH, S5_WIDTH + CONV_WIDTH, S5_WIDTH + CONV_WIDTH + POOL_WIDTH
    y = (_dot(s5.astype(BF16), wout_ref[0:o1, :]) + _dot(conv_ref[0], wout_ref[o1:o2, :])
         + _dot(pool_ref[0], wout_ref[o2:o3, :]) + _dot(att_ref[0], wout_ref[o3:, :]))
    o_ref[0] = x + gate * _rms(y, post_ref[...])


def _mix_out(x, mod_rows, post_g, y5, u, conv, pool, att, lw, tm):
    b, s, d = x.shape
    per_batch = mod_rows.shape[0] > 1
    row = lambda w: pl.BlockSpec((1, tm, w), lambda i, j: (i, j, 0))
    return pl.pallas_call(
        _mixout_kernel,
        grid=(b, s // tm),
        in_specs=[
            row(d),
            pl.BlockSpec((1, 1, N_MOD * d), (lambda i, j: (i, 0, 0)) if per_batch else (lambda i, j: (0, 0, 0))),
            _const_spec((1, d)),
            row(LANES), row(LANES), row(S5_WIDTH), _const_spec((1, S5_WIDTH)), _const_spec((S5_WIDTH, S5_WIDTH)),
            _const_spec((1, S5_WIDTH)),
            row(CONV_WIDTH), row(POOL_WIDTH), row(MLA_WIDTH), _const_spec((D_MIX, d)),
        ],
        out_specs=row(d),
        out_shape=jax.ShapeDtypeStruct(x.shape, F32),
        compiler_params=_cparams("arbitrary", "arbitrary"),
        name="mixer_out_proj",
    )(x, mod_rows, post_g.reshape(1, d), y5[0], y5[1], u, lw["s5_d"], lw["s5_w_glu"], lw["s5_b_glu"], conv, pool,
      att, lw["w_out"])


def _rope_partner(w):
    q = MLA_ROPE // 4
    return jnp.concatenate([-w[..., q:2 * q], w[..., 0:q], -w[..., 3 * q:4 * q], w[..., 2 * q:3 * q]], axis=-1)


def _rope_tables(n_lat):
    rows = n_lat // GRID_W
    axis_dim = MLA_ROPE // 2
    inv = ROPE_BASE ** (-jnp.arange(0, axis_dim, 2, dtype=F32) / axis_dim)
    ang_r = jnp.arange(rows, dtype=F32)[:, None] * inv
    ang_c = jnp.arange(GRID_W, dtype=F32)[:, None] * inv
    nq = MLA_ROPE // 4

    def table(fn, fill):
        fr, fc = fn(ang_r), fn(ang_c)
        pad = HEAD_PAD - MLA_NOPE - MLA_ROPE
        by_row = jnp.concatenate([jnp.full((rows, MLA_NOPE), fill, F32), fr, fr, jnp.zeros((rows, 2 * nq), F32),
                                  jnp.full((rows, pad), fill, F32)], axis=1)
        by_col = jnp.concatenate([jnp.zeros((GRID_W, MLA_NOPE + 2 * nq), F32), fc, fc,
                                  jnp.zeros((GRID_W, pad), F32)], axis=1)
        return (by_row[:, None, :] + by_col[None, :, :]).reshape(n_lat, HEAD_PAD)

    return table(jnp.cos, 1.0), table(jnp.sin, 0.0)


def _pad_head(nope, rope):
    r = nope.shape[0]
    z = jnp.zeros((r, MLA_HEADS, HEAD_PAD - MLA_NOPE - MLA_ROPE), nope.dtype)
    return jnp.concatenate([nope, rope, z], axis=-1).reshape(r, MLA_HEADS * HEAD_PAD)


def _band_matrices(w, center):
    n, h = S5_CHUNK, S5_GROUP
    n_taps, width = w.shape
    g = width // h
    c = jnp.arange(3)[:, None, None]
    t_in = jnp.arange(n)[None, :, None]
    t_out = jnp.arange(n)[None, None, :]
    k = t_in + n * (c - 1) - t_out + center
    valid = ((k >= 0) & (k < n_taps))[..., None]
    taps = jnp.where(valid, w[jnp.clip(k, 0, n_taps - 1)], 0.0)
    taps = taps.reshape(3, n, n, g, h).transpose(3, 0, 1, 4, 2)
    eye = jnp.eye(h, dtype=F32)[None, None, None, :, None, :]
    return (taps[..., None] * eye).astype(BF16).reshape(g, 3, n * h, n * h)


def _prep_layer(w_in, w_out, s5_d, s5_w_glu, s5_b_glu, conv_w, conv_b, conv_ln_g, conv_ln_b, pool_w, pool_scale,
                mla_q_norm, mla_w_uq, mla_kv_norm, mla_w_ukv):
    d = D_MODEL
    wi = w_in
    kr = wi[:, IN_KR:D_IN]
    zpad_l = jnp.zeros((d, MLA_NOPE), F32)
    zpad_r = jnp.zeros((d, HEAD_PAD - MLA_NOPE - MLA_ROPE), F32)
    w_in_ext = jnp.concatenate([wi[:, :IN_KR], zpad_l, kr, zpad_r, zpad_l, _rope_partner(kr), zpad_r], axis=1)
    uq = mla_w_uq.reshape(MLA_Q_RANK, MLA_HEADS, MLA_NOPE + MLA_ROPE)
    q_nope, q_rope = uq[..., :MLA_NOPE], uq[..., MLA_NOPE:]
    wq = jnp.concatenate([_pad_head(q_nope, q_rope), _pad_head(0 * q_nope, _rope_partner(q_rope))], axis=1)
    ukv = mla_w_ukv.reshape(MLA_KV_RANK, MLA_HEADS, MLA_NOPE + MLA_V)
    wk = _pad_head(ukv[..., :MLA_NOPE], jnp.zeros((MLA_KV_RANK, MLA_HEADS, MLA_ROPE), F32))
    wv = ukv[..., MLA_NOPE:].reshape(MLA_KV_RANK, MLA_WIDTH)
    eye = jnp.eye(HEAD_PAD, dtype=F32) * ((jnp.arange(HEAD_PAD) >= MLA_NOPE)
                                          & (jnp.arange(HEAD_PAD) < MLA_NOPE + MLA_ROPE))[:, None]
    place = jnp.tile(eye, (1, MLA_HEADS))
    pool_bd = jnp.zeros((POOL_WIDTH, POOL_WIDTH), F32)
    for gi in range(len(POOL_WINDOWS)):
        sl = slice(gi * POOL_GROUP, (gi + 1) * POOL_GROUP)
        pool_bd = pool_bd.at[sl, sl].set(pool_w[gi])
    return {
        "w_in": w_in_ext.astype(BF16), "w_out": w_out.astype(BF16),
        "q_norm": mla_q_norm.reshape(1, -1), "kv_norm": mla_kv_norm.reshape(1, -1),
        "wq": wq.astype(BF16), "wk": wk.astype(BF16), "wv": wv.astype(BF16), "place": place.astype(BF16),
        "s5_d": s5_d.reshape(1, -1), "s5_w_glu": s5_w_glu.astype(BF16), "s5_b_glu": s5_b_glu.reshape(1, -1),
        "conv_m": _band_matrices(conv_w, CONV_K // 2),
        "conv_b": conv_b.reshape(1, -1), "conv_ln_g": conv_ln_g.reshape(1, -1),
        "conv_ln_b": conv_ln_b.reshape(1, -1),
        "pool_w": pool_bd.astype(BF16), "pool_scale": pool_scale.reshape(1, -1),
    }


def kernel(x, c, ctx, c_ctx, w_ada, b_ada, norm_pre, norm_post, ffn_w_gate, ffn_w_up, ffn_w_down, w_in, w_out,
           s5_lam_re, s5_lam_im, s5_log_dt, s5_b_re, s5_b_im, s5_c_re, s5_c_im, s5_d, s5_w_glu, s5_b_glu, conv_w,
           conv_b, conv_ln_g, conv_ln_b, pool_w, pool_scale, mla_q_norm, mla_w_uq, mla_kv_norm, mla_w_ukv):
    batch, n_lat, d = x.shape
    n_ctx = ctx.shape[1]
    depth = w_ada.shape[0]
    assert d == D_MODEL and batch + 1 <= 8
    tm_lat = min(512, n_lat)
    tm_ffn = tm_lat
    tm_ctx = min(512, n_ctx)
    tq_lat = min(1024, n_lat)
    tk = min(1024, n_lat // 2)
    assert n_lat % tm_lat == 0 and n_ctx % S5_CHUNK == 0 and n_lat % GRID_W == 0

    c_rows = jnp.zeros((8, d), F32).at[:batch].set(c).at[batch].set(c_ctx)
    mod = _modulation(c_rows, w_ada, b_ada)
    cos_lat, sin_lat = _rope_tables(n_lat)
    cos_ctx = jnp.ones((n_ctx, HEAD_PAD), F32)
    sin_ctx = jnp.zeros((n_ctx, HEAD_PAD), F32)
    wg = ffn_w_gate.astype(BF16)
    wu = ffn_w_up.astype(BF16)
    wd = ffn_w_down.astype(BF16)
    layer_params = jax.vmap(_prep_layer)(w_in, w_out, s5_d, s5_w_glu, s5_b_glu, conv_w, conv_b, conv_ln_g, conv_ln_b,
                                         pool_w, pool_scale, mla_q_norm, mla_w_uq, mla_kv_norm, mla_w_ukv)
    s5_params = jax.vmap(_s5_matrices)(s5_lam_re, s5_lam_im, s5_log_dt, s5_b_re, s5_b_im, s5_c_re, s5_c_im)

    x_lat, x_ctx = x, ctx
    for l in range(depth):
        last = l == depth - 1
        mod_lat = mod[l, :batch].reshape(batch, 1, N_MOD * d)
        mod_ctx = mod[l, batch].reshape(1, 1, N_MOD * d)
        lw = {name: v[l] for name, v in layer_params.items()}
        s5_mats = tuple(m[l] for m in s5_params)
        x_lat = _ffn(x_lat, mod_lat, 0, norm_pre[l, 0], norm_post[l, 0], wg[l, 0], wu[l, 0], wd[l, 0], tm_ffn)
        x_ctx = _ffn(x_ctx, mod_ctx, 0, norm_pre[l, 0], norm_post[l, 0], wg[l, 0], wu[l, 0], wd[l, 0], tm_ctx)
        u_l, zc_l, zp_l, q_l, k_l, v_l = _inproj(x_lat, mod_lat, norm_pre[l, 1], lw, cos_lat, sin_lat, tm_lat)
        u_c, zc_c, zp_c, q_c, k_c, v_c = _inproj(x_ctx, mod_ctx, norm_pre[l, 1], lw, cos_ctx, sin_ctx, tm_ctx)
        y5_c, y5_l = _s5_mixer(u_c, u_l, s5_mats)
        conv_l, pool_l = _local_mixers(zc_l, zp_l, lw, min(2048, n_lat))
        vt_l, vt_c = v_l.transpose(0, 2, 1), v_c.transpose(0, 2, 1)
        att_l = _attention(q_l, k_c, vt_c, k_l, vt_l, tq_lat, tk)
        x_lat = _mix_out(x_lat, mod_lat, norm_post[l, 1], y5_l, u_l, conv_l, pool_l, att_l, lw, tm_lat)
        if not last:
            conv_c, pool_c = _local_mixers(zc_c, zp_c, lw, tm_ctx)
            att_c = _attention(q_c, k_c, vt_c, None, None, tm_ctx, tk)
            x_ctx = _mix_out(x_ctx, mod_ctx, norm_post[l, 1], y5_c, u_c, conv_c, pool_c, att_c, lw, tm_ctx)
        x_lat = _ffn(x_lat, mod_lat, 6, norm_pre[l, 2], norm_post[l, 2], wg[l, 1], wu[l, 1], wd[l, 1], tm_ffn)
        if not last:
            x_ctx = _ffn(x_ctx, mod_ctx, 6, norm_pre[l, 2], norm_post[l, 2], wg[l, 1], wu[l, 1], wd[l, 1], tm_ctx)
    return x_lat
```

```python
import functools
import math

import jax
import jax.numpy as jnp
from jax import lax
from jax.experimental import pallas as pl
from jax.experimental.pallas import tpu as pltpu

F32 = jnp.float32
BF16 = jnp.bfloat16

D_MODEL = 1024
GRID_W = 64
EPS = 1e-6
N_MOD = 9
MACARON_WEIGHT = 0.5
D_FF = 2816
S5_WIDTH = 256
S5_GROUP = 16
S5_GROUPS = 16
S5_STATE = 64
CONV_WIDTH = 256
CONV_K = 31
POOL_WIDTH = 256
POOL_WINDOWS = (2, 4, 8, 16)
POOL_GROUP = 64
MLA_HEADS = 8
MLA_NOPE = 64
MLA_ROPE = 32
MLA_V = 64
MLA_Q_RANK = 256
MLA_KV_RANK = 128
MLA_WIDTH = MLA_HEADS * MLA_V
ROPE_BASE = 10000.0
D_MIX = S5_WIDTH + CONV_WIDTH + POOL_WIDTH + MLA_WIDTH
IN_S5 = 0
IN_CONV = IN_S5 + S5_WIDTH
IN_POOL = IN_CONV + 2 * CONV_WIDTH
IN_CQ = IN_POOL + POOL_WIDTH
IN_CKV = IN_CQ + MLA_Q_RANK
IN_KR = IN_CKV + MLA_KV_RANK
D_IN = IN_KR + MLA_ROPE

LANES = 128
HEAD_PAD = 128
S5_CHUNK = 16
S5_FLAT = S5_CHUNK * S5_GROUP
HALO = 16
D_IN_EXT = IN_KR + 2 * HEAD_PAD
FF_CHUNKS = ((0, 1024), (1024, 2048), (2048, 2816))
VMEM_LIMIT = 56 * 1024 * 1024
ATTN_HEADS = 2
Q_SCALE = (MLA_NOPE + MLA_ROPE) ** -0.5 * math.log2(math.e)


def _cparams(*sem, flags=None):
    return pltpu.CompilerParams(dimension_semantics=sem, vmem_limit_bytes=VMEM_LIMIT, flags=flags)


def _const_spec(shape):
    nd = len(shape)
    return pl.BlockSpec(shape, lambda *_: (0,) * nd, pipeline_mode=pl.Buffered(1))


def _rms(x, g):
    return x * lax.rsqrt(jnp.mean(x * x, axis=-1, keepdims=True) + EPS) * g


def _dot(a, b):
    return jnp.dot(a, b, preferred_element_type=F32)


def _mod_kernel(c_ref, w_ref, b_ref, o_ref):
    c = c_ref[...]
    h = (c * jax.nn.sigmoid(c)).astype(BF16)
    o_ref[0] = _dot(h, w_ref[0].astype(BF16)) + b_ref[0]


def _modulation(c_rows, w_ada, b_ada):
    depth = w_ada.shape[0]
    d = D_MODEL
    return pl.pallas_call(
        _mod_kernel,
        grid=(depth, N_MOD),
        in_specs=[
            pl.BlockSpec((8, d), lambda l, j: (0, 0)),
            pl.BlockSpec((1, d, d), lambda l, j: (l, 0, j)),
            pl.BlockSpec((1, 1, d), lambda l, j: (l, 0, j)),
        ],
        out_specs=pl.BlockSpec((1, 8, d), lambda l, j: (l, 0, j)),
        out_shape=jax.ShapeDtypeStruct((depth, 8, N_MOD * d), F32),
        compiler_params=_cparams("arbitrary", "arbitrary"),
        name="adaln_mod",
    )(c_rows, w_ada, b_ada.reshape(depth, 1, N_MOD * d))


def _mod_slices(mod_ref, base):
    d = D_MODEL
    return (mod_ref[0, :, base * d:(base + 1) * d], mod_ref[0, :, (base + 1) * d:(base + 2) * d],
            mod_ref[0, :, (base + 2) * d:(base + 3) * d])


def _ffn_kernel(x_ref, mod_ref, pre_ref, post_ref, wg_ref, wu_ref, wd_ref, o_ref, *, base):
    x = x_ref[0]
    shift, scale, gate = _mod_slices(mod_ref, base)
    h = (_rms(x, pre_ref[...]) * (1.0 + scale) + shift).astype(BF16)
    acc = None
    for lo, hi in FF_CHUNKS:
        g = _dot(h, wg_ref[:, lo:hi])
        u = _dot(h, wu_ref[:, lo:hi])
        a = (g * jax.nn.sigmoid(g) * u).astype(BF16)
        part = _dot(a, wd_ref[lo:hi, :])
        acc = part if acc is None else acc + part
    o_ref[0] = x + MACARON_WEIGHT * gate * _rms(acc, post_ref[...])


def _ffn(x, mod_rows, base, pre_g, post_g, wg, wu, wd, tm):
    b, s, d = x.shape
    per_batch = mod_rows.shape[0] > 1
    return pl.pallas_call(
        functools.partial(_ffn_kernel, base=base),
        grid=(b, s // tm),
        in_specs=[
            pl.BlockSpec((1, tm, d), lambda i, j: (i, j, 0)),
            pl.BlockSpec((1, 1, N_MOD * d), (lambda i, j: (i, 0, 0)) if per_batch else (lambda i, j: (0, 0, 0))),
            _const_spec((1, d)), _const_spec((1, d)),
            _const_spec((d, D_FF)), _const_spec((d, D_FF)), _const_spec((D_FF, d)),
        ],
        out_specs=pl.BlockSpec((1, tm, d), lambda i, j: (i, j, 0)),
        out_shape=jax.ShapeDtypeStruct(x.shape, F32),
        compiler_params=_cparams("arbitrary", "arbitrary"),
        name="ffn_half_step",
    )(x, mod_rows, pre_g.reshape(1, d), post_g.reshape(1, d), wg, wu, wd)


def _inproj_kernel(x_ref, mod_ref, pre_ref, win_ref, qn_ref, kvn_ref, wq_ref, wk_ref, wv_ref, place_ref,
                   cos_ref, sin_ref, u_ref, zc_ref, zp_ref, q_ref, k_ref, v_ref):
    x = x_ref[0]
    shift, scale, _ = _mod_slices(mod_ref, 3)
    h = (_rms(x, pre_ref[...]) * (1.0 + scale) + shift).astype(BF16)
    z = _dot(h, win_ref[...])
    u_ref[0] = z[:, IN_S5:IN_CONV]
    zc_ref[0] = z[:, IN_CONV:IN_POOL]
    zp_ref[0] = z[:, IN_POOL:IN_CQ]
    cos = cos_ref[...]
    sin = sin_ref[...]
    cos_h = jnp.concatenate([cos] * MLA_HEADS, axis=1)
    sin_h = jnp.concatenate([sin] * MLA_HEADS, axis=1)
    cqn = _rms(z[:, IN_CQ:IN_CKV], qn_ref[...]).astype(BF16)
    qq = _dot(cqn, wq_ref[...])
    hw = MLA_HEADS * HEAD_PAD
    q = qq[:, :hw] * cos_h + qq[:, hw:] * sin_h
    q_ref[0] = (q * Q_SCALE).astype(BF16)
    ckvn = _rms(z[:, IN_CKV:IN_KR], kvn_ref[...]).astype(BF16)
    kr = z[:, IN_KR:IN_KR + HEAD_PAD] * cos + z[:, IN_KR + HEAD_PAD:IN_KR + 2 * HEAD_PAD] * sin
    k = _dot(ckvn, wk_ref[...]) + _dot(kr.astype(BF16), place_ref[...])
    k_ref[0] = k.astype(BF16)
    v_ref[0] = _dot(ckvn, wv_ref[...]).astype(BF16)


def _inproj(x, mod_rows, pre_g, lw, cos_t, sin_t, tm):
    b, s, d = x.shape
    per_batch = mod_rows.shape[0] > 1
    hw = MLA_HEADS * HEAD_PAD
    row = lambda w: pl.BlockSpec((1, tm, w), lambda i, j: (i, j, 0))
    return pl.pallas_call(
        _inproj_kernel,
        grid=(b, s // tm),
        in_specs=[
            row(d),
            pl.BlockSpec((1, 1, N_MOD * d), (lambda i, j: (i, 0, 0)) if per_batch else (lambda i, j: (0, 0, 0))),
            _const_spec((1, d)),
            _const_spec((d, D_IN_EXT)),
            _const_spec((1, MLA_Q_RANK)), _const_spec((1, MLA_KV_RANK)),
            _const_spec((MLA_Q_RANK, 2 * hw)), _const_spec((MLA_KV_RANK, hw)),
            _const_spec((MLA_KV_RANK, MLA_WIDTH)), _const_spec((HEAD_PAD, hw)),
            pl.BlockSpec((tm, HEAD_PAD), lambda i, j: (j, 0)),
            pl.BlockSpec((tm, HEAD_PAD), lambda i, j: (j, 0)),
        ],
        out_specs=[row(S5_WIDTH), row(2 * CONV_WIDTH), row(POOL_WIDTH), row(hw), row(hw), row(MLA_WIDTH)],
        out_shape=[
            jax.ShapeDtypeStruct((b, s, S5_WIDTH), F32),
            jax.ShapeDtypeStruct((b, s, 2 * CONV_WIDTH), F32),
            jax.ShapeDtypeStruct((b, s, POOL_WIDTH), F32),
            jax.ShapeDtypeStruct((b, s, hw), BF16),
            jax.ShapeDtypeStruct((b, s, hw), BF16),
            jax.ShapeDtypeStruct((b, s, MLA_WIDTH), BF16),
        ],
        compiler_params=_cparams("arbitrary", "arbitrary"),
        name="mixer_in_proj",
    )(x, mod_rows, pre_g.reshape(1, d), lw["w_in"], lw["q_norm"], lw["kv_norm"], lw["wq"], lw["wk"], lw["wv"],
      lw["place"], cos_t, sin_t)


def _block_transpose(lo, hi):
    n = S5_CHUNK
    half = n // 2
    halves = [[lo[i] for i in range(half)] + [hi[i] for i in range(half)],
              [lo[i + half] for i in range(half)] + [hi[i + half] for i in range(half)]]
    blk = lax.broadcasted_iota(jnp.int32, lo[0].shape, 1) // S5_GROUP
    for d in (4, 2, 1):
        keep = (blk & d) == 0
        for hv in halves:
            for i in range(n):
                if i & d:
                    continue
                x, y = hv[i], hv[i + d]
                hv[i] = jnp.where(keep, x, pltpu.roll(y, d * S5_GROUP, 1))
                hv[i + d] = jnp.where(keep, pltpu.roll(x, LANES - d * S5_GROUP, 1), y)
    return halves


def _strided_rows(ref, phase, period):
    return ref[0, pl.ds(phase, ref.shape[1] // period, stride=period), :]


def _chunk_flat(ua_ref, ub_ref):
    lo, hi = _block_transpose([_strided_rows(ua_ref, t, S5_CHUNK) for t in range(S5_CHUNK)],
                              [_strided_rows(ub_ref, t, S5_CHUNK) for t in range(S5_CHUNK)])
    return [jnp.concatenate([lo[g], hi[g]], axis=1).astype(BF16) for g in range(S5_GROUPS)]


def _s5_state_kernel(ua_ref, ub_ref, m_ref, sf_ref, sb_ref):
    r = ua_ref.shape[1] // S5_CHUNK
    for g, f in enumerate(_chunk_flat(ua_ref, ub_ref)):
        s = _dot(f, m_ref[g])
        sf_ref[0, pl.ds(g, r, stride=S5_GROUPS), :] = s[:, :LANES]
        sb_ref[0, pl.ds(g, r, stride=S5_GROUPS), :] = s[:, LANES:]


def _lane_half(tm, lane_block):
    return pl.BlockSpec((1, tm, LANES), lambda i, j: (i, j, lane_block))


def _s5_chunk_states(u, m_state, tm):
    b, s, _ = u.shape
    out = jax.ShapeDtypeStruct((b, s, LANES), F32)
    return pl.pallas_call(
        _s5_state_kernel,
        grid=(b, s // tm),
        in_specs=[_lane_half(tm, 0), _lane_half(tm, 1), _const_spec(m_state.shape)],
        out_specs=[_lane_half(tm, 0), _lane_half(tm, 0)],
        out_shape=[out, out],
        compiler_params=_cparams("arbitrary", "arbitrary"),
        name="s5_chunk_states",
    )(u, u, m_state)


def _s5_scan_kernel(scf_ref, scb_ref, sf_ref, sb_ref, a_ref, hcf_ref, hcb_ref, hf_ref, hb_ref, st_ref, wf_ref, wb_ref,
                    hrf_ref, hif_ref, hrb_ref, hib_ref):
    half = LANES // 2
    a_re = a_ref[:, :LANES]
    a_im = a_ref[:, LANES:]
    fwd_lane = lax.broadcasted_iota(jnp.int32, a_re.shape, 1) < half

    def swap(v):
        n, g, w = v.shape
        return pltpu.roll(v.reshape(n * g, w), half, 1).reshape(n, g, w)

    def run(in_f, in_b, out_f, out_b, carry):
        n = in_f.shape[1]
        wf_ref[0:n] = swap(in_f[0])
        wb_ref[0:n] = swap(in_b[0])

        def body(i, c):
            h_re, h_im = c
            r = n - 1 - i
            hrf_ref[i] = h_re
            hif_ref[i] = h_im
            hrb_ref[r] = h_re
            hib_ref[r] = h_im
            s_re = jnp.where(fwd_lane, in_f[0, i], wb_ref[r])
            s_im = jnp.where(fwd_lane, wf_ref[i], in_b[0, r])
            return (a_re * h_re - a_im * h_im + s_re, a_re * h_im + a_im * h_re + s_im)

        carry = lax.fori_loop(0, n, body, carry)
        out_f[0] = jnp.where(fwd_lane, hrf_ref[0:n], swap(hif_ref[0:n]))
        out_b[0] = jnp.where(fwd_lane, swap(hrb_ref[0:n]), hib_ref[0:n])
        return carry

    @pl.when(pl.program_id(1) == 0)
    def _():
        zero = jnp.zeros(a_re.shape, F32)
        c = run(scf_ref, scb_ref, hcf_ref, hcb_ref, (zero, zero))
        st_ref[0] = c[0]
        st_ref[1] = c[1]

    c = run(sf_ref, sb_ref, hf_ref, hb_ref, (st_ref[0], st_ref[1]))
    st_ref[0] = c[0]
    st_ref[1] = c[1]


def _s5_scan(sc_f, sc_b, sl_f, sl_b, a16):
    b, nc, g, _ = sc_f.shape
    c = sl_f.shape[1]
    cb = min(128, c)
    nb = c // cb
    ctx = pl.BlockSpec((1, nc, g, LANES), lambda i, j: (i, 0, 0, 0))
    up = pl.BlockSpec((1, cb, g, LANES), lambda i, j: (i, j, 0, 0))
    down = pl.BlockSpec((1, cb, g, LANES), lambda i, j: (i, nb - 1 - j, 0, 0))
    return pl.pallas_call(
        _s5_scan_kernel,
        grid=(b, nb),
        in_specs=[ctx, ctx, up, down, pl.BlockSpec((g, 2 * LANES), lambda i, j: (0, 0))],
        out_specs=[ctx, ctx, up, down],
        out_shape=[jax.ShapeDtypeStruct(sc_f.shape, F32), jax.ShapeDtypeStruct(sc_f.shape, F32),
                   jax.ShapeDtypeStruct(sl_f.shape, F32), jax.ShapeDtypeStruct(sl_f.shape, F32)],
        scratch_shapes=[pltpu.VMEM((2, g, LANES), F32)] + [pltpu.VMEM((max(cb, nc), g, LANES), F32)] * 6,
        compiler_params=_cparams("arbitrary", "arbitrary"),
        name="s5_chunk_scan",
    )(sc_f, sc_b, sl_f, sl_b, a16)


def _s5_out_kernel(ua_ref, ub_ref, hf_ref, hb_ref, mi_ref, mo_ref, ya_ref, yb_ref):
    lo, hi = [], []
    for g, f in enumerate(_chunk_flat(ua_ref, ub_ref)):
        hf = _strided_rows(hf_ref, g, S5_GROUPS).astype(BF16)
        hb = _strided_rows(hb_ref, g, S5_GROUPS).astype(BF16)
        y = _dot(f, mi_ref[g]) + _dot(hf, mo_ref[g, :LANES, :]) + _dot(hb, mo_ref[g, LANES:, :])
        lo.append(y[:, :LANES])
        hi.append(y[:, LANES:])
    lo, hi = _block_transpose(lo, hi)
    r = ua_ref.shape[1] // S5_CHUNK
    for t in range(S5_CHUNK):
        ya_ref[0, pl.ds(t, r, stride=S5_CHUNK), :] = lo[t]
        yb_ref[0, pl.ds(t, r, stride=S5_CHUNK), :] = hi[t]


def _s5_outputs(u, h_f, h_b, m_intra, m_out, tm):
    b, s, _ = u.shape
    out = jax.ShapeDtypeStruct((b, s, LANES), F32)
    return pl.pallas_call(
        _s5_out_kernel,
        grid=(b, s // tm),
        in_specs=[_lane_half(tm, 0), _lane_half(tm, 1), _lane_half(tm, 0), _lane_half(tm, 0),
                  _const_spec(m_intra.shape), _const_spec(m_out.shape)],
        out_specs=[_lane_half(tm, 0), _lane_half(tm, 0)],
        out_shape=[out, out],
        compiler_params=_cparams("arbitrary", "arbitrary"),
        name="s5_chunk_outputs",
    )(u, u, h_f, h_b, m_intra, m_out)


def _s5_matrices(lam_re, lam_im, log_dt, b_re, b_im, c_re, c_im):
    hp = lax.Precision.HIGHEST
    t = S5_CHUNK
    g, p, h = S5_GROUPS, S5_STATE, S5_GROUP
    lam = lax.complex(jnp.minimum(lam_re, -1e-4), lam_im)
    dt = jnp.exp(log_dt)[..., None]
    steps = jnp.arange(t + 1, dtype=F32)[:, None, None, None]
    apow = jnp.exp(lam[None] * dt[None] * steps)
    bb = ((apow[1] - 1.0) / lam)[..., None] * lax.complex(b_re, b_im)
    cc = lax.complex(c_re, c_im)
    taps = jnp.einsum('dghp,jdgp,dgpk->djghk', cc, apow[:t], bb, precision=hp).real
    by_dist = jnp.concatenate([taps[1, :0:-1], (taps[0, 0] + taps[1, 0])[None], taps[0, 1:]], axis=0)
    by_dist = by_dist.transpose(0, 1, 3, 2).astype(BF16)
    dist = jnp.arange(t)[None, :] - jnp.arange(t)[:, None] + (t - 1)
    m_intra = by_dist[dist].transpose(2, 0, 3, 1, 4).reshape(g, t * h, t * h)
    wf = apow[:t][::-1, 0][:, :, :, None] * bb[0][None]
    wb = apow[:t, 1][:, :, :, None] * bb[1][None]
    to_rows = lambda w: w.transpose(1, 0, 3, 2).reshape(g, t * h, p)
    m_state = jnp.concatenate([to_rows(wf.real), to_rows(wf.imag), to_rows(wb.real), to_rows(wb.imag)], axis=-1)
    of = cc[0][None] * apow[1:, 0][:, :, None, :]
    ob = cc[1][None] * apow[1:, 1][::-1][:, :, None, :]
    to_cols = lambda w: w.transpose(1, 3, 0, 2).reshape(g, p, t * h)
    m_out = jnp.concatenate([to_cols(of.real), -to_cols(of.imag), to_cols(ob.real), -to_cols(ob.imag)], axis=1)
    a_t = apow[t]
    a16 = jnp.concatenate([a_t[0].real, a_t[1].real, a_t[0].imag, a_t[1].imag], axis=-1)
    return m_intra, m_state.astype(BF16), m_out.astype(BF16), a16


def _s5_mixer(u_ctx, u_lat, mats):
    m_intra, m_state, m_out, a16 = mats
    b, lc, w = u_ctx.shape
    ll = u_lat.shape[1]
    g, t = S5_GROUPS, S5_CHUNK
    tm_c, tm_l = min(1024, lc), min(1024, ll)
    chunks = lambda x: x.reshape(b, x.shape[1] // g, g, LANES)
    rows = lambda x: x.reshape(b, x.shape[1] * g, LANES)
    sc_f, sc_b = _s5_chunk_states(u_ctx, m_state, tm_c)
    sl_f, sl_b = _s5_chunk_states(u_lat, m_state, tm_l)
    hc_f, hc_b, hl_f, hl_b = _s5_scan(chunks(sc_f), chunks(sc_b), chunks(sl_f), chunks(sl_b), a16)
    y_ctx = _s5_outputs(u_ctx, rows(hc_f), rows(hc_b), m_intra, m_out, tm_c)
    y_lat = _s5_outputs(u_lat, rows(hl_f), rows(hl_b), m_intra, m_out, tm_l)
    return y_ctx, y_lat


def _banded_time_filter(xa_ref, xb_ref, m_ref, ya_ref, yb_ref, split):
    r = ya_ref.shape[0] // S5_CHUNK
    flo, fhi = _block_transpose([xa_ref[pl.ds(t, r + 2, stride=S5_CHUNK), :] for t in range(S5_CHUNK)],
                                [xb_ref[pl.ds(t, r + 2, stride=S5_CHUNK), :] for t in range(S5_CHUNK)])
    out_lo, out_hi = [], []
    for g in range(len(flo)):
        f = jnp.concatenate([flo[g], fhi[g]], axis=1)
        y = None
        for c in range(3):
            fc = f[c:c + r]
            top = fc.astype(BF16)
            parts = [top, (fc - top.astype(F32)).astype(BF16)] if split else [top]
            for p in parts:
                term = _dot(p, m_ref[3 * g + c])
                y = term if y is None else y + term
        out_lo.append(y[:, :LANES])
        out_hi.append(y[:, LANES:])
    out_lo, out_hi = _block_transpose(out_lo, out_hi)
    for t in range(S5_CHUNK):
        ya_ref[pl.ds(t, r, stride=S5_CHUNK), :] = out_lo[t]
        yb_ref[pl.ds(t, r, stride=S5_CHUNK), :] = out_hi[t]
    return jnp.concatenate([ya_ref[...], yb_ref[...]], axis=1)


def _local_kernel(zc_ref, zcp_ref, zcn_ref, zp_ref, zpp_ref, zpn_ref, cm_ref, cb_ref, lg_ref, lb_ref, pw_ref, ps_ref,
                  conv_ref, pool_ref, xa_ref, xb_ref, ya_ref, yb_ref, pe_ref, *, seq_len):
    tm = zc_ref.shape[1]
    j = pl.program_id(1)
    first = j == 0
    last = j == pl.num_programs(1) - 1

    def fill(prev, main, nxt):
        for rows, val in ((slice(0, HALO), jnp.where(first, 0.0, prev)), (slice(HALO, HALO + tm), main),
                          (slice(HALO + tm, HALO + tm + HALO), jnp.where(last, 0.0, nxt))):
            xa_ref[rows] = val[:, :LANES]
            xb_ref[rows] = val[:, LANES:]

    def glu(z):
        return z[:, :CONV_WIDTH] * jax.nn.sigmoid(z[:, CONV_WIDTH:])

    fill(glu(zcp_ref[0]), glu(zc_ref[0]), glu(zcn_ref[0]))
    acc = _banded_time_filter(xa_ref, xb_ref, cm_ref, ya_ref, yb_ref, split=False) + cb_ref[...]
    mu = jnp.mean(acc, axis=-1, keepdims=True)
    xc = acc - mu
    var = jnp.mean(xc * xc, axis=-1, keepdims=True)
    y = xc * lax.rsqrt(var + EPS) * lg_ref[...] + lb_ref[...]
    conv_ref[0] = (y * jax.nn.sigmoid(y)).astype(BF16)

    pe_ref[0:HALO] = jnp.where(first, 0.0, zpp_ref[0])
    pe_ref[HALO:HALO + tm] = zp_ref[0]
    pe_ref[HALO + tm:] = jnp.where(last, 0.0, zpn_ref[0])

    def tap(dlt):
        return pe_ref[pl.ds(HALO + dlt, tm), :]

    z = tap(0)
    sums = {}
    run = z + tap(1)
    sums[2] = run
    run = run + tap(-1) + tap(2)
    sums[4] = run
    run = run + tap(-3) + tap(-2) + tap(3) + tap(4)
    sums[8] = run
    for dlt in (-7, -6, -5, -4, 5, 6, 7, 8):
        run = run + tap(dlt)
    sums[16] = run
    pos = j * tm + lax.broadcasted_iota(jnp.int32, (tm, 1), 0)
    lane = lax.broadcasted_iota(jnp.int32, (tm, POOL_WIDTH), 1)
    mean = None
    for gi, w in enumerate(POOL_WINDOWS):
        lo = jnp.maximum(pos - (w - 1) // 2, 0)
        hi = jnp.minimum(pos + w // 2, seq_len - 1)
        m = sums[w] / (hi - lo + 1).astype(F32)
        mean = m if mean is None else jnp.where(lane >= gi * POOL_GROUP, m, mean)
    d = (mean - z).astype(BF16)
    pool_ref[0] = (_dot(d, pw_ref[...]) * ps_ref[...]).astype(BF16)


def _local_mixers(zc, zp, lw, tm):
    b, s, _ = zc.shape
    nh = tm // HALO
    nblk = s // HALO
    main = lambda w: pl.BlockSpec((1, tm, w), lambda i, j: (i, j, 0))
    prev = lambda w: pl.BlockSpec((1, HALO, w), lambda i, j: (i, jnp.maximum(j * nh - 1, 0), 0))
    nxt = lambda w: pl.BlockSpec((1, HALO, w), lambda i, j: (i, jnp.minimum((j + 1) * nh, nblk - 1), 0))
    cw2, pw = 2 * CONV_WIDTH, POOL_WIDTH
    return pl.pallas_call(
        functools.partial(_local_kernel, seq_len=s),
        grid=(b, s // tm),
        in_specs=[main(cw2), prev(cw2), nxt(cw2), main(pw), prev(pw), nxt(pw),
                  _const_spec(lw["conv_m"].shape), _const_spec((1, CONV_WIDTH)), _const_spec((1, CONV_WIDTH)),
                  _const_spec((1, CONV_WIDTH)), _const_spec((pw, pw)), _const_spec((1, pw))],
        out_specs=[main(CONV_WIDTH), main(pw)],
        out_shape=[jax.ShapeDtypeStruct((b, s, CONV_WIDTH), BF16), jax.ShapeDtypeStruct((b, s, pw), BF16)],
        scratch_shapes=[pltpu.VMEM((tm + 2 * HALO, LANES), F32), pltpu.VMEM((tm + 2 * HALO, LANES), F32),
                        pltpu.VMEM((tm, LANES), F32), pltpu.VMEM((tm, LANES), F32),
                        pltpu.VMEM((tm + 2 * HALO, pw), F32)],
        compiler_params=_cparams("arbitrary", "arbitrary"),
        name="conv_pool",
    )(zc, zc, zc, zp, zp, zp, lw["conv_m"], lw["conv_b"], lw["conv_ln_g"], lw["conv_ln_b"], lw["pool_w"],
      lw["pool_scale"])


def _attn_kernel(*refs, tk, has_lat):
    if has_lat:
        q_ref, kc_ref, vtc_ref, kl_ref, vtl_ref, o_ref, s_ref = refs
    else:
        q_ref, kc_ref, vtc_ref, o_ref = refs
    tq = q_ref.shape[1]
    heads = range(ATTN_HEADS)
    qs = [q_ref[0, :, h * HEAD_PAD:(h + 1) * HEAD_PAD] for h in heads]

    def scores(h, k):
        return lax.dot_general(k, qs[h], (((1,), (1,)), ((), ())), preferred_element_type=F32)

    def consume(s, mt, vt, state):
        m, l, acc = state
        m_new = jnp.maximum(m, mt)
        p = jnp.exp2(s - m_new)
        alpha = jnp.exp2(m - m_new)
        l = alpha * l + jnp.sum(p, axis=0, keepdims=True)
        acc = alpha * acc + _dot(vt, p.astype(BF16))
        return m_new, l, acc

    init = (jnp.full((1, tq), -1e30, F32), jnp.zeros((1, tq), F32), jnp.zeros((MLA_V, tq), F32))
    state = []
    for h in heads:
        s = scores(h, kc_ref[0, :, h * HEAD_PAD:(h + 1) * HEAD_PAD])
        state.append(consume(s, jnp.max(s, axis=0, keepdims=True), vtc_ref[0, h * MLA_V:(h + 1) * MLA_V, :], init))

    if has_lat:
        n_tiles = kl_ref.shape[1] // tk

        def produce(h, slot, i):
            off = pl.multiple_of(i * tk, tk)
            s = scores(h, kl_ref[0, pl.ds(off, tk), h * HEAD_PAD:(h + 1) * HEAD_PAD])
            s_ref[h, slot] = s
            return jnp.max(s, axis=0, keepdims=True)

        def use(h, slot, i, mt, st):
            off = pl.multiple_of(i * tk, tk)
            return consume(s_ref[h, slot], mt, vtl_ref[0, h * MLA_V:(h + 1) * MLA_V, pl.ds(off, tk)], st)

        def pair(i0, st, mt0, more):
            mt1 = [produce(h, 1, i0 + 1) for h in heads]
            st = [use(h, 0, i0, mt0[h], st[h]) for h in heads]
            if more:
                mt0 = [produce(h, 0, i0 + 2) for h in heads]
            st = [use(h, 1, i0 + 1, mt1[h], st[h]) for h in heads]
            return st, mt0

        mt0 = [produce(h, 0, 0) for h in heads]
        state, mt0 = lax.fori_loop(0, n_tiles // 2 - 1, lambda j, c: pair(2 * j, c[0], c[1], True), (state, mt0))
        state, _ = pair(n_tiles - 2, state, mt0, False)
    out_t = jnp.concatenate([acc / l for _, l, acc in state], axis=0)
    o_ref[0] = out_t.T.astype(BF16)


def _attention(q, k_ctx, vt_ctx, k_lat, vt_lat, tq, tk):
    b, lq, _ = q.shape
    lc = k_ctx.shape[1]
    has_lat = k_lat is not None
    nh = ATTN_HEADS
    in_specs = [pl.BlockSpec((1, tq, nh * HEAD_PAD), lambda i, p, j: (i, j, p)),
                pl.BlockSpec((1, lc, nh * HEAD_PAD), lambda i, p, j: (i, 0, p)),
                pl.BlockSpec((1, nh * MLA_V, lc), lambda i, p, j: (i, p, 0))]
    args = [q, k_ctx, vt_ctx]
    scratch = []
    if has_lat:
        ll = k_lat.shape[1]
        assert ll % (2 * tk) == 0
        in_specs += [pl.BlockSpec((1, ll, nh * HEAD_PAD), lambda i, p, j: (i, 0, p)),
                     pl.BlockSpec((1, nh * MLA_V, ll), lambda i, p, j: (i, p, 0))]
        args += [k_lat, vt_lat]
        scratch = [pltpu.VMEM((nh, 2, tk, tq), F32)]
    return pl.pallas_call(
        functools.partial(_attn_kernel, tk=tk, has_lat=has_lat),
        grid=(b, MLA_HEADS // nh, lq // tq),
        in_specs=in_specs,
        out_specs=pl.BlockSpec((1, tq, nh * MLA_V), lambda i, p, j: (i, j, p)),
        out_shape=jax.ShapeDtypeStruct((b, lq, MLA_WIDTH), BF16),
        scratch_shapes=scratch,
        compiler_params=_cparams("arbitrary", "arbitrary", "arbitrary"),
        name="mla_attention",
    )(*args)


def _mixout_kernel(x_ref, mod_ref, post_ref, y5a_ref, y5b_ref, u_ref, d_ref, wglu_ref, bglu_ref, conv_ref, pool_ref,
                   att_ref, wout_ref, o_ref):
    x = x_ref[0]
    _, _, gate = _mod_slices(mod_ref, 3)
    y5 = jax.nn.gelu(jnp.concatenate([y5a_ref[0], y5b_ref[0]], axis=1) + d_ref[...] * u_ref[0])
    s5 = y5 * jax.nn.sigmoid(_dot(y5.astype(BF16), wglu_ref[...]) + bglu_ref[...])
    o1, o2, o3 = S5_WIDTH, S5_WIDTH + CONV_WIDTH, S5_WIDTH + CONV_WIDTH + POOL_WIDTH
    y = (_dot(s5.astype(BF16), wout_ref[0:o1, :]) + _dot(conv_ref[0], wout_ref[o1:o2, :])
         + _dot(pool_ref[0], wout_ref[o2:o3, :]) + _dot(att_ref[0], wout_ref[o3:, :]))
    o_ref[0] = x + gate * _rms(y, post_ref[...])


def _mix_out(x, mod_rows, post_g, y5, u, conv, pool, att, lw, tm):
    b, s, d = x.shape
    per_batch = mod_rows.shape[0] > 1
    row = lambda w: pl.BlockSpec((1, tm, w), lambda i, j: (i, j, 0))
    return pl.pallas_call(
        _mixout_kernel,
        grid=(b, s // tm),
        in_specs=[
            row(d),
            pl.BlockSpec((1, 1, N_MOD * d), (lambda i, j: (i, 0, 0)) if per_batch else (lambda i, j: (0, 0, 0))),
            _const_spec((1, d)),
            row(LANES), row(LANES), row(S5_WIDTH), _const_spec((1, S5_WIDTH)), _const_spec((S5_WIDTH, S5_WIDTH)),
            _const_spec((1, S5_WIDTH)),
            row(CONV_WIDTH), row(POOL_WIDTH), row(MLA_WIDTH), _const_spec((D_MIX, d)),
        ],
        out_specs=row(d),
        out_shape=jax.ShapeDtypeStruct(x.shape, F32),
        compiler_params=_cparams("arbitrary", "arbitrary"),
        name="mixer_out_proj",
    )(x, mod_rows, post_g.reshape(1, d), y5[0], y5[1], u, lw["s5_d"], lw["s5_w_glu"], lw["s5_b_glu"], conv, pool,
      att, lw["w_out"])


def _rope_partner(w):
    q = MLA_ROPE // 4
    return jnp.concatenate([-w[..., q:2 * q], w[..., 0:q], -w[..., 3 * q:4 * q], w[..., 2 * q:3 * q]], axis=-1)


def _rope_tables(n_lat):
    rows = n_lat // GRID_W
    axis_dim = MLA_ROPE // 2
    inv = ROPE_BASE ** (-jnp.arange(0, axis_dim, 2, dtype=F32) / axis_dim)
    ang_r = jnp.arange(rows, dtype=F32)[:, None] * inv
    ang_c = jnp.arange(GRID_W, dtype=F32)[:, None] * inv
    nq = MLA_ROPE // 4

    def table(fn, fill):
        fr, fc = fn(ang_r), fn(ang_c)
        pad = HEAD_PAD - MLA_NOPE - MLA_ROPE
        by_row = jnp.concatenate([jnp.full((rows, MLA_NOPE), fill, F32), fr, fr, jnp.zeros((rows, 2 * nq), F32),
                                  jnp.full((rows, pad), fill, F32)], axis=1)
        by_col = jnp.concatenate([jnp.zeros((GRID_W, MLA_NOPE + 2 * nq), F32), fc, fc,
                                  jnp.zeros((GRID_W, pad), F32)], axis=1)
        return (by_row[:, None, :] + by_col[None, :, :]).reshape(n_lat, HEAD_PAD)

    return table(jnp.cos, 1.0), table(jnp.sin, 0.0)


def _pad_head(nope, rope):
    r = nope.shape[0]
    z = jnp.zeros((r, MLA_HEADS, HEAD_PAD - MLA_NOPE - MLA_ROPE), nope.dtype)
    return jnp.concatenate([nope, rope, z], axis=-1).reshape(r, MLA_HEADS * HEAD_PAD)


def _band_matrices(w, center):
    n, h = S5_CHUNK, S5_GROUP
    n_taps, width = w.shape
    g = width // h
    c = jnp.arange(3)[:, None, None]
    t_in = jnp.arange(n)[None, :, None]
    t_out = jnp.arange(n)[None, None, :]
    k = t_in + n * (c - 1) - t_out + center
    valid = ((k >= 0) & (k < n_taps))[..., None]
    taps = jnp.where(valid, w[jnp.clip(k, 0, n_taps - 1)], 0.0)
    taps = taps.reshape(3, n, n, g, h).transpose(3, 0, 1, 4, 2).astype(BF16)
    eye = jnp.eye(h, dtype=BF16)[None, None, None, :, None, :]
    return (taps[..., None] * eye).reshape(g * 3, n * h, n * h)


def _prep_layer(w_in, w_out, s5_d, s5_w_glu, s5_b_glu, conv_w, conv_b, conv_ln_g, conv_ln_b, pool_w, pool_scale,
                mla_q_norm, mla_w_uq, mla_kv_norm, mla_w_ukv):
    d = D_MODEL
    wi = w_in
    kr = wi[:, IN_KR:D_IN]
    zpad_l = jnp.zeros((d, MLA_NOPE), F32)
    zpad_r = jnp.zeros((d, HEAD_PAD - MLA_NOPE - MLA_ROPE), F32)
    w_in_ext = jnp.concatenate([wi[:, :IN_KR], zpad_l, kr, zpad_r, zpad_l, _rope_partner(kr), zpad_r], axis=1)
    uq = mla_w_uq.reshape(MLA_Q_RANK, MLA_HEADS, MLA_NOPE + MLA_ROPE)
    q_nope, q_rope = uq[..., :MLA_NOPE], uq[..., MLA_NOPE:]
    wq = jnp.concatenate([_pad_head(q_nope, q_rope), _pad_head(0 * q_nope, _rope_partner(q_rope))], axis=1)
    ukv = mla_w_ukv.reshape(MLA_KV_RANK, MLA_HEADS, MLA_NOPE + MLA_V)
    wk = _pad_head(ukv[..., :MLA_NOPE], jnp.zeros((MLA_KV_RANK, MLA_HEADS, MLA_ROPE), F32))
    wv = ukv[..., MLA_NOPE:].reshape(MLA_KV_RANK, MLA_WIDTH)
    eye = jnp.eye(HEAD_PAD, dtype=F32) * ((jnp.arange(HEAD_PAD) >= MLA_NOPE)
                                          & (jnp.arange(HEAD_PAD) < MLA_NOPE + MLA_ROPE))[:, None]
    place = jnp.tile(eye, (1, MLA_HEADS))
    pool_bd = jnp.zeros((POOL_WIDTH, POOL_WIDTH), F32)
    for gi in range(len(POOL_WINDOWS)):
        sl = slice(gi * POOL_GROUP, (gi + 1) * POOL_GROUP)
        pool_bd = pool_bd.at[sl, sl].set(pool_w[gi])
    return {
        "w_in": w_in_ext.astype(BF16), "w_out": w_out.astype(BF16),
        "q_norm": mla_q_norm.reshape(1, -1), "kv_norm": mla_kv_norm.reshape(1, -1),
        "wq": wq.astype(BF16), "wk": wk.astype(BF16), "wv": wv.astype(BF16), "place": place.astype(BF16),
        "s5_d": s5_d.reshape(1, -1), "s5_w_glu": s5_w_glu.astype(BF16), "s5_b_glu": s5_b_glu.reshape(1, -1),
        "conv_m": _band_matrices(conv_w, CONV_K // 2),
        "conv_b": conv_b.reshape(1, -1), "conv_ln_g": conv_ln_g.reshape(1, -1),
        "conv_ln_b": conv_ln_b.reshape(1, -1),
        "pool_w": pool_bd.astype(BF16), "pool_scale": pool_scale.reshape(1, -1),
    }


def kernel(x, c, ctx, c_ctx, w_ada, b_ada, norm_pre, norm_post, ffn_w_gate, ffn_w_up, ffn_w_down, w_in, w_out,
           s5_lam_re, s5_lam_im, s5_log_dt, s5_b_re, s5_b_im, s5_c_re, s5_c_im, s5_d, s5_w_glu, s5_b_glu, conv_w,
           conv_b, conv_ln_g, conv_ln_b, pool_w, pool_scale, mla_q_norm, mla_w_uq, mla_kv_norm, mla_w_ukv):
    batch, n_lat, d = x.shape
    n_ctx = ctx.shape[1]
    depth = w_ada.shape[0]
    assert d == D_MODEL and batch + 1 <= 8
    tm_lat = min(512, n_lat)
    tm_ffn = tm_lat
    tm_ctx = min(512, n_ctx)
    tq_lat = min(1024, n_lat)
    tk = min(1024, n_lat // 2)
    assert n_lat % tm_lat == 0 and n_ctx % S5_CHUNK == 0 and n_lat % GRID_W == 0

    c_rows = jnp.zeros((8, d), F32).at[:batch].set(c).at[batch].set(c_ctx)
    mod = _modulation(c_rows, w_ada, b_ada)
    cos_lat, sin_lat = _rope_tables(n_lat)
    cos_ctx = jnp.ones((n_ctx, HEAD_PAD), F32)
    sin_ctx = jnp.zeros((n_ctx, HEAD_PAD), F32)
    wg = ffn_w_gate.astype(BF16)
    wu = ffn_w_up.astype(BF16)
    wd = ffn_w_down.astype(BF16)
    layer_params = jax.vmap(_prep_layer)(w_in, w_out, s5_d, s5_w_glu, s5_b_glu, conv_w, conv_b, conv_ln_g, conv_ln_b,
                                         pool_w, pool_scale, mla_q_norm, mla_w_uq, mla_kv_norm, mla_w_ukv)
    s5_params = jax.vmap(_s5_matrices)(s5_lam_re, s5_lam_im, s5_log_dt, s5_b_re, s5_b_im, s5_c_re, s5_c_im)

    x_lat, x_ctx = x, ctx
    for l in range(depth):
        last = l == depth - 1
        mod_lat = mod[l, :batch].reshape(batch, 1, N_MOD * d)
        mod_ctx = mod[l, batch].reshape(1, 1, N_MOD * d)
        lw = {name: v[l] for name, v in layer_params.items()}
        s5_mats = tuple(m[l] for m in s5_params)
        x_lat = _ffn(x_lat, mod_lat, 0, norm_pre[l, 0], norm_post[l, 0], wg[l, 0], wu[l, 0], wd[l, 0], tm_ffn)
        x_ctx = _ffn(x_ctx, mod_ctx, 0, norm_pre[l, 0], norm_post[l, 0], wg[l, 0], wu[l, 0], wd[l, 0], tm_ctx)
        u_l, zc_l, zp_l, q_l, k_l, v_l = _inproj(x_lat, mod_lat, norm_pre[l, 1], lw, cos_lat, sin_lat, tm_lat)
        u_c, zc_c, zp_c, q_c, k_c, v_c = _inproj(x_ctx, mod_ctx, norm_pre[l, 1], lw, cos_ctx, sin_ctx, tm_ctx)
        y5_c, y5_l = _s5_mixer(u_c, u_l, s5_mats)
        conv_l, pool_l = _local_mixers(zc_l, zp_l, lw, min(2048, n_lat))
        vt_l, vt_c = v_l.transpose(0, 2, 1), v_c.transpose(0, 2, 1)
        att_l = _attention(q_l, k_c, vt_c, k_l, vt_l, tq_lat, tk)
        x_lat = _mix_out(x_lat, mod_lat, norm_post[l, 1], y5_l, u_l, conv_l, pool_l, att_l, lw, tm_lat)
        if not last:
            conv_c, pool_c = _local_mixers(zc_c, zp_c, lw, tm_ctx)
            att_c = _attention(q_c, k_c, vt_c, None, None, tm_ctx, tk)
            x_ctx = _mix_out(x_ctx, mod_ctx, norm_post[l, 1], y5_c, u_c, conv_c, pool_c, att_c, lw, tm_ctx)
        x_lat = _ffn(x_lat, mod_lat, 6, norm_pre[l, 2], norm_post[l, 2], wg[l, 1], wu[l, 1], wd[l, 1], tm_ffn)
        if not last:
            x_ctx = _ffn(x_ctx, mod_ctx, 6, norm_pre[l, 2], norm_post[l, 2], wg[l, 1], wu[l, 1], wd[l, 1], tm_ctx)
    return x_lat
```

```python
import functools
import math
from typing import NamedTuple

import jax
import jax.numpy as jnp
from jax import lax
from jax.experimental import pallas as pl
from jax.experimental.pallas import tpu as pltpu

F32 = jnp.float32
BF16 = jnp.bfloat16

D_MODEL = 1024
GRID_W = 64
EPS = 1e-6
N_MOD = 9
MACARON_WEIGHT = 0.5
D_FF = 2816
S5_WIDTH = 256
S5_GROUP = 16
S5_GROUPS = 16
S5_STATE = 64
CONV_WIDTH = 256
CONV_K = 31
POOL_WIDTH = 256
POOL_WINDOWS = (2, 4, 8, 16)
POOL_GROUP = 64
MLA_HEADS = 8
MLA_NOPE = 64
MLA_ROPE = 32
MLA_V = 64
MLA_Q_RANK = 256
MLA_KV_RANK = 128
MLA_WIDTH = MLA_HEADS * MLA_V
ROPE_BASE = 10000.0
D_MIX = S5_WIDTH + CONV_WIDTH + POOL_WIDTH + MLA_WIDTH
IN_S5 = 0
IN_CONV = IN_S5 + S5_WIDTH
IN_POOL = IN_CONV + 2 * CONV_WIDTH
IN_CQ = IN_POOL + POOL_WIDTH
IN_CKV = IN_CQ + MLA_Q_RANK
IN_KR = IN_CKV + MLA_KV_RANK
D_IN = IN_KR + MLA_ROPE

LANES = 128
HEAD_PAD = 128
S5_CHUNK = 16
S5_FLAT = S5_CHUNK * S5_GROUP
HALO = 16
D_IN_EXT = IN_KR + 2 * HEAD_PAD
FF_CHUNKS = ((0, 1024), (1024, 2048), (2048, 2816))
VMEM_LIMIT = 56 * 1024 * 1024
ATTN_HEADS = 2
Q_SCALE = (MLA_NOPE + MLA_ROPE) ** -0.5 * math.log2(math.e)


def _cparams(*sem, flags=None):
    return pltpu.CompilerParams(dimension_semantics=sem, vmem_limit_bytes=VMEM_LIMIT, flags=flags)


class _Param(NamedTuple):
    stacked: jax.Array
    idx: tuple


def _const_spec(p):
    if isinstance(p, _Param):
        n = len(p.idx)
        idx = tuple(p.idx) + (0,) * (p.stacked.ndim - n)
        return pl.BlockSpec((None,) * n + p.stacked.shape[n:], lambda *_: idx, pipeline_mode=pl.Buffered(1))
    nd = len(p)
    return pl.BlockSpec(p, lambda *_: (0,) * nd, pipeline_mode=pl.Buffered(1))


def _rms(x, g):
    return x * lax.rsqrt(jnp.mean(x * x, axis=-1, keepdims=True) + EPS) * g


def _dot(a, b):
    return jnp.dot(a, b, preferred_element_type=F32)


def _mod_kernel(c_ref, w_ref, b_ref, o_ref):
    c = c_ref[...]
    h = (c * jax.nn.sigmoid(c)).astype(BF16)
    o_ref[0] = _dot(h, w_ref[0].astype(BF16)) + b_ref[0]


def _modulation(c_rows, w_ada, b_ada):
    depth = w_ada.shape[0]
    d = D_MODEL
    return pl.pallas_call(
        _mod_kernel,
        grid=(depth, N_MOD),
        in_specs=[
            pl.BlockSpec((8, d), lambda l, j: (0, 0)),
            pl.BlockSpec((1, d, d), lambda l, j: (l, 0, j)),
            pl.BlockSpec((1, 1, d), lambda l, j: (l, 0, j)),
        ],
        out_specs=pl.BlockSpec((1, 8, d), lambda l, j: (l, 0, j)),
        out_shape=jax.ShapeDtypeStruct((depth, 8, N_MOD * d), F32),
        compiler_params=_cparams("arbitrary", "arbitrary"),
        name="adaln_mod",
    )(c_rows, w_ada, b_ada.reshape(depth, 1, N_MOD * d))


def _mod_slices(mod_ref, base):
    d = D_MODEL
    return (mod_ref[0, :, base * d:(base + 1) * d], mod_ref[0, :, (base + 1) * d:(base + 2) * d],
            mod_ref[0, :, (base + 2) * d:(base + 3) * d])


def _ffn_kernel(x_ref, mod_ref, pre_ref, post_ref, wg_ref, wu_ref, wd_ref, o_ref, *, base):
    x = x_ref[0]
    shift, scale, gate = _mod_slices(mod_ref, base)
    h = (_rms(x, pre_ref[...]) * (1.0 + scale) + shift).astype(BF16)
    acc = None
    for lo, hi in FF_CHUNKS:
        g = _dot(h, wg_ref[:, lo:hi])
        u = _dot(h, wu_ref[:, lo:hi])
        a = (g * jax.nn.sigmoid(g) * u).astype(BF16)
        part = _dot(a, wd_ref[lo:hi, :])
        acc = part if acc is None else acc + part
    o_ref[0] = x + MACARON_WEIGHT * gate * _rms(acc, post_ref[...])


def _ffn(x, mod_rows, base, pre_g, post_g, wg, wu, wd, tm):
    b, s, d = x.shape
    per_batch = mod_rows.shape[0] > 1
    return pl.pallas_call(
        functools.partial(_ffn_kernel, base=base),
        grid=(b, s // tm),
        in_specs=[
            pl.BlockSpec((1, tm, d), lambda i, j: (i, j, 0)),
            pl.BlockSpec((1, 1, N_MOD * d), (lambda i, j: (i, 0, 0)) if per_batch else (lambda i, j: (0, 0, 0))),
            _const_spec((1, d)), _const_spec((1, d)),
            _const_spec(wg), _const_spec(wu), _const_spec(wd),
        ],
        out_specs=pl.BlockSpec((1, tm, d), lambda i, j: (i, j, 0)),
        out_shape=jax.ShapeDtypeStruct(x.shape, F32),
        compiler_params=_cparams("arbitrary", "arbitrary"),
        name="ffn_half_step",
    )(x, mod_rows, pre_g.reshape(1, d), post_g.reshape(1, d), wg.stacked, wu.stacked, wd.stacked)


def _inproj_kernel(x_ref, mod_ref, pre_ref, win_ref, qn_ref, kvn_ref, wq_ref, wk_ref, wv_ref, place_ref,
                   cos_ref, sin_ref, u_ref, zc_ref, zp_ref, q_ref, k_ref, v_ref):
    x = x_ref[0]
    shift, scale, _ = _mod_slices(mod_ref, 3)
    h = (_rms(x, pre_ref[...]) * (1.0 + scale) + shift).astype(BF16)
    z = _dot(h, win_ref[...])
    u_ref[0] = z[:, IN_S5:IN_CONV]
    zc_ref[0] = z[:, IN_CONV:IN_POOL]
    zp_ref[0] = z[:, IN_POOL:IN_CQ]
    cos = cos_ref[...]
    sin = sin_ref[...]
    cos_h = jnp.concatenate([cos] * MLA_HEADS, axis=1)
    sin_h = jnp.concatenate([sin] * MLA_HEADS, axis=1)
    cqn = _rms(z[:, IN_CQ:IN_CKV], qn_ref[...]).astype(BF16)
    qq = _dot(cqn, wq_ref[...])
    hw = MLA_HEADS * HEAD_PAD
    q = qq[:, :hw] * cos_h + qq[:, hw:] * sin_h
    q_ref[0] = (q * Q_SCALE).astype(BF16)
    ckvn = _rms(z[:, IN_CKV:IN_KR], kvn_ref[...]).astype(BF16)
    kr = z[:, IN_KR:IN_KR + HEAD_PAD] * cos + z[:, IN_KR + HEAD_PAD:IN_KR + 2 * HEAD_PAD] * sin
    k = _dot(ckvn, wk_ref[...]) + _dot(kr.astype(BF16), place_ref[...])
    k_ref[0] = k.astype(BF16)
    v_ref[0] = _dot(ckvn, wv_ref[...]).astype(BF16)


def _inproj(x, mod_rows, pre_g, lw, cos_t, sin_t, tm):
    b, s, d = x.shape
    per_batch = mod_rows.shape[0] > 1
    hw = MLA_HEADS * HEAD_PAD
    row = lambda w: pl.BlockSpec((1, tm, w), lambda i, j: (i, j, 0))
    params = [lw[name] for name in ("w_in", "q_norm", "kv_norm", "wq", "wk", "wv", "place")]
    return pl.pallas_call(
        _inproj_kernel,
        grid=(b, s // tm),
        in_specs=[
            row(d),
            pl.BlockSpec((1, 1, N_MOD * d), (lambda i, j: (i, 0, 0)) if per_batch else (lambda i, j: (0, 0, 0))),
            _const_spec((1, d)),
            *[_const_spec(p) for p in params],
            pl.BlockSpec((tm, HEAD_PAD), lambda i, j: (j, 0)),
            pl.BlockSpec((tm, HEAD_PAD), lambda i, j: (j, 0)),
        ],
        out_specs=[row(S5_WIDTH), row(2 * CONV_WIDTH), row(POOL_WIDTH), row(hw), row(hw), row(MLA_WIDTH)],
        out_shape=[
            jax.ShapeDtypeStruct((b, s, S5_WIDTH), F32),
            jax.ShapeDtypeStruct((b, s, 2 * CONV_WIDTH), F32),
            jax.ShapeDtypeStruct((b, s, POOL_WIDTH), F32),
            jax.ShapeDtypeStruct((b, s, hw), BF16),
            jax.ShapeDtypeStruct((b, s, hw), BF16),
            jax.ShapeDtypeStruct((b, s, MLA_WIDTH), BF16),
        ],
        compiler_params=_cparams("arbitrary", "arbitrary"),
        name="mixer_in_proj",
    )(x, mod_rows, pre_g.reshape(1, d), *[p.stacked for p in params], cos_t, sin_t)


def _block_transpose(lo, hi):
    n = S5_CHUNK
    half = n // 2
    halves = [[lo[i] for i in range(half)] + [hi[i] for i in range(half)],
              [lo[i + half] for i in range(half)] + [hi[i + half] for i in range(half)]]
    blk = lax.broadcasted_iota(jnp.int32, lo[0].shape, 1) // S5_GROUP
    for d in (4, 2, 1):
        keep = (blk & d) == 0
        for hv in halves:
            for i in range(n):
                if i & d:
                    continue
                x, y = hv[i], hv[i + d]
                hv[i] = jnp.where(keep, x, pltpu.roll(y, d * S5_GROUP, 1))
                hv[i + d] = jnp.where(keep, pltpu.roll(x, LANES - d * S5_GROUP, 1), y)
    return halves


def _strided_rows(ref, phase, period):
    return ref[0, pl.ds(phase, ref.shape[1] // period, stride=period), :]


def _chunk_flat(ua_ref, ub_ref):
    lo, hi = _block_transpose([_strided_rows(ua_ref, t, S5_CHUNK) for t in range(S5_CHUNK)],
                              [_strided_rows(ub_ref, t, S5_CHUNK) for t in range(S5_CHUNK)])
    return [jnp.concatenate([lo[g], hi[g]], axis=1).astype(BF16) for g in range(S5_GROUPS)]


def _s5_state_kernel(ua_ref, ub_ref, m_ref, sf_ref, sb_ref):
    r = ua_ref.shape[1] // S5_CHUNK
    for g, f in enumerate(_chunk_flat(ua_ref, ub_ref)):
        s = _dot(f, m_ref[g])
        sf_ref[0, pl.ds(g, r, stride=S5_GROUPS), :] = s[:, :LANES]
        sb_ref[0, pl.ds(g, r, stride=S5_GROUPS), :] = s[:, LANES:]


def _lane_half(tm, lane_block):
    return pl.BlockSpec((1, tm, LANES), lambda i, j: (i, j, lane_block))


def _s5_chunk_states(u, m_state, tm):
    b, s, _ = u.shape
    out = jax.ShapeDtypeStruct((b, s, LANES), F32)
    return pl.pallas_call(
        _s5_state_kernel,
        grid=(b, s // tm),
        in_specs=[_lane_half(tm, 0), _lane_half(tm, 1), _const_spec(m_state)],
        out_specs=[_lane_half(tm, 0), _lane_half(tm, 0)],
        out_shape=[out, out],
        compiler_params=_cparams("arbitrary", "arbitrary"),
        name="s5_chunk_states",
    )(u, u, m_state.stacked)


def _s5_scan_kernel(scf_ref, scb_ref, sf_ref, sb_ref, a_ref, hcf_ref, hcb_ref, hf_ref, hb_ref, st_ref, wf_ref, wb_ref,
                    hrf_ref, hif_ref, hrb_ref, hib_ref):
    half = LANES // 2
    a_re = a_ref[:, :LANES]
    a_im = a_ref[:, LANES:]
    fwd_lane = lax.broadcasted_iota(jnp.int32, a_re.shape, 1) < half

    def swap(v):
        n, g, w = v.shape
        return pltpu.roll(v.reshape(n * g, w), half, 1).reshape(n, g, w)

    def run(in_f, in_b, out_f, out_b, carry):
        n = in_f.shape[1]
        wf_ref[0:n] = swap(in_f[0])
        wb_ref[0:n] = swap(in_b[0])

        def body(i, c):
            h_re, h_im = c
            r = n - 1 - i
            hrf_ref[i] = h_re
            hif_ref[i] = h_im
            hrb_ref[r] = h_re
            hib_ref[r] = h_im
            s_re = jnp.where(fwd_lane, in_f[0, i], wb_ref[r])
            s_im = jnp.where(fwd_lane, wf_ref[i], in_b[0, r])
            return (a_re * h_re - a_im * h_im + s_re, a_re * h_im + a_im * h_re + s_im)

        carry = lax.fori_loop(0, n, body, carry)
        out_f[0] = jnp.where(fwd_lane, hrf_ref[0:n], swap(hif_ref[0:n]))
        out_b[0] = jnp.where(fwd_lane, swap(hrb_ref[0:n]), hib_ref[0:n])
        return carry

    @pl.when(pl.program_id(1) == 0)
    def _():
        zero = jnp.zeros(a_re.shape, F32)
        c = run(scf_ref, scb_ref, hcf_ref, hcb_ref, (zero, zero))
        st_ref[0] = c[0]
        st_ref[1] = c[1]

    c = run(sf_ref, sb_ref, hf_ref, hb_ref, (st_ref[0], st_ref[1]))
    st_ref[0] = c[0]
    st_ref[1] = c[1]


def _s5_scan(sc_f, sc_b, sl_f, sl_b, a16):
    b, nc, g, _ = sc_f.shape
    c = sl_f.shape[1]
    cb = min(128, c)
    nb = c // cb
    ctx = pl.BlockSpec((1, nc, g, LANES), lambda i, j: (i, 0, 0, 0))
    up = pl.BlockSpec((1, cb, g, LANES), lambda i, j: (i, j, 0, 0))
    down = pl.BlockSpec((1, cb, g, LANES), lambda i, j: (i, nb - 1 - j, 0, 0))
    return pl.pallas_call(
        _s5_scan_kernel,
        grid=(b, nb),
        in_specs=[ctx, ctx, up, down, pl.BlockSpec((g, 2 * LANES), lambda i, j: (0, 0))],
        out_specs=[ctx, ctx, up, down],
        out_shape=[jax.ShapeDtypeStruct(sc_f.shape, F32), jax.ShapeDtypeStruct(sc_f.shape, F32),
                   jax.ShapeDtypeStruct(sl_f.shape, F32), jax.ShapeDtypeStruct(sl_f.shape, F32)],
        scratch_shapes=[pltpu.VMEM((2, g, LANES), F32)] + [pltpu.VMEM((max(cb, nc), g, LANES), F32)] * 6,
        compiler_params=_cparams("arbitrary", "arbitrary"),
        name="s5_chunk_scan",
    )(sc_f, sc_b, sl_f, sl_b, a16)


def _s5_out_kernel(ua_ref, ub_ref, hf_ref, hb_ref, mi_ref, mo_ref, ya_ref, yb_ref):
    lo, hi = [], []
    for g, f in enumerate(_chunk_flat(ua_ref, ub_ref)):
        hf = _strided_rows(hf_ref, g, S5_GROUPS).astype(BF16)
        hb = _strided_rows(hb_ref, g, S5_GROUPS).astype(BF16)
        y = _dot(f, mi_ref[g]) + _dot(hf, mo_ref[g, :LANES, :]) + _dot(hb, mo_ref[g, LANES:, :])
        lo.append(y[:, :LANES])
        hi.append(y[:, LANES:])
    lo, hi = _block_transpose(lo, hi)
    r = ua_ref.shape[1] // S5_CHUNK
    for t in range(S5_CHUNK):
        ya_ref[0, pl.ds(t, r, stride=S5_CHUNK), :] = lo[t]
        yb_ref[0, pl.ds(t, r, stride=S5_CHUNK), :] = hi[t]


def _s5_outputs(u, h_f, h_b, m_intra, m_out, tm):
    b, s, _ = u.shape
    out = jax.ShapeDtypeStruct((b, s, LANES), F32)
    return pl.pallas_call(
        _s5_out_kernel,
        grid=(b, s // tm),
        in_specs=[_lane_half(tm, 0), _lane_half(tm, 1), _lane_half(tm, 0), _lane_half(tm, 0),
                  _const_spec(m_intra), _const_spec(m_out)],
        out_specs=[_lane_half(tm, 0), _lane_half(tm, 0)],
        out_shape=[out, out],
        compiler_params=_cparams("arbitrary", "arbitrary"),
        name="s5_chunk_outputs",
    )(u, u, h_f, h_b, m_intra.stacked, m_out.stacked)


def _s5_matrices(lam_re, lam_im, log_dt, b_re, b_im, c_re, c_im):
    hp = lax.Precision.HIGHEST
    t = S5_CHUNK
    g, p, h = S5_GROUPS, S5_STATE, S5_GROUP
    lam = lax.complex(jnp.minimum(lam_re, -1e-4), lam_im)
    dt = jnp.exp(log_dt)[..., None]
    steps = jnp.arange(t + 1, dtype=F32)[:, None, None, None]
    apow = jnp.exp(lam[None] * dt[None] * steps)
    bb = ((apow[1] - 1.0) / lam)[..., None] * lax.complex(b_re, b_im)
    cc = lax.complex(c_re, c_im)
    taps = jnp.einsum('dghp,jdgp,dgpk->djghk', cc, apow[:t], bb, precision=hp).real
    by_dist = jnp.concatenate([taps[1, :0:-1], (taps[0, 0] + taps[1, 0])[None], taps[0, 1:]], axis=0)
    by_dist = by_dist.transpose(0, 1, 3, 2).astype(BF16)
    dist = jnp.arange(t)[None, :] - jnp.arange(t)[:, None] + (t - 1)
    m_intra = by_dist[dist].transpose(2, 0, 3, 1, 4).reshape(g, t * h, t * h)
    wf = apow[:t][::-1, 0][:, :, :, None] * bb[0][None]
    wb = apow[:t, 1][:, :, :, None] * bb[1][None]
    to_rows = lambda w: w.transpose(1, 0, 3, 2).reshape(g, t * h, p)
    m_state = jnp.concatenate([to_rows(wf.real), to_rows(wf.imag), to_rows(wb.real), to_rows(wb.imag)], axis=-1)
    of = cc[0][None] * apow[1:, 0][:, :, None, :]
    ob = cc[1][None] * apow[1:, 1][::-1][:, :, None, :]
    to_cols = lambda w: w.transpose(1, 3, 0, 2).reshape(g, p, t * h)
    m_out = jnp.concatenate([to_cols(of.real), -to_cols(of.imag), to_cols(ob.real), -to_cols(ob.imag)], axis=1)
    a_t = apow[t]
    a16 = jnp.concatenate([a_t[0].real, a_t[1].real, a_t[0].imag, a_t[1].imag], axis=-1)
    return m_intra, m_state.astype(BF16), m_out.astype(BF16), a16


def _s5_mixer(u_ctx, u_lat, mats):
    m_intra, m_state, m_out, a16 = mats
    b, lc, w = u_ctx.shape
    ll = u_lat.shape[1]
    g, t = S5_GROUPS, S5_CHUNK
    tm_c, tm_l = min(1024, lc), min(1024, ll)
    chunks = lambda x: x.reshape(b, x.shape[1] // g, g, LANES)
    rows = lambda x: x.reshape(b, x.shape[1] * g, LANES)
    sc_f, sc_b = _s5_chunk_states(u_ctx, m_state, tm_c)
    sl_f, sl_b = _s5_chunk_states(u_lat, m_state, tm_l)
    hc_f, hc_b, hl_f, hl_b = _s5_scan(chunks(sc_f), chunks(sc_b), chunks(sl_f), chunks(sl_b), a16)
    y_ctx = _s5_outputs(u_ctx, rows(hc_f), rows(hc_b), m_intra, m_out, tm_c)
    y_lat = _s5_outputs(u_lat, rows(hl_f), rows(hl_b), m_intra, m_out, tm_l)
    return y_ctx, y_lat


def _banded_time_filter(xa_ref, xb_ref, m_ref, ya_ref, yb_ref, split):
    r = ya_ref.shape[0] // S5_CHUNK
    flo, fhi = _block_transpose([xa_ref[pl.ds(t, r + 2, stride=S5_CHUNK), :] for t in range(S5_CHUNK)],
                                [xb_ref[pl.ds(t, r + 2, stride=S5_CHUNK), :] for t in range(S5_CHUNK)])
    out_lo, out_hi = [], []
    for g in range(len(flo)):
        f = jnp.concatenate([flo[g], fhi[g]], axis=1)
        y = None
        for c in range(3):
            fc = f[c:c + r]
            top = fc.astype(BF16)
            parts = [top, (fc - top.astype(F32)).astype(BF16)] if split else [top]
            for p in parts:
                term = _dot(p, m_ref[3 * g + c])
                y = term if y is None else y + term
        out_lo.append(y[:, :LANES])
        out_hi.append(y[:, LANES:])
    out_lo, out_hi = _block_transpose(out_lo, out_hi)
    for t in range(S5_CHUNK):
        ya_ref[pl.ds(t, r, stride=S5_CHUNK), :] = out_lo[t]
        yb_ref[pl.ds(t, r, stride=S5_CHUNK), :] = out_hi[t]
    return jnp.concatenate([ya_ref[...], yb_ref[...]], axis=1)


def _local_kernel(zc_ref, zcp_ref, zcn_ref, zp_ref, zpp_ref, zpn_ref, cm_ref, cb_ref, lg_ref, lb_ref, pw_ref, ps_ref,
                  conv_ref, pool_ref, xa_ref, xb_ref, ya_ref, yb_ref, pe_ref, *, seq_len):
    tm = zc_ref.shape[1]
    j = pl.program_id(1)
    first = j == 0
    last = j == pl.num_programs(1) - 1

    def fill(prev, main, nxt):
        for rows, val in ((slice(0, HALO), jnp.where(first, 0.0, prev)), (slice(HALO, HALO + tm), main),
                          (slice(HALO + tm, HALO + tm + HALO), jnp.where(last, 0.0, nxt))):
            xa_ref[rows] = val[:, :LANES]
            xb_ref[rows] = val[:, LANES:]

    def glu(z):
        return z[:, :CONV_WIDTH] * jax.nn.sigmoid(z[:, CONV_WIDTH:])

    fill(glu(zcp_ref[0]), glu(zc_ref[0]), glu(zcn_ref[0]))
    acc = _banded_time_filter(xa_ref, xb_ref, cm_ref, ya_ref, yb_ref, split=False) + cb_ref[...]
    mu = jnp.mean(acc, axis=-1, keepdims=True)
    xc = acc - mu
    var = jnp.mean(xc * xc, axis=-1, keepdims=True)
    y = xc * lax.rsqrt(var + EPS) * lg_ref[...] + lb_ref[...]
    conv_ref[0] = (y * jax.nn.sigmoid(y)).astype(BF16)

    pe_ref[0:HALO] = jnp.where(first, 0.0, zpp_ref[0])
    pe_ref[HALO:HALO + tm] = zp_ref[0]
    pe_ref[HALO + tm:] = jnp.where(last, 0.0, zpn_ref[0])

    def tap(dlt):
        return pe_ref[pl.ds(HALO + dlt, tm), :]

    z = tap(0)
    sums = {}
    run = z + tap(1)
    sums[2] = run
    run = run + tap(-1) + tap(2)
    sums[4] = run
    run = run + tap(-3) + tap(-2) + tap(3) + tap(4)
    sums[8] = run
    for dlt in (-7, -6, -5, -4, 5, 6, 7, 8):
        run = run + tap(dlt)
    sums[16] = run
    pos = j * tm + lax.broadcasted_iota(jnp.int32, (tm, 1), 0)
    lane = lax.broadcasted_iota(jnp.int32, (tm, POOL_WIDTH), 1)
    mean = None
    for gi, w in enumerate(POOL_WINDOWS):
        lo = jnp.maximum(pos - (w - 1) // 2, 0)
        hi = jnp.minimum(pos + w // 2, seq_len - 1)
        m = sums[w] / (hi - lo + 1).astype(F32)
        mean = m if mean is None else jnp.where(lane >= gi * POOL_GROUP, m, mean)
    d = (mean - z).astype(BF16)
    pool_ref[0] = (_dot(d, pw_ref[...]) * ps_ref[...]).astype(BF16)


def _local_mixers(zc, zp, lw, tm):
    b, s, _ = zc.shape
    nh = tm // HALO
    nblk = s // HALO
    main = lambda w: pl.BlockSpec((1, tm, w), lambda i, j: (i, j, 0))
    prev = lambda w: pl.BlockSpec((1, HALO, w), lambda i, j: (i, jnp.maximum(j * nh - 1, 0), 0))
    nxt = lambda w: pl.BlockSpec((1, HALO, w), lambda i, j: (i, jnp.minimum((j + 1) * nh, nblk - 1), 0))
    cw2, pw = 2 * CONV_WIDTH, POOL_WIDTH
    params = [lw[name] for name in ("conv_m", "conv_b", "conv_ln_g", "conv_ln_b", "pool_w", "pool_scale")]
    return pl.pallas_call(
        functools.partial(_local_kernel, seq_len=s),
        grid=(b, s // tm),
        in_specs=[main(cw2), prev(cw2), nxt(cw2), main(pw), prev(pw), nxt(pw), *[_const_spec(p) for p in params]],
        out_specs=[main(CONV_WIDTH), main(pw)],
        out_shape=[jax.ShapeDtypeStruct((b, s, CONV_WIDTH), BF16), jax.ShapeDtypeStruct((b, s, pw), BF16)],
        scratch_shapes=[pltpu.VMEM((tm + 2 * HALO, LANES), F32), pltpu.VMEM((tm + 2 * HALO, LANES), F32),
                        pltpu.VMEM((tm, LANES), F32), pltpu.VMEM((tm, LANES), F32),
                        pltpu.VMEM((tm + 2 * HALO, pw), F32)],
        compiler_params=_cparams("arbitrary", "arbitrary"),
        name="conv_pool",
    )(zc, zc, zc, zp, zp, zp, *[p.stacked for p in params])


def _attn_kernel(*refs, tk, has_lat):
    if has_lat:
        q_ref, kc_ref, vtc_ref, kl_ref, vtl_ref, o_ref, s_ref = refs
    else:
        q_ref, kc_ref, vtc_ref, o_ref = refs
    tq = q_ref.shape[1]
    heads = range(ATTN_HEADS)
    qs = [q_ref[0, :, h * HEAD_PAD:(h + 1) * HEAD_PAD] for h in heads]

    def scores(h, k):
        return lax.dot_general(k, qs[h], (((1,), (1,)), ((), ())), preferred_element_type=F32)

    def consume(s, mt, vt, state):
        m, l, acc = state
        m_new = jnp.maximum(m, mt)
        p = jnp.exp2(s - m_new)
        alpha = jnp.exp2(m - m_new)
        l = alpha * l + jnp.sum(p, axis=0, keepdims=True)
        acc = alpha * acc + _dot(vt, p.astype(BF16))
        return m_new, l, acc

    init = (jnp.full((1, tq), -1e30, F32), jnp.zeros((1, tq), F32), jnp.zeros((MLA_V, tq), F32))
    state = []
    for h in heads:
        s = scores(h, kc_ref[0, :, h * HEAD_PAD:(h + 1) * HEAD_PAD])
        state.append(consume(s, jnp.max(s, axis=0, keepdims=True), vtc_ref[0, h * MLA_V:(h + 1) * MLA_V, :], init))

    if has_lat:
        n_tiles = kl_ref.shape[1] // tk

        def produce(h, slot, i):
            off = pl.multiple_of(i * tk, tk)
            s = scores(h, kl_ref[0, pl.ds(off, tk), h * HEAD_PAD:(h + 1) * HEAD_PAD])
            s_ref[h, slot] = s
            return jnp.max(s, axis=0, keepdims=True)

        def use(h, slot, i, mt, st):
            off = pl.multiple_of(i * tk, tk)
            return consume(s_ref[h, slot], mt, vtl_ref[0, h * MLA_V:(h + 1) * MLA_V, pl.ds(off, tk)], st)

        def pair(i0, st, mt0, more):
            mt1 = [produce(h, 1, i0 + 1) for h in heads]
            st = [use(h, 0, i0, mt0[h], st[h]) for h in heads]
            if more:
                mt0 = [produce(h, 0, i0 + 2) for h in heads]
            st = [use(h, 1, i0 + 1, mt1[h], st[h]) for h in heads]
            return st, mt0

        mt0 = [produce(h, 0, 0) for h in heads]
        state, mt0 = lax.fori_loop(0, n_tiles // 2 - 1, lambda j, c: pair(2 * j, c[0], c[1], True), (state, mt0))
        state, _ = pair(n_tiles - 2, state, mt0, False)
    out_t = jnp.concatenate([acc / l for _, l, acc in state], axis=0)
    o_ref[0] = out_t.T.astype(BF16)


def _attention(q, k_ctx, vt_ctx, k_lat, vt_lat, tq, tk):
    b, lq, _ = q.shape
    lc = k_ctx.shape[1]
    has_lat = k_lat is not None
    nh = ATTN_HEADS
    in_specs = [pl.BlockSpec((1, tq, nh * HEAD_PAD), lambda i, p, j: (i, j, p)),
                pl.BlockSpec((1, lc, nh * HEAD_PAD), lambda i, p, j: (i, 0, p)),
                pl.BlockSpec((1, nh * MLA_V, lc), lambda i, p, j: (i, p, 0))]
    args = [q, k_ctx, vt_ctx]
    scratch = []
    if has_lat:
        ll = k_lat.shape[1]
        assert ll % (2 * tk) == 0
        in_specs += [pl.BlockSpec((1, ll, nh * HEAD_PAD), lambda i, p, j: (i, 0, p)),
                     pl.BlockSpec((1, nh * MLA_V, ll), lambda i, p, j: (i, p, 0))]
        args += [k_lat, vt_lat]
        scratch = [pltpu.VMEM((nh, 2, tk, tq), F32)]
    return pl.pallas_call(
        functools.partial(_attn_kernel, tk=tk, has_lat=has_lat),
        grid=(b, MLA_HEADS // nh, lq // tq),
        in_specs=in_specs,
        out_specs=pl.BlockSpec((1, tq, nh * MLA_V), lambda i, p, j: (i, j, p)),
        out_shape=jax.ShapeDtypeStruct((b, lq, MLA_WIDTH), BF16),
        scratch_shapes=scratch,
        compiler_params=_cparams("arbitrary", "arbitrary", "arbitrary"),
        name="mla_attention",
    )(*args)


def _mixout_kernel(x_ref, mod_ref, post_ref, y5a_ref, y5b_ref, u_ref, d_ref, wglu_ref, bglu_ref, conv_ref, pool_ref,
                   att_ref, wout_ref, o_ref):
    x = x_ref[0]
    _, _, gate = _mod_slices(mod_ref, 3)
    y5 = jax.nn.gelu(jnp.concatenate([y5a_ref[0], y5b_ref[0]], axis=1) + d_ref[...] * u_ref[0])
    s5 = y5 * jax.nn.sigmoid(_dot(y5.astype(BF16), wglu_ref[...]) + bglu_ref[...])
    o1, o2, o3 = S5_WIDTH, S5_WIDTH + CONV_WIDTH, S5_WIDTH + CONV_WIDTH + POOL_WIDTH
    y = (_dot(s5.astype(BF16), wout_ref[0:o1, :]) + _dot(conv_ref[0], wout_ref[o1:o2, :])
         + _dot(pool_ref[0], wout_ref[o2:o3, :]) + _dot(att_ref[0], wout_ref[o3:, :]))
    o_ref[0] = x + gate * _rms(y, post_ref[...])


def _mix_out(x, mod_rows, post_g, y5, u, conv, pool, att, lw, tm):
    b, s, d = x.shape
    per_batch = mod_rows.shape[0] > 1
    row = lambda w: pl.BlockSpec((1, tm, w), lambda i, j: (i, j, 0))
    return pl.pallas_call(
        _mixout_kernel,
        grid=(b, s // tm),
        in_specs=[
            row(d),
            pl.BlockSpec((1, 1, N_MOD * d), (lambda i, j: (i, 0, 0)) if per_batch else (lambda i, j: (0, 0, 0))),
            _const_spec((1, d)),
            row(LANES), row(LANES), row(S5_WIDTH), _const_spec(lw["s5_d"]), _const_spec(lw["s5_w_glu"]),
            _const_spec(lw["s5_b_glu"]),
            row(CONV_WIDTH), row(POOL_WIDTH), row(MLA_WIDTH), _const_spec(lw["w_out"]),
        ],
        out_specs=row(d),
        out_shape=jax.ShapeDtypeStruct(x.shape, F32),
        compiler_params=_cparams("arbitrary", "arbitrary"),
        name="mixer_out_proj",
    )(x, mod_rows, post_g.reshape(1, d), y5[0], y5[1], u, lw["s5_d"].stacked, lw["s5_w_glu"].stacked,
      lw["s5_b_glu"].stacked, conv, pool, att, lw["w_out"].stacked)


def _rope_partner(w):
    q = MLA_ROPE // 4
    return jnp.concatenate([-w[..., q:2 * q], w[..., 0:q], -w[..., 3 * q:4 * q], w[..., 2 * q:3 * q]], axis=-1)


def _rope_tables(n_lat):
    rows = n_lat // GRID_W
    axis_dim = MLA_ROPE // 2
    inv = ROPE_BASE ** (-jnp.arange(0, axis_dim, 2, dtype=F32) / axis_dim)
    ang_r = jnp.arange(rows, dtype=F32)[:, None] * inv
    ang_c = jnp.arange(GRID_W, dtype=F32)[:, None] * inv
    nq = MLA_ROPE // 4

    def table(fn, fill):
        fr, fc = fn(ang_r), fn(ang_c)
        pad = HEAD_PAD - MLA_NOPE - MLA_ROPE
        by_row = jnp.concatenate([jnp.full((rows, MLA_NOPE), fill, F32), fr, fr, jnp.zeros((rows, 2 * nq), F32),
                                  jnp.full((rows, pad), fill, F32)], axis=1)
        by_col = jnp.concatenate([jnp.zeros((GRID_W, MLA_NOPE + 2 * nq), F32), fc, fc,
                                  jnp.zeros((GRID_W, pad), F32)], axis=1)
        return (by_row[:, None, :] + by_col[None, :, :]).reshape(n_lat, HEAD_PAD)

    return table(jnp.cos, 1.0), table(jnp.sin, 0.0)


def _pad_head(nope, rope):
    r = nope.shape[0]
    z = jnp.zeros((r, MLA_HEADS, HEAD_PAD - MLA_NOPE - MLA_ROPE), nope.dtype)
    return jnp.concatenate([nope, rope, z], axis=-1).reshape(r, MLA_HEADS * HEAD_PAD)


def _band_matrices(w, center):
    n, h = S5_CHUNK, S5_GROUP
    n_taps, width = w.shape
    g = width // h
    c = jnp.arange(3)[:, None, None]
    t_in = jnp.arange(n)[None, :, None]
    t_out = jnp.arange(n)[None, None, :]
    k = t_in + n * (c - 1) - t_out + center
    valid = ((k >= 0) & (k < n_taps))[..., None]
    taps = jnp.where(valid, w[jnp.clip(k, 0, n_taps - 1)], 0.0)
    taps = taps.reshape(3, n, n, g, h).transpose(3, 0, 1, 4, 2).astype(BF16)
    eye = jnp.eye(h, dtype=BF16)[None, None, None, :, None, :]
    return (taps[..., None] * eye).reshape(g * 3, n * h, n * h)


def _prep_layer(w_in, w_out, s5_d, s5_w_glu, s5_b_glu, conv_w, conv_b, conv_ln_g, conv_ln_b, pool_w, pool_scale,
                mla_q_norm, mla_w_uq, mla_kv_norm, mla_w_ukv):
    d = D_MODEL
    wi = w_in
    kr = wi[:, IN_KR:D_IN]
    zpad_l = jnp.zeros((d, MLA_NOPE), F32)
    zpad_r = jnp.zeros((d, HEAD_PAD - MLA_NOPE - MLA_ROPE), F32)
    w_in_ext = jnp.concatenate([wi[:, :IN_KR], zpad_l, kr, zpad_r, zpad_l, _rope_partner(kr), zpad_r], axis=1)
    uq = mla_w_uq.reshape(MLA_Q_RANK, MLA_HEADS, MLA_NOPE + MLA_ROPE)
    q_nope, q_rope = uq[..., :MLA_NOPE], uq[..., MLA_NOPE:]
    wq = jnp.concatenate([_pad_head(q_nope, q_rope), _pad_head(0 * q_nope, _rope_partner(q_rope))], axis=1)
    ukv = mla_w_ukv.reshape(MLA_KV_RANK, MLA_HEADS, MLA_NOPE + MLA_V)
    wk = _pad_head(ukv[..., :MLA_NOPE], jnp.zeros((MLA_KV_RANK, MLA_HEADS, MLA_ROPE), F32))
    wv = ukv[..., MLA_NOPE:].reshape(MLA_KV_RANK, MLA_WIDTH)
    eye = jnp.eye(HEAD_PAD, dtype=F32) * ((jnp.arange(HEAD_PAD) >= MLA_NOPE)
                                          & (jnp.arange(HEAD_PAD) < MLA_NOPE + MLA_ROPE))[:, None]
    place = jnp.tile(eye, (1, MLA_HEADS))
    pool_bd = jnp.zeros((POOL_WIDTH, POOL_WIDTH), F32)
    for gi in range(len(POOL_WINDOWS)):
        sl = slice(gi * POOL_GROUP, (gi + 1) * POOL_GROUP)
        pool_bd = pool_bd.at[sl, sl].set(pool_w[gi])
    return {
        "w_in": w_in_ext.astype(BF16), "w_out": w_out.astype(BF16),
        "q_norm": mla_q_norm.reshape(1, -1), "kv_norm": mla_kv_norm.reshape(1, -1),
        "wq": wq.astype(BF16), "wk": wk.astype(BF16), "wv": wv.astype(BF16), "place": place.astype(BF16),
        "s5_d": s5_d.reshape(1, -1), "s5_w_glu": s5_w_glu.astype(BF16), "s5_b_glu": s5_b_glu.reshape(1, -1),
        "conv_m": _band_matrices(conv_w, CONV_K // 2),
        "conv_b": conv_b.reshape(1, -1), "conv_ln_g": conv_ln_g.reshape(1, -1),
        "conv_ln_b": conv_ln_b.reshape(1, -1),
        "pool_w": pool_bd.astype(BF16), "pool_scale": pool_scale.reshape(1, -1),
    }


def kernel(x, c, ctx, c_ctx, w_ada, b_ada, norm_pre, norm_post, ffn_w_gate, ffn_w_up, ffn_w_down, w_in, w_out,
           s5_lam_re, s5_lam_im, s5_log_dt, s5_b_re, s5_b_im, s5_c_re, s5_c_im, s5_d, s5_w_glu, s5_b_glu, conv_w,
           conv_b, conv_ln_g, conv_ln_b, pool_w, pool_scale, mla_q_norm, mla_w_uq, mla_kv_norm, mla_w_ukv):
    batch, n_lat, d = x.shape
    n_ctx = ctx.shape[1]
    depth = w_ada.shape[0]
    assert d == D_MODEL and batch + 1 <= 8
    tm_lat = min(512, n_lat)
    tm_ffn = tm_lat
    tm_ctx = min(512, n_ctx)
    tq_lat = min(1024, n_lat)
    tk = min(1024, n_lat // 2)
    assert n_lat % tm_lat == 0 and n_ctx % S5_CHUNK == 0 and n_lat % GRID_W == 0

    c_rows = jnp.zeros((8, d), F32).at[:batch].set(c).at[batch].set(c_ctx)
    mod = _modulation(c_rows, w_ada, b_ada)
    cos_lat, sin_lat = _rope_tables(n_lat)
    cos_ctx = jnp.ones((n_ctx, HEAD_PAD), F32)
    sin_ctx = jnp.zeros((n_ctx, HEAD_PAD), F32)
    wg = ffn_w_gate.astype(BF16)
    wu = ffn_w_up.astype(BF16)
    wd = ffn_w_down.astype(BF16)
    layer_params = jax.vmap(_prep_layer)(w_in, w_out, s5_d, s5_w_glu, s5_b_glu, conv_w, conv_b, conv_ln_g, conv_ln_b,
                                         pool_w, pool_scale, mla_q_norm, mla_w_uq, mla_kv_norm, mla_w_ukv)
    s5_params = jax.vmap(_s5_matrices)(s5_lam_re, s5_lam_im, s5_log_dt, s5_b_re, s5_b_im, s5_c_re, s5_c_im)

    x_lat, x_ctx = x, ctx
    for l in range(depth):
        last = l == depth - 1
        mod_lat = mod[l, :batch].reshape(batch, 1, N_MOD * d)
        mod_ctx = mod[l, batch].reshape(1, 1, N_MOD * d)
        lw = {name: _Param(v, (l,)) for name, v in layer_params.items()}
        s5_mats = tuple(_Param(m, (l,)) for m in s5_params[:3]) + (s5_params[3][l],)
        ffn_w = [[_Param(w, (l, half)) for w in (wg, wu, wd)] for half in range(2)]
        x_lat = _ffn(x_lat, mod_lat, 0, norm_pre[l, 0], norm_post[l, 0], *ffn_w[0], tm_ffn)
        x_ctx = _ffn(x_ctx, mod_ctx, 0, norm_pre[l, 0], norm_post[l, 0], *ffn_w[0], tm_ctx)
        u_l, zc_l, zp_l, q_l, k_l, v_l = _inproj(x_lat, mod_lat, norm_pre[l, 1], lw, cos_lat, sin_lat, tm_lat)
        u_c, zc_c, zp_c, q_c, k_c, v_c = _inproj(x_ctx, mod_ctx, norm_pre[l, 1], lw, cos_ctx, sin_ctx, tm_ctx)
        y5_c, y5_l = _s5_mixer(u_c, u_l, s5_mats)
        conv_l, pool_l = _local_mixers(zc_l, zp_l, lw, min(2048, n_lat))
        vt_l, vt_c = v_l.transpose(0, 2, 1), v_c.transpose(0, 2, 1)
        att_l = _attention(q_l, k_c, vt_c, k_l, vt_l, tq_lat, tk)
        x_lat = _mix_out(x_lat, mod_lat, norm_post[l, 1], y5_l, u_l, conv_l, pool_l, att_l, lw, tm_lat)
        if not last:
            conv_c, pool_c = _local_mixers(zc_c, zp_c, lw, tm_ctx)
            att_c = _attention(q_c, k_c, vt_c, None, None, tm_ctx, tk)
            x_ctx = _mix_out(x_ctx, mod_ctx, norm_post[l, 1], y5_c, u_c, conv_c, pool_c, att_c, lw, tm_ctx)
        x_lat = _ffn(x_lat, mod_lat, 6, norm_pre[l, 2], norm_post[l, 2], *ffn_w[1], tm_ffn)
        if not last:
            x_ctx = _ffn(x_ctx, mod_ctx, 6, norm_pre[l, 2], norm_post[l, 2], *ffn_w[1], tm_ctx)
    return x_lat
```

```python
import functools
import math
from typing import NamedTuple

import jax
import jax.numpy as jnp
from jax import lax
from jax.experimental import pallas as pl
from jax.experimental.pallas import tpu as pltpu

F32 = jnp.float32
BF16 = jnp.bfloat16

D_MODEL = 1024
GRID_W = 64
EPS = 1e-6
N_MOD = 9
MACARON_WEIGHT = 0.5
D_FF = 2816
S5_WIDTH = 256
S5_GROUP = 16
S5_GROUPS = 16
S5_STATE = 64
CONV_WIDTH = 256
CONV_K = 31
POOL_WIDTH = 256
POOL_WINDOWS = (2, 4, 8, 16)
POOL_GROUP = 64
MLA_HEADS = 8
MLA_NOPE = 64
MLA_ROPE = 32
MLA_V = 64
MLA_Q_RANK = 256
MLA_KV_RANK = 128
MLA_WIDTH = MLA_HEADS * MLA_V
ROPE_BASE = 10000.0
D_MIX = S5_WIDTH + CONV_WIDTH + POOL_WIDTH + MLA_WIDTH
IN_S5 = 0
IN_CONV = IN_S5 + S5_WIDTH
IN_POOL = IN_CONV + 2 * CONV_WIDTH
IN_CQ = IN_POOL + POOL_WIDTH
IN_CKV = IN_CQ + MLA_Q_RANK
IN_KR = IN_CKV + MLA_KV_RANK
D_IN = IN_KR + MLA_ROPE

LANES = 128
HEAD_PAD = 128
S5_CHUNK = 16
S5_FLAT = S5_CHUNK * S5_GROUP
HALO = 16
D_IN_EXT = IN_KR + 2 * HEAD_PAD
FF_CHUNKS = ((0, 1024), (1024, 2048), (2048, 2816))
VMEM_LIMIT = 56 * 1024 * 1024
ATTN_HEADS = 2
Q_SCALE = (MLA_NOPE + MLA_ROPE) ** -0.5 * math.log2(math.e)


def _cparams(*sem, flags=None):
    return pltpu.CompilerParams(dimension_semantics=sem, vmem_limit_bytes=VMEM_LIMIT, flags=flags)


class _Param(NamedTuple):
    stacked: jax.Array
    idx: tuple


def _const_spec(p):
    if isinstance(p, _Param):
        n = len(p.idx)
        idx = tuple(p.idx) + (0,) * (p.stacked.ndim - n)
        return pl.BlockSpec((None,) * n + p.stacked.shape[n:], lambda *_: idx, pipeline_mode=pl.Buffered(1))
    nd = len(p)
    return pl.BlockSpec(p, lambda *_: (0,) * nd, pipeline_mode=pl.Buffered(1))


def _rms(x, g):
    return x * lax.rsqrt(jnp.mean(x * x, axis=-1, keepdims=True) + EPS) * g


def _dot(a, b):
    return jnp.dot(a, b, preferred_element_type=F32)


def _mod_kernel(c_ref, w_ref, b_ref, o_ref):
    c = c_ref[...]
    h = (c * jax.nn.sigmoid(c)).astype(BF16)
    o_ref[0] = _dot(h, w_ref[0].astype(BF16)) + b_ref[0]


def _modulation(c_rows, w_ada, b_ada):
    depth = w_ada.shape[0]
    d = D_MODEL
    return pl.pallas_call(
        _mod_kernel,
        grid=(depth, N_MOD),
        in_specs=[
            pl.BlockSpec((8, d), lambda l, j: (0, 0)),
            pl.BlockSpec((1, d, d), lambda l, j: (l, 0, j)),
            pl.BlockSpec((1, 1, d), lambda l, j: (l, 0, j)),
        ],
        out_specs=pl.BlockSpec((1, 8, d), lambda l, j: (l, 0, j)),
        out_shape=jax.ShapeDtypeStruct((depth, 8, N_MOD * d), F32),
        compiler_params=_cparams("arbitrary", "arbitrary"),
        name="adaln_mod",
    )(c_rows, w_ada, b_ada.reshape(depth, 1, N_MOD * d))


def _mod_slices(mod_ref, base):
    d = D_MODEL
    return (mod_ref[0, :, base * d:(base + 1) * d], mod_ref[0, :, (base + 1) * d:(base + 2) * d],
            mod_ref[0, :, (base + 2) * d:(base + 3) * d])


def _ffn_kernel(x_ref, mod_ref, pre_ref, post_ref, wg_ref, wu_ref, wd_ref, o_ref, *, base):
    x = x_ref[0]
    shift, scale, gate = _mod_slices(mod_ref, base)
    h = (_rms(x, pre_ref[...]) * (1.0 + scale) + shift).astype(BF16)
    acc = None
    for lo, hi in FF_CHUNKS:
        g = _dot(h, wg_ref[:, lo:hi])
        u = _dot(h, wu_ref[:, lo:hi])
        a = (g * jax.nn.sigmoid(g) * u).astype(BF16)
        part = _dot(a, wd_ref[lo:hi, :])
        acc = part if acc is None else acc + part
    o_ref[0] = x + MACARON_WEIGHT * gate * _rms(acc, post_ref[...])


def _ffn(x, mod_rows, base, pre_g, post_g, wg, wu, wd, tm):
    b, s, d = x.shape
    per_batch = mod_rows.shape[0] > 1
    return pl.pallas_call(
        functools.partial(_ffn_kernel, base=base),
        grid=(b, s // tm),
        in_specs=[
            pl.BlockSpec((1, tm, d), lambda i, j: (i, j, 0)),
            pl.BlockSpec((1, 1, N_MOD * d), (lambda i, j: (i, 0, 0)) if per_batch else (lambda i, j: (0, 0, 0))),
            _const_spec((1, d)), _const_spec((1, d)),
            _const_spec(wg), _const_spec(wu), _const_spec(wd),
        ],
        out_specs=pl.BlockSpec((1, tm, d), lambda i, j: (i, j, 0)),
        out_shape=jax.ShapeDtypeStruct(x.shape, F32),
        compiler_params=_cparams("arbitrary", "arbitrary"),
        name="ffn_half_step",
    )(x, mod_rows, pre_g.reshape(1, d), post_g.reshape(1, d), wg.stacked, wu.stacked, wd.stacked)


def _inproj_kernel(x_ref, mod_ref, pre_ref, win_ref, qn_ref, kvn_ref, wq_ref, wk_ref, wv_ref, place_ref,
                   cos_ref, sin_ref, u_ref, zc_ref, zp_ref, q_ref, k_ref, v_ref):
    x = x_ref[0]
    shift, scale, _ = _mod_slices(mod_ref, 3)
    h = (_rms(x, pre_ref[...]) * (1.0 + scale) + shift).astype(BF16)
    z = _dot(h, win_ref[...])
    u_ref[0] = z[:, IN_S5:IN_CONV]
    zc_ref[0] = z[:, IN_CONV:IN_POOL]
    zp_ref[0] = z[:, IN_POOL:IN_CQ]
    cos = cos_ref[...]
    sin = sin_ref[...]
    cos_h = jnp.concatenate([cos] * MLA_HEADS, axis=1)
    sin_h = jnp.concatenate([sin] * MLA_HEADS, axis=1)
    cqn = _rms(z[:, IN_CQ:IN_CKV], qn_ref[...]).astype(BF16)
    qq = _dot(cqn, wq_ref[...])
    hw = MLA_HEADS * HEAD_PAD
    q = qq[:, :hw] * cos_h + qq[:, hw:] * sin_h
    q_ref[0] = (q * Q_SCALE).astype(BF16)
    ckvn = _rms(z[:, IN_CKV:IN_KR], kvn_ref[...]).astype(BF16)
    kr = z[:, IN_KR:IN_KR + HEAD_PAD] * cos + z[:, IN_KR + HEAD_PAD:IN_KR + 2 * HEAD_PAD] * sin
    k = _dot(ckvn, wk_ref[...]) + _dot(kr.astype(BF16), place_ref[...])
    k_ref[0] = k.astype(BF16)
    v_ref[0] = _dot(ckvn, wv_ref[...]).astype(BF16)


def _inproj(x, mod_rows, pre_g, lw, cos_t, sin_t, tm):
    b, s, d = x.shape
    per_batch = mod_rows.shape[0] > 1
    hw = MLA_HEADS * HEAD_PAD
    row = lambda w: pl.BlockSpec((1, tm, w), lambda i, j: (i, j, 0))
    params = [lw[name] for name in ("w_in", "q_norm", "kv_norm", "wq", "wk", "wv", "place")]
    return pl.pallas_call(
        _inproj_kernel,
        grid=(b, s // tm),
        in_specs=[
            row(d),
            pl.BlockSpec((1, 1, N_MOD * d), (lambda i, j: (i, 0, 0)) if per_batch else (lambda i, j: (0, 0, 0))),
            _const_spec((1, d)),
            *[_const_spec(p) for p in params],
            pl.BlockSpec((tm, HEAD_PAD), lambda i, j: (j, 0)),
            pl.BlockSpec((tm, HEAD_PAD), lambda i, j: (j, 0)),
        ],
        out_specs=[row(S5_WIDTH), row(2 * CONV_WIDTH), row(POOL_WIDTH), row(hw), row(hw), row(MLA_WIDTH)],
        out_shape=[
            jax.ShapeDtypeStruct((b, s, S5_WIDTH), F32),
            jax.ShapeDtypeStruct((b, s, 2 * CONV_WIDTH), F32),
            jax.ShapeDtypeStruct((b, s, POOL_WIDTH), F32),
            jax.ShapeDtypeStruct((b, s, hw), BF16),
            jax.ShapeDtypeStruct((b, s, hw), BF16),
            jax.ShapeDtypeStruct((b, s, MLA_WIDTH), BF16),
        ],
        compiler_params=_cparams("arbitrary", "arbitrary"),
        name="mixer_in_proj",
    )(x, mod_rows, pre_g.reshape(1, d), *[p.stacked for p in params], cos_t, sin_t)


def _block_transpose(lo, hi):
    n = S5_CHUNK
    half = n // 2
    halves = [[lo[i] for i in range(half)] + [hi[i] for i in range(half)],
              [lo[i + half] for i in range(half)] + [hi[i + half] for i in range(half)]]
    blk = lax.broadcasted_iota(jnp.int32, lo[0].shape, 1) // S5_GROUP
    for d in (4, 2, 1):
        keep = (blk & d) == 0
        for hv in halves:
            for i in range(n):
                if i & d:
                    continue
                x, y = hv[i], hv[i + d]
                hv[i] = jnp.where(keep, x, pltpu.roll(y, d * S5_GROUP, 1))
                hv[i + d] = jnp.where(keep, pltpu.roll(x, LANES - d * S5_GROUP, 1), y)
    return halves


def _strided_rows(ref, phase, period):
    return ref[0, pl.ds(phase, ref.shape[1] // period, stride=period), :]


def _chunk_flat(ua_ref, ub_ref):
    lo, hi = _block_transpose([_strided_rows(ua_ref, t, S5_CHUNK) for t in range(S5_CHUNK)],
                              [_strided_rows(ub_ref, t, S5_CHUNK) for t in range(S5_CHUNK)])
    return [jnp.concatenate([lo[g], hi[g]], axis=1).astype(BF16) for g in range(S5_GROUPS)]


def _s5_state_kernel(ua_ref, ub_ref, m_ref, sf_ref, sb_ref):
    r = ua_ref.shape[1] // S5_CHUNK
    for g, f in enumerate(_chunk_flat(ua_ref, ub_ref)):
        s = _dot(f, m_ref[g])
        sf_ref[0, pl.ds(g, r, stride=S5_GROUPS), :] = s[:, :LANES]
        sb_ref[0, pl.ds(g, r, stride=S5_GROUPS), :] = s[:, LANES:]


def _lane_half(tm, lane_block):
    return pl.BlockSpec((1, tm, LANES), lambda i, j: (i, j, lane_block))


def _s5_chunk_states(u, m_state, tm):
    b, s, _ = u.shape
    out = jax.ShapeDtypeStruct((b, s, LANES), F32)
    return pl.pallas_call(
        _s5_state_kernel,
        grid=(b, s // tm),
        in_specs=[_lane_half(tm, 0), _lane_half(tm, 1), _const_spec(m_state)],
        out_specs=[_lane_half(tm, 0), _lane_half(tm, 0)],
        out_shape=[out, out],
        compiler_params=_cparams("arbitrary", "arbitrary"),
        name="s5_chunk_states",
    )(u, u, m_state.stacked)


def _s5_scan_kernel(scf_ref, scb_ref, sf_ref, sb_ref, a_ref, hcf_ref, hcb_ref, hf_ref, hb_ref, st_ref, wf_ref, wb_ref,
                    hrf_ref, hif_ref, hrb_ref, hib_ref):
    half = LANES // 2
    a_re = a_ref[:, :LANES]
    a_im = a_ref[:, LANES:]
    fwd_lane = lax.broadcasted_iota(jnp.int32, a_re.shape, 1) < half

    def swap(v):
        n, g, w = v.shape
        return pltpu.roll(v.reshape(n * g, w), half, 1).reshape(n, g, w)

    def run(in_f, in_b, out_f, out_b, carry):
        n = in_f.shape[1]
        wf_ref[0:n] = swap(in_f[0])
        wb_ref[0:n] = swap(in_b[0])

        def body(i, c):
            h_re, h_im = c
            r = n - 1 - i
            hrf_ref[i] = h_re
            hif_ref[i] = h_im
            hrb_ref[r] = h_re
            hib_ref[r] = h_im
            s_re = jnp.where(fwd_lane, in_f[0, i], wb_ref[r])
            s_im = jnp.where(fwd_lane, wf_ref[i], in_b[0, r])
            return (a_re * h_re - a_im * h_im + s_re, a_re * h_im + a_im * h_re + s_im)

        carry = lax.fori_loop(0, n, body, carry)
        out_f[0] = jnp.where(fwd_lane, hrf_ref[0:n], swap(hif_ref[0:n]))
        out_b[0] = jnp.where(fwd_lane, swap(hrb_ref[0:n]), hib_ref[0:n])
        return carry

    @pl.when(pl.program_id(1) == 0)
    def _():
        zero = jnp.zeros(a_re.shape, F32)
        c = run(scf_ref, scb_ref, hcf_ref, hcb_ref, (zero, zero))
        st_ref[0] = c[0]
        st_ref[1] = c[1]

    c = run(sf_ref, sb_ref, hf_ref, hb_ref, (st_ref[0], st_ref[1]))
    st_ref[0] = c[0]
    st_ref[1] = c[1]


def _s5_scan(sc_f, sc_b, sl_f, sl_b, a16):
    b, nc, g, _ = sc_f.shape
    c = sl_f.shape[1]
    cb = min(128, c)
    nb = c // cb
    ctx = pl.BlockSpec((1, nc, g, LANES), lambda i, j: (i, 0, 0, 0))
    up = pl.BlockSpec((1, cb, g, LANES), lambda i, j: (i, j, 0, 0))
    down = pl.BlockSpec((1, cb, g, LANES), lambda i, j: (i, nb - 1 - j, 0, 0))
    return pl.pallas_call(
        _s5_scan_kernel,
        grid=(b, nb),
        in_specs=[ctx, ctx, up, down, pl.BlockSpec((g, 2 * LANES), lambda i, j: (0, 0))],
        out_specs=[ctx, ctx, up, down],
        out_shape=[jax.ShapeDtypeStruct(sc_f.shape, F32), jax.ShapeDtypeStruct(sc_f.shape, F32),
                   jax.ShapeDtypeStruct(sl_f.shape, F32), jax.ShapeDtypeStruct(sl_f.shape, F32)],
        scratch_shapes=[pltpu.VMEM((2, g, LANES), F32)] + [pltpu.VMEM((max(cb, nc), g, LANES), F32)] * 6,
        compiler_params=_cparams("arbitrary", "arbitrary"),
        name="s5_chunk_scan",
    )(sc_f, sc_b, sl_f, sl_b, a16)


def _s5_out_kernel(ua_ref, ub_ref, hf_ref, hb_ref, mi_ref, mo_ref, ya_ref, yb_ref):
    lo, hi = [], []
    for g, f in enumerate(_chunk_flat(ua_ref, ub_ref)):
        hf = _strided_rows(hf_ref, g, S5_GROUPS).astype(BF16)
        hb = _strided_rows(hb_ref, g, S5_GROUPS).astype(BF16)
        y = _dot(f, mi_ref[g]) + _dot(hf, mo_ref[g, :LANES, :]) + _dot(hb, mo_ref[g, LANES:, :])
        lo.append(y[:, :LANES])
        hi.append(y[:, LANES:])
    lo, hi = _block_transpose(lo, hi)
    r = ua_ref.shape[1] // S5_CHUNK
    for t in range(S5_CHUNK):
        ya_ref[0, pl.ds(t, r, stride=S5_CHUNK), :] = lo[t]
        yb_ref[0, pl.ds(t, r, stride=S5_CHUNK), :] = hi[t]


def _s5_outputs(u, h_f, h_b, m_intra, m_out, tm):
    b, s, _ = u.shape
    out = jax.ShapeDtypeStruct((b, s, LANES), F32)
    return pl.pallas_call(
        _s5_out_kernel,
        grid=(b, s // tm),
        in_specs=[_lane_half(tm, 0), _lane_half(tm, 1), _lane_half(tm, 0), _lane_half(tm, 0),
                  _const_spec(m_intra), _const_spec(m_out)],
        out_specs=[_lane_half(tm, 0), _lane_half(tm, 0)],
        out_shape=[out, out],
        compiler_params=_cparams("arbitrary", "arbitrary"),
        name="s5_chunk_outputs",
    )(u, u, h_f, h_b, m_intra.stacked, m_out.stacked)


def _s5_matrices(lam_re, lam_im, log_dt, b_re, b_im, c_re, c_im):
    hp = lax.Precision.HIGHEST
    t = S5_CHUNK
    g, p, h = S5_GROUPS, S5_STATE, S5_GROUP
    lam = lax.complex(jnp.minimum(lam_re, -1e-4), lam_im)
    dt = jnp.exp(log_dt)[..., None]
    steps = jnp.arange(t + 1, dtype=F32)[:, None, None, None]
    apow = jnp.exp(lam[None] * dt[None] * steps)
    bb = ((apow[1] - 1.0) / lam)[..., None] * lax.complex(b_re, b_im)
    cc = lax.complex(c_re, c_im)
    taps = jnp.einsum('dghp,jdgp,dgpk->djghk', cc, apow[:t], bb, precision=hp).real
    by_dist = jnp.concatenate([taps[1, :0:-1], (taps[0, 0] + taps[1, 0])[None], taps[0, 1:]], axis=0)
    by_dist = by_dist.transpose(1, 3, 0, 2).astype(BF16).reshape(g, h, (2 * t - 1) * h)
    m_intra = jnp.stack([by_dist[:, :, (t - 1 - s) * h:(2 * t - 1 - s) * h] for s in range(t)], axis=1)
    m_intra = m_intra.reshape(g, t * h, t * h)
    wf = apow[:t][::-1, 0][:, :, :, None] * bb[0][None]
    wb = apow[:t, 1][:, :, :, None] * bb[1][None]
    to_rows = lambda w: w.transpose(1, 0, 3, 2).reshape(g, t * h, p)
    m_state = jnp.concatenate([to_rows(wf.real), to_rows(wf.imag), to_rows(wb.real), to_rows(wb.imag)], axis=-1)
    of = cc[0][None] * apow[1:, 0][:, :, None, :]
    ob = cc[1][None] * apow[1:, 1][::-1][:, :, None, :]
    to_cols = lambda w: w.transpose(1, 3, 0, 2).reshape(g, p, t * h)
    m_out = jnp.concatenate([to_cols(of.real), -to_cols(of.imag), to_cols(ob.real), -to_cols(ob.imag)], axis=1)
    a_t = apow[t]
    a16 = jnp.concatenate([a_t[0].real, a_t[1].real, a_t[0].imag, a_t[1].imag], axis=-1)
    return m_intra, m_state.astype(BF16), m_out.astype(BF16), a16


def _s5_mixer(u_ctx, u_lat, mats):
    m_intra, m_state, m_out, a16 = mats
    b, lc, w = u_ctx.shape
    ll = u_lat.shape[1]
    g, t = S5_GROUPS, S5_CHUNK
    tm_c, tm_l = min(1024, lc), min(1024, ll)
    chunks = lambda x: x.reshape(b, x.shape[1] // g, g, LANES)
    rows = lambda x: x.reshape(b, x.shape[1] * g, LANES)
    sc_f, sc_b = _s5_chunk_states(u_ctx, m_state, tm_c)
    sl_f, sl_b = _s5_chunk_states(u_lat, m_state, tm_l)
    hc_f, hc_b, hl_f, hl_b = _s5_scan(chunks(sc_f), chunks(sc_b), chunks(sl_f), chunks(sl_b), a16)
    y_ctx = _s5_outputs(u_ctx, rows(hc_f), rows(hc_b), m_intra, m_out, tm_c)
    y_lat = _s5_outputs(u_lat, rows(hl_f), rows(hl_b), m_intra, m_out, tm_l)
    return y_ctx, y_lat


def _banded_time_filter(xa_ref, xb_ref, m_ref, ya_ref, yb_ref, split):
    r = ya_ref.shape[0] // S5_CHUNK
    flo, fhi = _block_transpose([xa_ref[pl.ds(t, r + 2, stride=S5_CHUNK), :] for t in range(S5_CHUNK)],
                                [xb_ref[pl.ds(t, r + 2, stride=S5_CHUNK), :] for t in range(S5_CHUNK)])
    out_lo, out_hi = [], []
    for g in range(len(flo)):
        f = jnp.concatenate([flo[g], fhi[g]], axis=1)
        y = None
        for c in range(3):
            fc = f[c:c + r]
            top = fc.astype(BF16)
            parts = [top, (fc - top.astype(F32)).astype(BF16)] if split else [top]
            for p in parts:
                term = _dot(p, m_ref[3 * g + c])
                y = term if y is None else y + term
        out_lo.append(y[:, :LANES])
        out_hi.append(y[:, LANES:])
    out_lo, out_hi = _block_transpose(out_lo, out_hi)
    for t in range(S5_CHUNK):
        ya_ref[pl.ds(t, r, stride=S5_CHUNK), :] = out_lo[t]
        yb_ref[pl.ds(t, r, stride=S5_CHUNK), :] = out_hi[t]
    return jnp.concatenate([ya_ref[...], yb_ref[...]], axis=1)


def _local_kernel(zc_ref, zcp_ref, zcn_ref, zp_ref, zpp_ref, zpn_ref, cm_ref, cb_ref, lg_ref, lb_ref, pw_ref, ps_ref,
                  conv_ref, pool_ref, xa_ref, xb_ref, ya_ref, yb_ref, pe_ref, *, seq_len):
    tm = zc_ref.shape[1]
    j = pl.program_id(1)
    first = j == 0
    last = j == pl.num_programs(1) - 1

    def fill(prev, main, nxt):
        for rows, val in ((slice(0, HALO), jnp.where(first, 0.0, prev)), (slice(HALO, HALO + tm), main),
                          (slice(HALO + tm, HALO + tm + HALO), jnp.where(last, 0.0, nxt))):
            xa_ref[rows] = val[:, :LANES]
            xb_ref[rows] = val[:, LANES:]

    def glu(z):
        return z[:, :CONV_WIDTH] * jax.nn.sigmoid(z[:, CONV_WIDTH:])

    fill(glu(zcp_ref[0]), glu(zc_ref[0]), glu(zcn_ref[0]))
    acc = _banded_time_filter(xa_ref, xb_ref, cm_ref, ya_ref, yb_ref, split=False) + cb_ref[...]
    mu = jnp.mean(acc, axis=-1, keepdims=True)
    xc = acc - mu
    var = jnp.mean(xc * xc, axis=-1, keepdims=True)
    y = xc * lax.rsqrt(var + EPS) * lg_ref[...] + lb_ref[...]
    conv_ref[0] = (y * jax.nn.sigmoid(y)).astype(BF16)

    pe_ref[0:HALO] = jnp.where(first, 0.0, zpp_ref[0])
    pe_ref[HALO:HALO + tm] = zp_ref[0]
    pe_ref[HALO + tm:] = jnp.where(last, 0.0, zpn_ref[0])

    def tap(dlt):
        return pe_ref[pl.ds(HALO + dlt, tm), :]

    z = tap(0)
    sums = {}
    run = z + tap(1)
    sums[2] = run
    run = run + tap(-1) + tap(2)
    sums[4] = run
    run = run + tap(-3) + tap(-2) + tap(3) + tap(4)
    sums[8] = run
    for dlt in (-7, -6, -5, -4, 5, 6, 7, 8):
        run = run + tap(dlt)
    sums[16] = run
    pos = j * tm + lax.broadcasted_iota(jnp.int32, (tm, 1), 0)
    lane = lax.broadcasted_iota(jnp.int32, (tm, POOL_WIDTH), 1)
    mean = None
    for gi, w in enumerate(POOL_WINDOWS):
        lo = jnp.maximum(pos - (w - 1) // 2, 0)
        hi = jnp.minimum(pos + w // 2, seq_len - 1)
        m = sums[w] / (hi - lo + 1).astype(F32)
        mean = m if mean is None else jnp.where(lane >= gi * POOL_GROUP, m, mean)
    d = (mean - z).astype(BF16)
    pool_ref[0] = (_dot(d, pw_ref[...]) * ps_ref[...]).astype(BF16)


def _local_mixers(zc, zp, lw, tm):
    b, s, _ = zc.shape
    nh = tm // HALO
    nblk = s // HALO
    main = lambda w: pl.BlockSpec((1, tm, w), lambda i, j: (i, j, 0))
    prev = lambda w: pl.BlockSpec((1, HALO, w), lambda i, j: (i, jnp.maximum(j * nh - 1, 0), 0))
    nxt = lambda w: pl.BlockSpec((1, HALO, w), lambda i, j: (i, jnp.minimum((j + 1) * nh, nblk - 1), 0))
    cw2, pw = 2 * CONV_WIDTH, POOL_WIDTH
    params = [lw[name] for name in ("conv_m", "conv_b", "conv_ln_g", "conv_ln_b", "pool_w", "pool_scale")]
    return pl.pallas_call(
        functools.partial(_local_kernel, seq_len=s),
        grid=(b, s // tm),
        in_specs=[main(cw2), prev(cw2), nxt(cw2), main(pw), prev(pw), nxt(pw), *[_const_spec(p) for p in params]],
        out_specs=[main(CONV_WIDTH), main(pw)],
        out_shape=[jax.ShapeDtypeStruct((b, s, CONV_WIDTH), BF16), jax.ShapeDtypeStruct((b, s, pw), BF16)],
        scratch_shapes=[pltpu.VMEM((tm + 2 * HALO, LANES), F32), pltpu.VMEM((tm + 2 * HALO, LANES), F32),
                        pltpu.VMEM((tm, LANES), F32), pltpu.VMEM((tm, LANES), F32),
                        pltpu.VMEM((tm + 2 * HALO, pw), F32)],
        compiler_params=_cparams("arbitrary", "arbitrary"),
        name="conv_pool",
    )(zc, zc, zc, zp, zp, zp, *[p.stacked for p in params])


def _attn_kernel(*refs, tk, has_lat):
    if has_lat:
        q_ref, kc_ref, vtc_ref, kl_ref, vtl_ref, o_ref, s_ref = refs
    else:
        q_ref, kc_ref, vtc_ref, o_ref = refs
    tq = q_ref.shape[1]
    heads = range(ATTN_HEADS)
    qs = [q_ref[0, :, h * HEAD_PAD:(h + 1) * HEAD_PAD] for h in heads]

    def scores(h, k):
        return lax.dot_general(k, qs[h], (((1,), (1,)), ((), ())), preferred_element_type=F32)

    def consume(s, mt, vt, state):
        m, l, acc = state
        m_new = jnp.maximum(m, mt)
        p = jnp.exp2(s - m_new)
        alpha = jnp.exp2(m - m_new)
        l = alpha * l + jnp.sum(p, axis=0, keepdims=True)
        acc = alpha * acc + _dot(vt, p.astype(BF16))
        return m_new, l, acc

    init = (jnp.full((1, tq), -1e30, F32), jnp.zeros((1, tq), F32), jnp.zeros((MLA_V, tq), F32))
    state = []
    for h in heads:
        s = scores(h, kc_ref[0, :, h * HEAD_PAD:(h + 1) * HEAD_PAD])
        state.append(consume(s, jnp.max(s, axis=0, keepdims=True), vtc_ref[0, h * MLA_V:(h + 1) * MLA_V, :], init))

    if has_lat:
        n_tiles = kl_ref.shape[1] // tk

        def produce(h, slot, i):
            off = pl.multiple_of(i * tk, tk)
            s = scores(h, kl_ref[0, pl.ds(off, tk), h * HEAD_PAD:(h + 1) * HEAD_PAD])
            s_ref[h, slot] = s
            return jnp.max(s, axis=0, keepdims=True)

        def use(h, slot, i, mt, st):
            off = pl.multiple_of(i * tk, tk)
            return consume(s_ref[h, slot], mt, vtl_ref[0, h * MLA_V:(h + 1) * MLA_V, pl.ds(off, tk)], st)

        def pair(i0, st, mt0, more):
            mt1 = [produce(h, 1, i0 + 1) for h in heads]
            st = [use(h, 0, i0, mt0[h], st[h]) for h in heads]
            if more:
                mt0 = [produce(h, 0, i0 + 2) for h in heads]
            st = [use(h, 1, i0 + 1, mt1[h], st[h]) for h in heads]
            return st, mt0

        mt0 = [produce(h, 0, 0) for h in heads]
        state, mt0 = lax.fori_loop(0, n_tiles // 2 - 1, lambda j, c: pair(2 * j, c[0], c[1], True), (state, mt0))
        state, _ = pair(n_tiles - 2, state, mt0, False)
    out_t = jnp.concatenate([acc / l for _, l, acc in state], axis=0)
    o_ref[0] = out_t.T.astype(BF16)


def _attention(q, k_ctx, vt_ctx, k_lat, vt_lat, tq, tk):
    b, lq, _ = q.shape
    lc = k_ctx.shape[1]
    has_lat = k_lat is not None
    nh = ATTN_HEADS
    in_specs = [pl.BlockSpec((1, tq, nh * HEAD_PAD), lambda i, p, j: (i, j, p)),
                pl.BlockSpec((1, lc, nh * HEAD_PAD), lambda i, p, j: (i, 0, p)),
                pl.BlockSpec((1, nh * MLA_V, lc), lambda i, p, j: (i, p, 0))]
    args = [q, k_ctx, vt_ctx]
    scratch = []
    if has_lat:
        ll = k_lat.shape[1]
        assert ll % (2 * tk) == 0
        in_specs += [pl.BlockSpec((1, ll, nh * HEAD_PAD), lambda i, p, j: (i, 0, p)),
                     pl.BlockSpec((1, nh * MLA_V, ll), lambda i, p, j: (i, p, 0))]
        args += [k_lat, vt_lat]
        scratch = [pltpu.VMEM((nh, 2, tk, tq), F32)]
    return pl.pallas_call(
        functools.partial(_attn_kernel, tk=tk, has_lat=has_lat),
        grid=(b, MLA_HEADS // nh, lq // tq),
        in_specs=in_specs,
        out_specs=pl.BlockSpec((1, tq, nh * MLA_V), lambda i, p, j: (i, j, p)),
        out_shape=jax.ShapeDtypeStruct((b, lq, MLA_WIDTH), BF16),
        scratch_shapes=scratch,
        compiler_params=_cparams("arbitrary", "arbitrary", "arbitrary"),
        name="mla_attention",
    )(*args)


def _mixout_kernel(x_ref, mod_ref, post_ref, y5a_ref, y5b_ref, u_ref, d_ref, wglu_ref, bglu_ref, conv_ref, pool_ref,
                   att_ref, wout_ref, o_ref):
    x = x_ref[0]
    _, _, gate = _mod_slices(mod_ref, 3)
    y5 = jax.nn.gelu(jnp.concatenate([y5a_ref[0], y5b_ref[0]], axis=1) + d_ref[...] * u_ref[0])
    s5 = y5 * jax.nn.sigmoid(_dot(y5.astype(BF16), wglu_ref[...]) + bglu_ref[...])
    o1, o2, o3 = S5_WIDTH, S5_WIDTH + CONV_WIDTH, S5_WIDTH + CONV_WIDTH + POOL_WIDTH
    y = (_dot(s5.astype(BF16), wout_ref[0:o1, :]) + _dot(conv_ref[0], wout_ref[o1:o2, :])
         + _dot(pool_ref[0], wout_ref[o2:o3, :]) + _dot(att_ref[0], wout_ref[o3:, :]))
    o_ref[0] = x + gate * _rms(y, post_ref[...])


def _mix_out(x, mod_rows, post_g, y5, u, conv, pool, att, lw, tm):
    b, s, d = x.shape
    per_batch = mod_rows.shape[0] > 1
    row = lambda w: pl.BlockSpec((1, tm, w), lambda i, j: (i, j, 0))
    return pl.pallas_call(
        _mixout_kernel,
        grid=(b, s // tm),
        in_specs=[
            row(d),
            pl.BlockSpec((1, 1, N_MOD * d), (lambda i, j: (i, 0, 0)) if per_batch else (lambda i, j: (0, 0, 0))),
            _const_spec((1, d)),
            row(LANES), row(LANES), row(S5_WIDTH), _const_spec(lw["s5_d"]), _const_spec(lw["s5_w_glu"]),
            _const_spec(lw["s5_b_glu"]),
            row(CONV_WIDTH), row(POOL_WIDTH), row(MLA_WIDTH), _const_spec(lw["w_out"]),
        ],
        out_specs=row(d),
        out_shape=jax.ShapeDtypeStruct(x.shape, F32),
        compiler_params=_cparams("arbitrary", "arbitrary"),
        name="mixer_out_proj",
    )(x, mod_rows, post_g.reshape(1, d), y5[0], y5[1], u, lw["s5_d"].stacked, lw["s5_w_glu"].stacked,
      lw["s5_b_glu"].stacked, conv, pool, att, lw["w_out"].stacked)


def _rope_partner(w):
    q = MLA_ROPE // 4
    return jnp.concatenate([-w[..., q:2 * q], w[..., 0:q], -w[..., 3 * q:4 * q], w[..., 2 * q:3 * q]], axis=-1)


def _rope_tables(n_lat):
    rows = n_lat // GRID_W
    axis_dim = MLA_ROPE // 2
    inv = ROPE_BASE ** (-jnp.arange(0, axis_dim, 2, dtype=F32) / axis_dim)
    ang_r = jnp.arange(rows, dtype=F32)[:, None] * inv
    ang_c = jnp.arange(GRID_W, dtype=F32)[:, None] * inv
    nq = MLA_ROPE // 4

    def table(fn, fill):
        fr, fc = fn(ang_r), fn(ang_c)
        pad = HEAD_PAD - MLA_NOPE - MLA_ROPE
        by_row = jnp.concatenate([jnp.full((rows, MLA_NOPE), fill, F32), fr, fr, jnp.zeros((rows, 2 * nq), F32),
                                  jnp.full((rows, pad), fill, F32)], axis=1)
        by_col = jnp.concatenate([jnp.zeros((GRID_W, MLA_NOPE + 2 * nq), F32), fc, fc,
                                  jnp.zeros((GRID_W, pad), F32)], axis=1)
        return (by_row[:, None, :] + by_col[None, :, :]).reshape(n_lat, HEAD_PAD)

    return table(jnp.cos, 1.0), table(jnp.sin, 0.0)


def _pad_head(nope, rope):
    r = nope.shape[0]
    z = jnp.zeros((r, MLA_HEADS, HEAD_PAD - MLA_NOPE - MLA_ROPE), nope.dtype)
    return jnp.concatenate([nope, rope, z], axis=-1).reshape(r, MLA_HEADS * HEAD_PAD)


def _band_matrices(w, center):
    n, h = S5_CHUNK, S5_GROUP
    n_taps, width = w.shape
    g = width // h
    c = jnp.arange(3)[:, None, None]
    t_in = jnp.arange(n)[None, :, None]
    t_out = jnp.arange(n)[None, None, :]
    k = t_in + n * (c - 1) - t_out + center
    valid = ((k >= 0) & (k < n_taps))[..., None]
    taps = jnp.where(valid, w[jnp.clip(k, 0, n_taps - 1)], 0.0)
    taps = taps.reshape(3, n, n, g, h).transpose(3, 0, 1, 4, 2).astype(BF16)
    eye = jnp.eye(h, dtype=BF16)[None, None, None, :, None, :]
    return (taps[..., None] * eye).reshape(g * 3, n * h, n * h)


def _prep_layer(w_in, w_out, s5_d, s5_w_glu, s5_b_glu, conv_w, conv_b, conv_ln_g, conv_ln_b, pool_w, pool_scale,
                mla_q_norm, mla_w_uq, mla_kv_norm, mla_w_ukv):
    d = D_MODEL
    wi = w_in
    kr = wi[:, IN_KR:D_IN]
    zpad_l = jnp.zeros((d, MLA_NOPE), F32)
    zpad_r = jnp.zeros((d, HEAD_PAD - MLA_NOPE - MLA_ROPE), F32)
    w_in_ext = jnp.concatenate([wi[:, :IN_KR], zpad_l, kr, zpad_r, zpad_l, _rope_partner(kr), zpad_r], axis=1)
    uq = mla_w_uq.reshape(MLA_Q_RANK, MLA_HEADS, MLA_NOPE + MLA_ROPE)
    q_nope, q_rope = uq[..., :MLA_NOPE], uq[..., MLA_NOPE:]
    wq = jnp.concatenate([_pad_head(q_nope, q_rope), _pad_head(0 * q_nope, _rope_partner(q_rope))], axis=1)
    ukv = mla_w_ukv.reshape(MLA_KV_RANK, MLA_HEADS, MLA_NOPE + MLA_V)
    wk = _pad_head(ukv[..., :MLA_NOPE], jnp.zeros((MLA_KV_RANK, MLA_HEADS, MLA_ROPE), F32))
    wv = ukv[..., MLA_NOPE:].reshape(MLA_KV_RANK, MLA_WIDTH)
    eye = jnp.eye(HEAD_PAD, dtype=F32) * ((jnp.arange(HEAD_PAD) >= MLA_NOPE)
                                          & (jnp.arange(HEAD_PAD) < MLA_NOPE + MLA_ROPE))[:, None]
    place = jnp.tile(eye, (1, MLA_HEADS))
    pool_bd = jnp.zeros((POOL_WIDTH, POOL_WIDTH), F32)
    for gi in range(len(POOL_WINDOWS)):
        sl = slice(gi * POOL_GROUP, (gi + 1) * POOL_GROUP)
        pool_bd = pool_bd.at[sl, sl].set(pool_w[gi])
    return {
        "w_in": w_in_ext.astype(BF16), "w_out": w_out.astype(BF16),
        "q_norm": mla_q_norm.reshape(1, -1), "kv_norm": mla_kv_norm.reshape(1, -1),
        "wq": wq.astype(BF16), "wk": wk.astype(BF16), "wv": wv.astype(BF16), "place": place.astype(BF16),
        "s5_d": s5_d.reshape(1, -1), "s5_w_glu": s5_w_glu.astype(BF16), "s5_b_glu": s5_b_glu.reshape(1, -1),
        "conv_m": _band_matrices(conv_w, CONV_K // 2),
        "conv_b": conv_b.reshape(1, -1), "conv_ln_g": conv_ln_g.reshape(1, -1),
        "conv_ln_b": conv_ln_b.reshape(1, -1),
        "pool_w": pool_bd.astype(BF16), "pool_scale": pool_scale.reshape(1, -1),
    }


def kernel(x, c, ctx, c_ctx, w_ada, b_ada, norm_pre, norm_post, ffn_w_gate, ffn_w_up, ffn_w_down, w_in, w_out,
           s5_lam_re, s5_lam_im, s5_log_dt, s5_b_re, s5_b_im, s5_c_re, s5_c_im, s5_d, s5_w_glu, s5_b_glu, conv_w,
           conv_b, conv_ln_g, conv_ln_b, pool_w, pool_scale, mla_q_norm, mla_w_uq, mla_kv_norm, mla_w_ukv):
    batch, n_lat, d = x.shape
    n_ctx = ctx.shape[1]
    depth = w_ada.shape[0]
    assert d == D_MODEL and batch + 1 <= 8
    tm_lat = min(1024, n_lat)
    tm_ffn = min(512, n_lat)
    tm_ctx = min(512, n_ctx)
    tq_lat = min(1024, n_lat)
    tk = min(1024, n_lat // 2)
    assert n_lat % tm_lat == 0 and n_ctx % S5_CHUNK == 0 and n_lat % GRID_W == 0

    c_rows = jnp.zeros((8, d), F32).at[:batch].set(c).at[batch].set(c_ctx)
    mod = _modulation(c_rows, w_ada, b_ada)
    cos_lat, sin_lat = _rope_tables(n_lat)
    cos_ctx = jnp.ones((n_ctx, HEAD_PAD), F32)
    sin_ctx = jnp.zeros((n_ctx, HEAD_PAD), F32)
    wg = ffn_w_gate.astype(BF16)
    wu = ffn_w_up.astype(BF16)
    wd = ffn_w_down.astype(BF16)
    layer_params = jax.vmap(_prep_layer)(w_in, w_out, s5_d, s5_w_glu, s5_b_glu, conv_w, conv_b, conv_ln_g, conv_ln_b,
                                         pool_w, pool_scale, mla_q_norm, mla_w_uq, mla_kv_norm, mla_w_ukv)
    s5_params = jax.vmap(_s5_matrices)(s5_lam_re, s5_lam_im, s5_log_dt, s5_b_re, s5_b_im, s5_c_re, s5_c_im)

    x_lat, x_ctx = x, ctx
    for l in range(depth):
        last = l == depth - 1
        mod_lat = mod[l, :batch].reshape(batch, 1, N_MOD * d)
        mod_ctx = mod[l, batch].reshape(1, 1, N_MOD * d)
        lw = {name: _Param(v, (l,)) for name, v in layer_params.items()}
        s5_mats = tuple(_Param(m, (l,)) for m in s5_params[:3]) + (s5_params[3][l],)
        ffn_w = [[_Param(w, (l, half)) for w in (wg, wu, wd)] for half in range(2)]
        x_lat = _ffn(x_lat, mod_lat, 0, norm_pre[l, 0], norm_post[l, 0], *ffn_w[0], tm_ffn)
        x_ctx = _ffn(x_ctx, mod_ctx, 0, norm_pre[l, 0], norm_post[l, 0], *ffn_w[0], tm_ctx)
        u_l, zc_l, zp_l, q_l, k_l, v_l = _inproj(x_lat, mod_lat, norm_pre[l, 1], lw, cos_lat, sin_lat, tm_lat)
        u_c, zc_c, zp_c, q_c, k_c, v_c = _inproj(x_ctx, mod_ctx, norm_pre[l, 1], lw, cos_ctx, sin_ctx, tm_ctx)
        y5_c, y5_l = _s5_mixer(u_c, u_l, s5_mats)
        conv_l, pool_l = _local_mixers(zc_l, zp_l, lw, min(2048, n_lat))
        vt_l, vt_c = v_l.transpose(0, 2, 1), v_c.transpose(0, 2, 1)
        att_l = _attention(q_l, k_c, vt_c, k_l, vt_l, tq_lat, tk)
        x_lat = _mix_out(x_lat, mod_lat, norm_post[l, 1], y5_l, u_l, conv_l, pool_l, att_l, lw, tm_lat)
        if not last:
            conv_c, pool_c = _local_mixers(zc_c, zp_c, lw, tm_ctx)
            att_c = _attention(q_c, k_c, vt_c, None, None, tm_ctx, tk)
            x_ctx = _mix_out(x_ctx, mod_ctx, norm_post[l, 1], y5_c, u_c, conv_c, pool_c, att_c, lw, tm_ctx)
        x_lat = _ffn(x_lat, mod_lat, 6, norm_pre[l, 2], norm_post[l, 2], *ffn_w[1], tm_ffn)
        if not last:
            x_ctx = _ffn(x_ctx, mod_ctx, 6, norm_pre[l, 2], norm_post[l, 2], *ffn_w[1], tm_ctx)
    return x_lat
```

```python
import functools
import math
from typing import NamedTuple

import jax
import jax.numpy as jnp
from jax import lax
from jax.experimental import pallas as pl
from jax.experimental.pallas import tpu as pltpu

F32 = jnp.float32
BF16 = jnp.bfloat16

D_MODEL = 1024
GRID_W = 64
EPS = 1e-6
N_MOD = 9
MACARON_WEIGHT = 0.5
D_FF = 2816
S5_WIDTH = 256
S5_GROUP = 16
S5_GROUPS = 16
S5_STATE = 64
CONV_WIDTH = 256
CONV_K = 31
POOL_WIDTH = 256
POOL_WINDOWS = (2, 4, 8, 16)
POOL_GROUP = 64
MLA_HEADS = 8
MLA_NOPE = 64
MLA_ROPE = 32
MLA_V = 64
MLA_Q_RANK = 256
MLA_KV_RANK = 128
MLA_WIDTH = MLA_HEADS * MLA_V
ROPE_BASE = 10000.0
D_MIX = S5_WIDTH + CONV_WIDTH + POOL_WIDTH + MLA_WIDTH
IN_S5 = 0
IN_CONV = IN_S5 + S5_WIDTH
IN_POOL = IN_CONV + 2 * CONV_WIDTH
IN_CQ = IN_POOL + POOL_WIDTH
IN_CKV = IN_CQ + MLA_Q_RANK
IN_KR = IN_CKV + MLA_KV_RANK
D_IN = IN_KR + MLA_ROPE

LANES = 128
HEAD_PAD = 128
S5_CHUNK = 16
S5_FLAT = S5_CHUNK * S5_GROUP
HALO = 16
D_IN_EXT = IN_KR + 2 * HEAD_PAD
FF_CHUNKS = ((0, 1024), (1024, 2048), (2048, 2816))
VMEM_LIMIT = 56 * 1024 * 1024
ATTN_HEADS = 2
Q_SCALE = (MLA_NOPE + MLA_ROPE) ** -0.5 * math.log2(math.e)


def _cparams(*sem, flags=None):
    return pltpu.CompilerParams(dimension_semantics=sem, vmem_limit_bytes=VMEM_LIMIT, flags=flags)


class _Param(NamedTuple):
    stacked: jax.Array
    idx: tuple


def _const_spec(p):
    if isinstance(p, _Param):
        n = len(p.idx)
        idx = tuple(p.idx) + (0,) * (p.stacked.ndim - n)
        return pl.BlockSpec((None,) * n + p.stacked.shape[n:], lambda *_: idx, pipeline_mode=pl.Buffered(1))
    nd = len(p)
    return pl.BlockSpec(p, lambda *_: (0,) * nd, pipeline_mode=pl.Buffered(1))


def _rms(x, g):
    return x * lax.rsqrt(jnp.mean(x * x, axis=-1, keepdims=True) + EPS) * g


def _dot(a, b):
    return jnp.dot(a, b, preferred_element_type=F32)


def _mod_kernel(c_ref, w_ref, b_ref, o_ref):
    c = c_ref[...]
    h = (c * jax.nn.sigmoid(c)).astype(BF16)
    o_ref[0] = _dot(h, w_ref[0].astype(BF16)) + b_ref[0]


def _modulation(c_rows, w_ada, b_ada):
    depth = w_ada.shape[0]
    d = D_MODEL
    return pl.pallas_call(
        _mod_kernel,
        grid=(depth, N_MOD),
        in_specs=[
            pl.BlockSpec((8, d), lambda l, j: (0, 0)),
            pl.BlockSpec((1, d, d), lambda l, j: (l, 0, j)),
            pl.BlockSpec((1, 1, d), lambda l, j: (l, 0, j)),
        ],
        out_specs=pl.BlockSpec((1, 8, d), lambda l, j: (l, 0, j)),
        out_shape=jax.ShapeDtypeStruct((depth, 8, N_MOD * d), F32),
        compiler_params=_cparams("arbitrary", "arbitrary"),
        name="adaln_mod",
    )(c_rows, w_ada, b_ada.reshape(depth, 1, N_MOD * d))


def _mod_slices(mod_ref, base):
    d = D_MODEL
    return (mod_ref[0, :, base * d:(base + 1) * d], mod_ref[0, :, (base + 1) * d:(base + 2) * d],
            mod_ref[0, :, (base + 2) * d:(base + 3) * d])


def _ffn_kernel(x_ref, mod_ref, pre_ref, post_ref, wg_ref, wu_ref, wd_ref, o_ref, *, base):
    x = x_ref[0]
    shift, scale, gate = _mod_slices(mod_ref, base)
    h = (_rms(x, pre_ref[...]) * (1.0 + scale) + shift).astype(BF16)
    acc = None
    for lo, hi in FF_CHUNKS:
        g = _dot(h, wg_ref[:, lo:hi])
        u = _dot(h, wu_ref[:, lo:hi])
        a = (g * jax.nn.sigmoid(g) * u).astype(BF16)
        part = _dot(a, wd_ref[lo:hi, :])
        acc = part if acc is None else acc + part
    o_ref[0] = x + MACARON_WEIGHT * gate * _rms(acc, post_ref[...])


def _ffn(x, mod_rows, base, pre_g, post_g, wg, wu, wd, tm):
    b, s, d = x.shape
    per_batch = mod_rows.shape[0] > 1
    return pl.pallas_call(
        functools.partial(_ffn_kernel, base=base),
        grid=(b, s // tm),
        in_specs=[
            pl.BlockSpec((1, tm, d), lambda i, j: (i, j, 0)),
            pl.BlockSpec((1, 1, N_MOD * d), (lambda i, j: (i, 0, 0)) if per_batch else (lambda i, j: (0, 0, 0))),
            _const_spec((1, d)), _const_spec((1, d)),
            _const_spec(wg), _const_spec(wu), _const_spec(wd),
        ],
        out_specs=pl.BlockSpec((1, tm, d), lambda i, j: (i, j, 0)),
        out_shape=jax.ShapeDtypeStruct(x.shape, F32),
        compiler_params=_cparams("arbitrary", "arbitrary"),
        name="ffn_half_step",
    )(x, mod_rows, pre_g.reshape(1, d), post_g.reshape(1, d), wg.stacked, wu.stacked, wd.stacked)


def _inproj_kernel(x_ref, mod_ref, pre_ref, win_ref, qn_ref, kvn_ref, wq_ref, wk_ref, wv_ref, place_ref,
                   cos_ref, sin_ref, u_ref, zc_ref, zp_ref, q_ref, k_ref, v_ref):
    x = x_ref[0]
    shift, scale, _ = _mod_slices(mod_ref, 3)
    h = (_rms(x, pre_ref[...]) * (1.0 + scale) + shift).astype(BF16)
    z = _dot(h, win_ref[...])
    u_ref[0] = z[:, IN_S5:IN_CONV]
    zc_ref[0] = z[:, IN_CONV:IN_POOL]
    zp_ref[0] = z[:, IN_POOL:IN_CQ]
    cos = cos_ref[...]
    sin = sin_ref[...]
    cos_h = jnp.concatenate([cos] * MLA_HEADS, axis=1)
    sin_h = jnp.concatenate([sin] * MLA_HEADS, axis=1)
    cqn = _rms(z[:, IN_CQ:IN_CKV], qn_ref[...]).astype(BF16)
    qq = _dot(cqn, wq_ref[...])
    hw = MLA_HEADS * HEAD_PAD
    q = qq[:, :hw] * cos_h + qq[:, hw:] * sin_h
    q_ref[0] = (q * Q_SCALE).astype(BF16)
    ckvn = _rms(z[:, IN_CKV:IN_KR], kvn_ref[...]).astype(BF16)
    kr = z[:, IN_KR:IN_KR + HEAD_PAD] * cos + z[:, IN_KR + HEAD_PAD:IN_KR + 2 * HEAD_PAD] * sin
    k = _dot(ckvn, wk_ref[...]) + _dot(kr.astype(BF16), place_ref[...])
    k_ref[0] = k.astype(BF16)
    v_ref[0] = _dot(ckvn, wv_ref[...]).astype(BF16)


def _inproj(x, mod_rows, pre_g, lw, cos_t, sin_t, tm):
    b, s, d = x.shape
    per_batch = mod_rows.shape[0] > 1
    hw = MLA_HEADS * HEAD_PAD
    row = lambda w: pl.BlockSpec((1, tm, w), lambda i, j: (i, j, 0))
    params = [lw[name] for name in ("w_in", "q_norm", "kv_norm", "wq", "wk", "wv", "place")]
    return pl.pallas_call(
        _inproj_kernel,
        grid=(b, s // tm),
        in_specs=[
            row(d),
            pl.BlockSpec((1, 1, N_MOD * d), (lambda i, j: (i, 0, 0)) if per_batch else (lambda i, j: (0, 0, 0))),
            _const_spec((1, d)),
            *[_const_spec(p) for p in params],
            pl.BlockSpec((tm, HEAD_PAD), lambda i, j: (j, 0)),
            pl.BlockSpec((tm, HEAD_PAD), lambda i, j: (j, 0)),
        ],
        out_specs=[row(S5_WIDTH), row(2 * CONV_WIDTH), row(POOL_WIDTH), row(hw), row(hw), row(MLA_WIDTH)],
        out_shape=[
            jax.ShapeDtypeStruct((b, s, S5_WIDTH), F32),
            jax.ShapeDtypeStruct((b, s, 2 * CONV_WIDTH), F32),
            jax.ShapeDtypeStruct((b, s, POOL_WIDTH), F32),
            jax.ShapeDtypeStruct((b, s, hw), BF16),
            jax.ShapeDtypeStruct((b, s, hw), BF16),
            jax.ShapeDtypeStruct((b, s, MLA_WIDTH), BF16),
        ],
        compiler_params=_cparams("arbitrary", "arbitrary"),
        name="mixer_in_proj",
    )(x, mod_rows, pre_g.reshape(1, d), *[p.stacked for p in params], cos_t, sin_t)


def _block_transpose(lo, hi):
    n = S5_CHUNK
    half = n // 2
    halves = [[lo[i] for i in range(half)] + [hi[i] for i in range(half)],
              [lo[i + half] for i in range(half)] + [hi[i + half] for i in range(half)]]
    blk = lax.broadcasted_iota(jnp.int32, lo[0].shape, 1) // S5_GROUP
    for d in (4, 2, 1):
        keep = (blk & d) == 0
        for hv in halves:
            for i in range(n):
                if i & d:
                    continue
                x, y = hv[i], hv[i + d]
                hv[i] = jnp.where(keep, x, pltpu.roll(y, d * S5_GROUP, 1))
                hv[i + d] = jnp.where(keep, pltpu.roll(x, LANES - d * S5_GROUP, 1), y)
    return halves


def _strided_rows(ref, phase, period):
    return ref[0, pl.ds(phase, ref.shape[1] // period, stride=period), :]


def _chunk_flat(ua_ref, ub_ref):
    lo, hi = _block_transpose([_strided_rows(ua_ref, t, S5_CHUNK) for t in range(S5_CHUNK)],
                              [_strided_rows(ub_ref, t, S5_CHUNK) for t in range(S5_CHUNK)])
    return [jnp.concatenate([lo[g], hi[g]], axis=1).astype(BF16) for g in range(S5_GROUPS)]


def _s5_state_kernel(ua_ref, ub_ref, m_ref, sf_ref, sb_ref):
    r = ua_ref.shape[1] // S5_CHUNK
    for g, f in enumerate(_chunk_flat(ua_ref, ub_ref)):
        s = _dot(f, m_ref[g])
        sf_ref[0, pl.ds(g, r, stride=S5_GROUPS), :] = s[:, :LANES]
        sb_ref[0, pl.ds(g, r, stride=S5_GROUPS), :] = s[:, LANES:]


def _lane_half(tm, lane_block):
    return pl.BlockSpec((1, tm, LANES), lambda i, j: (i, j, lane_block))


def _s5_chunk_states(u, m_state, tm):
    b, s, _ = u.shape
    out = jax.ShapeDtypeStruct((b, s, LANES), F32)
    return pl.pallas_call(
        _s5_state_kernel,
        grid=(b, s // tm),
        in_specs=[_lane_half(tm, 0), _lane_half(tm, 1), _const_spec(m_state)],
        out_specs=[_lane_half(tm, 0), _lane_half(tm, 0)],
        out_shape=[out, out],
        compiler_params=_cparams("arbitrary", "arbitrary"),
        name="s5_chunk_states",
    )(u, u, m_state.stacked)


def _s5_scan_kernel(scf_ref, scb_ref, sf_ref, sb_ref, a_ref, hcf_ref, hcb_ref, hf_ref, hb_ref, st_ref, wf_ref, wb_ref,
                    hrf_ref, hif_ref, hrb_ref, hib_ref):
    half = LANES // 2
    a_re = a_ref[:, :LANES]
    a_im = a_ref[:, LANES:]
    fwd_lane = lax.broadcasted_iota(jnp.int32, a_re.shape, 1) < half

    def swap(v):
        n, g, w = v.shape
        return pltpu.roll(v.reshape(n * g, w), half, 1).reshape(n, g, w)

    def run(in_f, in_b, out_f, out_b, carry):
        n = in_f.shape[1]
        wf_ref[0:n] = swap(in_f[0])
        wb_ref[0:n] = swap(in_b[0])

        def body(i, c):
            h_re, h_im = c
            r = n - 1 - i
            hrf_ref[i] = h_re
            hif_ref[i] = h_im
            hrb_ref[r] = h_re
            hib_ref[r] = h_im
            s_re = jnp.where(fwd_lane, in_f[0, i], wb_ref[r])
            s_im = jnp.where(fwd_lane, wf_ref[i], in_b[0, r])
            return (a_re * h_re - a_im * h_im + s_re, a_re * h_im + a_im * h_re + s_im)

        carry = lax.fori_loop(0, n, body, carry)
        out_f[0] = jnp.where(fwd_lane, hrf_ref[0:n], swap(hif_ref[0:n]))
        out_b[0] = jnp.where(fwd_lane, swap(hrb_ref[0:n]), hib_ref[0:n])
        return carry

    @pl.when(pl.program_id(1) == 0)
    def _():
        zero = jnp.zeros(a_re.shape, F32)
        c = run(scf_ref, scb_ref, hcf_ref, hcb_ref, (zero, zero))
        st_ref[0] = c[0]
        st_ref[1] = c[1]

    c = run(sf_ref, sb_ref, hf_ref, hb_ref, (st_ref[0], st_ref[1]))
    st_ref[0] = c[0]
    st_ref[1] = c[1]


def _s5_scan(sc_f, sc_b, sl_f, sl_b, a16):
    b, nc, g, _ = sc_f.shape
    c = sl_f.shape[1]
    cb = min(128, c)
    nb = c // cb
    ctx = pl.BlockSpec((1, nc, g, LANES), lambda i, j: (i, 0, 0, 0))
    up = pl.BlockSpec((1, cb, g, LANES), lambda i, j: (i, j, 0, 0))
    down = pl.BlockSpec((1, cb, g, LANES), lambda i, j: (i, nb - 1 - j, 0, 0))
    return pl.pallas_call(
        _s5_scan_kernel,
        grid=(b, nb),
        in_specs=[ctx, ctx, up, down, pl.BlockSpec((g, 2 * LANES), lambda i, j: (0, 0))],
        out_specs=[ctx, ctx, up, down],
        out_shape=[jax.ShapeDtypeStruct(sc_f.shape, F32), jax.ShapeDtypeStruct(sc_f.shape, F32),
                   jax.ShapeDtypeStruct(sl_f.shape, F32), jax.ShapeDtypeStruct(sl_f.shape, F32)],
        scratch_shapes=[pltpu.VMEM((2, g, LANES), F32)] + [pltpu.VMEM((max(cb, nc), g, LANES), F32)] * 6,
        compiler_params=_cparams("arbitrary", "arbitrary"),
        name="s5_chunk_scan",
    )(sc_f, sc_b, sl_f, sl_b, a16)


def _s5_out_kernel(ua_ref, ub_ref, hf_ref, hb_ref, mi_ref, mo_ref, ya_ref, yb_ref):
    lo, hi = [], []
    for g, f in enumerate(_chunk_flat(ua_ref, ub_ref)):
        hf = _strided_rows(hf_ref, g, S5_GROUPS).astype(BF16)
        hb = _strided_rows(hb_ref, g, S5_GROUPS).astype(BF16)
        y = _dot(f, mi_ref[g]) + _dot(hf, mo_ref[g, :LANES, :]) + _dot(hb, mo_ref[g, LANES:, :])
        lo.append(y[:, :LANES])
        hi.append(y[:, LANES:])
    lo, hi = _block_transpose(lo, hi)
    r = ua_ref.shape[1] // S5_CHUNK
    for t in range(S5_CHUNK):
        ya_ref[0, pl.ds(t, r, stride=S5_CHUNK), :] = lo[t]
        yb_ref[0, pl.ds(t, r, stride=S5_CHUNK), :] = hi[t]


def _s5_outputs(u, h_f, h_b, m_intra, m_out, tm):
    b, s, _ = u.shape
    out = jax.ShapeDtypeStruct((b, s, LANES), F32)
    return pl.pallas_call(
        _s5_out_kernel,
        grid=(b, s // tm),
        in_specs=[_lane_half(tm, 0), _lane_half(tm, 1), _lane_half(tm, 0), _lane_half(tm, 0),
                  _const_spec(m_intra), _const_spec(m_out)],
        out_specs=[_lane_half(tm, 0), _lane_half(tm, 0)],
        out_shape=[out, out],
        compiler_params=_cparams("arbitrary", "arbitrary"),
        name="s5_chunk_outputs",
    )(u, u, h_f, h_b, m_intra.stacked, m_out.stacked)


def _s5_matrices(lam_re, lam_im, log_dt, b_re, b_im, c_re, c_im):
    hp = lax.Precision.HIGHEST
    t = S5_CHUNK
    g, p, h = S5_GROUPS, S5_STATE, S5_GROUP
    lam = lax.complex(jnp.minimum(lam_re, -1e-4), lam_im)
    dt = jnp.exp(log_dt)[..., None]
    steps = jnp.arange(t + 1, dtype=F32)[:, None, None, None]
    apow = jnp.exp(lam[None] * dt[None] * steps)
    bb = ((apow[1] - 1.0) / lam)[..., None] * lax.complex(b_re, b_im)
    cc = lax.complex(c_re, c_im)
    taps = jnp.einsum('dghp,jdgp,dgpk->djghk', cc, apow[:t], bb, precision=hp).real
    by_dist = jnp.concatenate([taps[1, :0:-1], (taps[0, 0] + taps[1, 0])[None], taps[0, 1:]], axis=0)
    by_dist = by_dist.transpose(1, 3, 0, 2).reshape(g, h, (2 * t - 1) * h)
    m_intra = jnp.stack([by_dist[:, :, (t - 1 - s) * h:(2 * t - 1 - s) * h] for s in range(t)], axis=1)
    m_intra = m_intra.reshape(g, t * h, t * h).astype(BF16)
    wf = apow[:t][::-1, 0][:, :, :, None] * bb[0][None]
    wb = apow[:t, 1][:, :, :, None] * bb[1][None]
    to_rows = lambda w: w.transpose(1, 0, 3, 2).reshape(g, t * h, p)
    m_state = jnp.concatenate([to_rows(wf.real), to_rows(wf.imag), to_rows(wb.real), to_rows(wb.imag)], axis=-1)
    of = cc[0][None] * apow[1:, 0][:, :, None, :]
    ob = cc[1][None] * apow[1:, 1][::-1][:, :, None, :]
    to_cols = lambda w: w.transpose(1, 3, 0, 2).reshape(g, p, t * h)
    m_out = jnp.concatenate([to_cols(of.real), -to_cols(of.imag), to_cols(ob.real), -to_cols(ob.imag)], axis=1)
    a_t = apow[t]
    a16 = jnp.concatenate([a_t[0].real, a_t[1].real, a_t[0].imag, a_t[1].imag], axis=-1)
    return m_intra, m_state.astype(BF16), m_out.astype(BF16), a16


def _s5_mixer(u_ctx, u_lat, mats):
    m_intra, m_state, m_out, a16 = mats
    b, lc, w = u_ctx.shape
    ll = u_lat.shape[1]
    g, t = S5_GROUPS, S5_CHUNK
    tm_c, tm_l = min(1024, lc), min(1024, ll)
    chunks = lambda x: x.reshape(b, x.shape[1] // g, g, LANES)
    rows = lambda x: x.reshape(b, x.shape[1] * g, LANES)
    sc_f, sc_b = _s5_chunk_states(u_ctx, m_state, tm_c)
    sl_f, sl_b = _s5_chunk_states(u_lat, m_state, tm_l)
    hc_f, hc_b, hl_f, hl_b = _s5_scan(chunks(sc_f), chunks(sc_b), chunks(sl_f), chunks(sl_b), a16)
    y_ctx = _s5_outputs(u_ctx, rows(hc_f), rows(hc_b), m_intra, m_out, tm_c)
    y_lat = _s5_outputs(u_lat, rows(hl_f), rows(hl_b), m_intra, m_out, tm_l)
    return y_ctx, y_lat


def _banded_time_filter(xa_ref, xb_ref, m_ref, ya_ref, yb_ref, split):
    r = ya_ref.shape[0] // S5_CHUNK
    flo, fhi = _block_transpose([xa_ref[pl.ds(t, r + 2, stride=S5_CHUNK), :] for t in range(S5_CHUNK)],
                                [xb_ref[pl.ds(t, r + 2, stride=S5_CHUNK), :] for t in range(S5_CHUNK)])
    out_lo, out_hi = [], []
    for g in range(len(flo)):
        f = jnp.concatenate([flo[g], fhi[g]], axis=1)
        y = None
        for c in range(3):
            fc = f[c:c + r]
            top = fc.astype(BF16)
            parts = [top, (fc - top.astype(F32)).astype(BF16)] if split else [top]
            for p in parts:
                term = _dot(p, m_ref[3 * g + c])
                y = term if y is None else y + term
        out_lo.append(y[:, :LANES])
        out_hi.append(y[:, LANES:])
    out_lo, out_hi = _block_transpose(out_lo, out_hi)
    for t in range(S5_CHUNK):
        ya_ref[pl.ds(t, r, stride=S5_CHUNK), :] = out_lo[t]
        yb_ref[pl.ds(t, r, stride=S5_CHUNK), :] = out_hi[t]
    return jnp.concatenate([ya_ref[...], yb_ref[...]], axis=1)


def _local_kernel(zc_ref, zcp_ref, zcn_ref, zp_ref, zpp_ref, zpn_ref, cm_ref, cb_ref, lg_ref, lb_ref, pw_ref, ps_ref,
                  conv_ref, pool_ref, xa_ref, xb_ref, ya_ref, yb_ref, pe_ref, *, seq_len):
    tm = zc_ref.shape[1]
    j = pl.program_id(1)
    first = j == 0
    last = j == pl.num_programs(1) - 1

    def fill(prev, main, nxt):
        for rows, val in ((slice(0, HALO), jnp.where(first, 0.0, prev)), (slice(HALO, HALO + tm), main),
                          (slice(HALO + tm, HALO + tm + HALO), jnp.where(last, 0.0, nxt))):
            xa_ref[rows] = val[:, :LANES]
            xb_ref[rows] = val[:, LANES:]

    def glu(z):
        return z[:, :CONV_WIDTH] * jax.nn.sigmoid(z[:, CONV_WIDTH:])

    fill(glu(zcp_ref[0]), glu(zc_ref[0]), glu(zcn_ref[0]))
    acc = _banded_time_filter(xa_ref, xb_ref, cm_ref, ya_ref, yb_ref, split=False) + cb_ref[...]
    mu = jnp.mean(acc, axis=-1, keepdims=True)
    xc = acc - mu
    var = jnp.mean(xc * xc, axis=-1, keepdims=True)
    y = xc * lax.rsqrt(var + EPS) * lg_ref[...] + lb_ref[...]
    conv_ref[0] = (y * jax.nn.sigmoid(y)).astype(BF16)

    pe_ref[0:HALO] = jnp.where(first, 0.0, zpp_ref[0])
    pe_ref[HALO:HALO + tm] = zp_ref[0]
    pe_ref[HALO + tm:] = jnp.where(last, 0.0, zpn_ref[0])

    def tap(dlt):
        return pe_ref[pl.ds(HALO + dlt, tm), :]

    z = tap(0)
    sums = {}
    run = z + tap(1)
    sums[2] = run
    run = run + tap(-1) + tap(2)
    sums[4] = run
    run = run + tap(-3) + tap(-2) + tap(3) + tap(4)
    sums[8] = run
    for dlt in (-7, -6, -5, -4, 5, 6, 7, 8):
        run = run + tap(dlt)
    sums[16] = run
    pos = j * tm + lax.broadcasted_iota(jnp.int32, (tm, 1), 0)
    lane = lax.broadcasted_iota(jnp.int32, (tm, POOL_WIDTH), 1)
    mean = None
    for gi, w in enumerate(POOL_WINDOWS):
        lo = jnp.maximum(pos - (w - 1) // 2, 0)
        hi = jnp.minimum(pos + w // 2, seq_len - 1)
        m = sums[w] / (hi - lo + 1).astype(F32)
        mean = m if mean is None else jnp.where(lane >= gi * POOL_GROUP, m, mean)
    d = (mean - z).astype(BF16)
    pool_ref[0] = (_dot(d, pw_ref[...]) * ps_ref[...]).astype(BF16)


def _local_mixers(zc, zp, lw, tm):
    b, s, _ = zc.shape
    nh = tm // HALO
    nblk = s // HALO
    main = lambda w: pl.BlockSpec((1, tm, w), lambda i, j: (i, j, 0))
    prev = lambda w: pl.BlockSpec((1, HALO, w), lambda i, j: (i, jnp.maximum(j * nh - 1, 0), 0))
    nxt = lambda w: pl.BlockSpec((1, HALO, w), lambda i, j: (i, jnp.minimum((j + 1) * nh, nblk - 1), 0))
    cw2, pw = 2 * CONV_WIDTH, POOL_WIDTH
    params = [lw[name] for name in ("conv_m", "conv_b", "conv_ln_g", "conv_ln_b", "pool_w", "pool_scale")]
    return pl.pallas_call(
        functools.partial(_local_kernel, seq_len=s),
        grid=(b, s // tm),
        in_specs=[main(cw2), prev(cw2), nxt(cw2), main(pw), prev(pw), nxt(pw), *[_const_spec(p) for p in params]],
        out_specs=[main(CONV_WIDTH), main(pw)],
        out_shape=[jax.ShapeDtypeStruct((b, s, CONV_WIDTH), BF16), jax.ShapeDtypeStruct((b, s, pw), BF16)],
        scratch_shapes=[pltpu.VMEM((tm + 2 * HALO, LANES), F32), pltpu.VMEM((tm + 2 * HALO, LANES), F32),
                        pltpu.VMEM((tm, LANES), F32), pltpu.VMEM((tm, LANES), F32),
                        pltpu.VMEM((tm + 2 * HALO, pw), F32)],
        compiler_params=_cparams("arbitrary", "arbitrary"),
        name="conv_pool",
    )(zc, zc, zc, zp, zp, zp, *[p.stacked for p in params])


def _attn_kernel(*refs, tk, has_lat):
    if has_lat:
        q_ref, kc_ref, vtc_ref, kl_ref, vtl_ref, o_ref, s_ref = refs
    else:
        q_ref, kc_ref, vtc_ref, o_ref = refs
    tq = q_ref.shape[1]
    heads = range(ATTN_HEADS)
    qs = [q_ref[0, :, h * HEAD_PAD:(h + 1) * HEAD_PAD] for h in heads]

    def scores(h, k):
        return lax.dot_general(k, qs[h], (((1,), (1,)), ((), ())), preferred_element_type=F32)

    def consume(s, mt, vt, state):
        m, l, acc = state
        m_new = jnp.maximum(m, mt)
        p = jnp.exp2(s - m_new)
        alpha = jnp.exp2(m - m_new)
        l = alpha * l + jnp.sum(p, axis=0, keepdims=True)
        acc = alpha * acc + _dot(vt, p.astype(BF16))
        return m_new, l, acc

    init = (jnp.full((1, tq), -1e30, F32), jnp.zeros((1, tq), F32), jnp.zeros((MLA_V, tq), F32))
    state = []
    for h in heads:
        s = scores(h, kc_ref[0, :, h * HEAD_PAD:(h + 1) * HEAD_PAD])
        state.append(consume(s, jnp.max(s, axis=0, keepdims=True), vtc_ref[0, h * MLA_V:(h + 1) * MLA_V, :], init))

    if has_lat:
        n_tiles = kl_ref.shape[1] // tk

        def produce(h, slot, i):
            off = pl.multiple_of(i * tk, tk)
            s = scores(h, kl_ref[0, pl.ds(off, tk), h * HEAD_PAD:(h + 1) * HEAD_PAD])
            s_ref[h, slot] = s
            return jnp.max(s, axis=0, keepdims=True)

        def use(h, slot, i, mt, st):
            off = pl.multiple_of(i * tk, tk)
            return consume(s_ref[h, slot], mt, vtl_ref[0, h * MLA_V:(h + 1) * MLA_V, pl.ds(off, tk)], st)

        def pair(i0, st, mt0, more):
            mt1 = [produce(h, 1, i0 + 1) for h in heads]
            st = [use(h, 0, i0, mt0[h], st[h]) for h in heads]
            if more:
                mt0 = [produce(h, 0, i0 + 2) for h in heads]
            st = [use(h, 1, i0 + 1, mt1[h], st[h]) for h in heads]
            return st, mt0

        mt0 = [produce(h, 0, 0) for h in heads]
        state, mt0 = lax.fori_loop(0, n_tiles // 2 - 1, lambda j, c: pair(2 * j, c[0], c[1], True), (state, mt0))
        state, _ = pair(n_tiles - 2, state, mt0, False)
    out_t = jnp.concatenate([acc / l for _, l, acc in state], axis=0)
    o_ref[0] = out_t.T.astype(BF16)


def _attention(q, k_ctx, vt_ctx, k_lat, vt_lat, tq, tk):
    b, lq, _ = q.shape
    lc = k_ctx.shape[1]
    has_lat = k_lat is not None
    nh = ATTN_HEADS
    in_specs = [pl.BlockSpec((1, tq, nh * HEAD_PAD), lambda i, p, j: (i, j, p)),
                pl.BlockSpec((1, lc, nh * HEAD_PAD), lambda i, p, j: (i, 0, p)),
                pl.BlockSpec((1, nh * MLA_V, lc), lambda i, p, j: (i, p, 0))]
    args = [q, k_ctx, vt_ctx]
    scratch = []
    if has_lat:
        ll = k_lat.shape[1]
        assert ll % (2 * tk) == 0
        in_specs += [pl.BlockSpec((1, ll, nh * HEAD_PAD), lambda i, p, j: (i, 0, p)),
                     pl.BlockSpec((1, nh * MLA_V, ll), lambda i, p, j: (i, p, 0))]
        args += [k_lat, vt_lat]
        scratch = [pltpu.VMEM((nh, 2, tk, tq), F32)]
    return pl.pallas_call(
        functools.partial(_attn_kernel, tk=tk, has_lat=has_lat),
        grid=(b, MLA_HEADS // nh, lq // tq),
        in_specs=in_specs,
        out_specs=pl.BlockSpec((1, tq, nh * MLA_V), lambda i, p, j: (i, j, p)),
        out_shape=jax.ShapeDtypeStruct((b, lq, MLA_WIDTH), BF16),
        scratch_shapes=scratch,
        compiler_params=_cparams("arbitrary", "arbitrary", "arbitrary"),
        name="mla_attention",
    )(*args)


def _mixout_kernel(x_ref, mod_ref, post_ref, y5a_ref, y5b_ref, u_ref, d_ref, wglu_ref, bglu_ref, conv_ref, pool_ref,
                   att_ref, wout_ref, o_ref):
    x = x_ref[0]
    _, _, gate = _mod_slices(mod_ref, 3)
    y5 = jax.nn.gelu(jnp.concatenate([y5a_ref[0], y5b_ref[0]], axis=1) + d_ref[...] * u_ref[0])
    s5 = y5 * jax.nn.sigmoid(_dot(y5.astype(BF16), wglu_ref[...]) + bglu_ref[...])
    o1, o2, o3 = S5_WIDTH, S5_WIDTH + CONV_WIDTH, S5_WIDTH + CONV_WIDTH + POOL_WIDTH
    y = (_dot(s5.astype(BF16), wout_ref[0:o1, :]) + _dot(conv_ref[0], wout_ref[o1:o2, :])
         + _dot(pool_ref[0], wout_ref[o2:o3, :]) + _dot(att_ref[0], wout_ref[o3:, :]))
    o_ref[0] = x + gate * _rms(y, post_ref[...])


def _mix_out(x, mod_rows, post_g, y5, u, conv, pool, att, lw, tm):
    b, s, d = x.shape
    per_batch = mod_rows.shape[0] > 1
    row = lambda w: pl.BlockSpec((1, tm, w), lambda i, j: (i, j, 0))
    return pl.pallas_call(
        _mixout_kernel,
        grid=(b, s // tm),
        in_specs=[
            row(d),
            pl.BlockSpec((1, 1, N_MOD * d), (lambda i, j: (i, 0, 0)) if per_batch else (lambda i, j: (0, 0, 0))),
            _const_spec((1, d)),
            row(LANES), row(LANES), row(S5_WIDTH), _const_spec(lw["s5_d"]), _const_spec(lw["s5_w_glu"]),
            _const_spec(lw["s5_b_glu"]),
            row(CONV_WIDTH), row(POOL_WIDTH), row(MLA_WIDTH), _const_spec(lw["w_out"]),
        ],
        out_specs=row(d),
        out_shape=jax.ShapeDtypeStruct(x.shape, F32),
        compiler_params=_cparams("arbitrary", "arbitrary"),
        name="mixer_out_proj",
    )(x, mod_rows, post_g.reshape(1, d), y5[0], y5[1], u, lw["s5_d"].stacked, lw["s5_w_glu"].stacked,
      lw["s5_b_glu"].stacked, conv, pool, att, lw["w_out"].stacked)


def _rope_partner(w):
    q = MLA_ROPE // 4
    return jnp.concatenate([-w[..., q:2 * q], w[..., 0:q], -w[..., 3 * q:4 * q], w[..., 2 * q:3 * q]], axis=-1)


def _rope_tables(n_lat):
    rows = n_lat // GRID_W
    axis_dim = MLA_ROPE // 2
    inv = ROPE_BASE ** (-jnp.arange(0, axis_dim, 2, dtype=F32) / axis_dim)
    ang_r = jnp.arange(rows, dtype=F32)[:, None] * inv
    ang_c = jnp.arange(GRID_W, dtype=F32)[:, None] * inv
    nq = MLA_ROPE // 4

    def table(fn, fill):
        fr, fc = fn(ang_r), fn(ang_c)
        pad = HEAD_PAD - MLA_NOPE - MLA_ROPE
        by_row = jnp.concatenate([jnp.full((rows, MLA_NOPE), fill, F32), fr, fr, jnp.zeros((rows, 2 * nq), F32),
                                  jnp.full((rows, pad), fill, F32)], axis=1)
        by_col = jnp.concatenate([jnp.zeros((GRID_W, MLA_NOPE + 2 * nq), F32), fc, fc,
                                  jnp.zeros((GRID_W, pad), F32)], axis=1)
        return (by_row[:, None, :] + by_col[None, :, :]).reshape(n_lat, HEAD_PAD)

    return table(jnp.cos, 1.0), table(jnp.sin, 0.0)


def _pad_head(nope, rope):
    r = nope.shape[0]
    z = jnp.zeros((r, MLA_HEADS, HEAD_PAD - MLA_NOPE - MLA_ROPE), nope.dtype)
    return jnp.concatenate([nope, rope, z], axis=-1).reshape(r, MLA_HEADS * HEAD_PAD)


def _band_matrices(w, center):
    n, h = S5_CHUNK, S5_GROUP
    n_taps, width = w.shape
    g = width // h
    c = jnp.arange(3)[:, None, None]
    t_in = jnp.arange(n)[None, :, None]
    t_out = jnp.arange(n)[None, None, :]
    k = t_in + n * (c - 1) - t_out + center
    valid = ((k >= 0) & (k < n_taps))[..., None]
    taps = jnp.where(valid, w[jnp.clip(k, 0, n_taps - 1)], 0.0)
    taps = taps.reshape(3, n, n, g, h).transpose(3, 0, 1, 4, 2).astype(BF16)
    eye = jnp.eye(h, dtype=BF16)[None, None, None, :, None, :]
    return (taps[..., None] * eye).reshape(g * 3, n * h, n * h)


def _prep_layer(w_in, w_out, s5_d, s5_w_glu, s5_b_glu, conv_w, conv_b, conv_ln_g, conv_ln_b, pool_w, pool_scale,
                mla_q_norm, mla_w_uq, mla_kv_norm, mla_w_ukv):
    d = D_MODEL
    wi = w_in
    kr = wi[:, IN_KR:D_IN]
    zpad_l = jnp.zeros((d, MLA_NOPE), F32)
    zpad_r = jnp.zeros((d, HEAD_PAD - MLA_NOPE - MLA_ROPE), F32)
    w_in_ext = jnp.concatenate([wi[:, :IN_KR], zpad_l, kr, zpad_r, zpad_l, _rope_partner(kr), zpad_r], axis=1)
    uq = mla_w_uq.reshape(MLA_Q_RANK, MLA_HEADS, MLA_NOPE + MLA_ROPE)
    q_nope, q_rope = uq[..., :MLA_NOPE], uq[..., MLA_NOPE:]
    wq = jnp.concatenate([_pad_head(q_nope, q_rope), _pad_head(0 * q_nope, _rope_partner(q_rope))], axis=1)
    ukv = mla_w_ukv.reshape(MLA_KV_RANK, MLA_HEADS, MLA_NOPE + MLA_V)
    wk = _pad_head(ukv[..., :MLA_NOPE], jnp.zeros((MLA_KV_RANK, MLA_HEADS, MLA_ROPE), F32))
    wv = ukv[..., MLA_NOPE:].reshape(MLA_KV_RANK, MLA_WIDTH)
    eye = jnp.eye(HEAD_PAD, dtype=F32) * ((jnp.arange(HEAD_PAD) >= MLA_NOPE)
                                          & (jnp.arange(HEAD_PAD) < MLA_NOPE + MLA_ROPE))[:, None]
    place = jnp.tile(eye, (1, MLA_HEADS))
    pool_bd = jnp.zeros((POOL_WIDTH, POOL_WIDTH), F32)
    for gi in range(len(POOL_WINDOWS)):
        sl = slice(gi * POOL_GROUP, (gi + 1) * POOL_GROUP)
        pool_bd = pool_bd.at[sl, sl].set(pool_w[gi])
    return {
        "w_in": w_in_ext.astype(BF16), "w_out": w_out.astype(BF16),
        "q_norm": mla_q_norm.reshape(1, -1), "kv_norm": mla_kv_norm.reshape(1, -1),
        "wq": wq.astype(BF16), "wk": wk.astype(BF16), "wv": wv.astype(BF16), "place": place.astype(BF16),
        "s5_d": s5_d.reshape(1, -1), "s5_w_glu": s5_w_glu.astype(BF16), "s5_b_glu": s5_b_glu.reshape(1, -1),
        "conv_m": _band_matrices(conv_w, CONV_K // 2),
        "conv_b": conv_b.reshape(1, -1), "conv_ln_g": conv_ln_g.reshape(1, -1),
        "conv_ln_b": conv_ln_b.reshape(1, -1),
        "pool_w": pool_bd.astype(BF16), "pool_scale": pool_scale.reshape(1, -1),
    }


def kernel(x, c, ctx, c_ctx, w_ada, b_ada, norm_pre, norm_post, ffn_w_gate, ffn_w_up, ffn_w_down, w_in, w_out,
           s5_lam_re, s5_lam_im, s5_log_dt, s5_b_re, s5_b_im, s5_c_re, s5_c_im, s5_d, s5_w_glu, s5_b_glu, conv_w,
           conv_b, conv_ln_g, conv_ln_b, pool_w, pool_scale, mla_q_norm, mla_w_uq, mla_kv_norm, mla_w_ukv):
    batch, n_lat, d = x.shape
    n_ctx = ctx.shape[1]
    depth = w_ada.shape[0]
    assert d == D_MODEL and batch + 1 <= 8
    tm_lat = min(1024, n_lat)
    tm_ffn = min(512, n_lat)
    tm_ctx = min(512, n_ctx)
    tq_lat = min(1024, n_lat)
    tk = min(1024, n_lat // 2)
    assert n_lat % tm_lat == 0 and n_ctx % S5_CHUNK == 0 and n_lat % GRID_W == 0

    c_rows = jnp.zeros((8, d), F32).at[:batch].set(c).at[batch].set(c_ctx)
    mod = _modulation(c_rows, w_ada, b_ada)
    cos_lat, sin_lat = _rope_tables(n_lat)
    cos_ctx = jnp.ones((n_ctx, HEAD_PAD), F32)
    sin_ctx = jnp.zeros((n_ctx, HEAD_PAD), F32)
    wg = ffn_w_gate.astype(BF16)
    wu = ffn_w_up.astype(BF16)
    wd = ffn_w_down.astype(BF16)
    layer_params = jax.vmap(_prep_layer)(w_in, w_out, s5_d, s5_w_glu, s5_b_glu, conv_w, conv_b, conv_ln_g, conv_ln_b,
                                         pool_w, pool_scale, mla_q_norm, mla_w_uq, mla_kv_norm, mla_w_ukv)
    s5_params = jax.vmap(_s5_matrices)(s5_lam_re, s5_lam_im, s5_log_dt, s5_b_re, s5_b_im, s5_c_re, s5_c_im)

    x_lat, x_ctx = x, ctx
    for l in range(depth):
        last = l == depth - 1
        mod_lat = mod[l, :batch].reshape(batch, 1, N_MOD * d)
        mod_ctx = mod[l, batch].reshape(1, 1, N_MOD * d)
        lw = {name: _Param(v, (l,)) for name, v in layer_params.items()}
        s5_mats = tuple(_Param(m, (l,)) for m in s5_params[:3]) + (s5_params[3][l],)
        ffn_w = [[_Param(w, (l, half)) for w in (wg, wu, wd)] for half in range(2)]
        x_lat = _ffn(x_lat, mod_lat, 0, norm_pre[l, 0], norm_post[l, 0], *ffn_w[0], tm_ffn)
        x_ctx = _ffn(x_ctx, mod_ctx, 0, norm_pre[l, 0], norm_post[l, 0], *ffn_w[0], tm_ctx)
        u_l, zc_l, zp_l, q_l, k_l, v_l = _inproj(x_lat, mod_lat, norm_pre[l, 1], lw, cos_lat, sin_lat, tm_lat)
        u_c, zc_c, zp_c, q_c, k_c, v_c = _inproj(x_ctx, mod_ctx, norm_pre[l, 1], lw, cos_ctx, sin_ctx, tm_ctx)
        y5_c, y5_l = _s5_mixer(u_c, u_l, s5_mats)
        conv_l, pool_l = _local_mixers(zc_l, zp_l, lw, min(2048, n_lat))
        vt_l, vt_c = v_l.transpose(0, 2, 1), v_c.transpose(0, 2, 1)
        att_l = _attention(q_l, k_c, vt_c, k_l, vt_l, tq_lat, tk)
        x_lat = _mix_out(x_lat, mod_lat, norm_post[l, 1], y5_l, u_l, conv_l, pool_l, att_l, lw, tm_lat)
        if not last:
            conv_c, pool_c = _local_mixers(zc_c, zp_c, lw, tm_ctx)
            att_c = _attention(q_c, k_c, vt_c, None, None, tm_ctx, tk)
            x_ctx = _mix_out(x_ctx, mod_ctx, norm_post[l, 1], y5_c, u_c, conv_c, pool_c, att_c, lw, tm_ctx)
        x_lat = _ffn(x_lat, mod_lat, 6, norm_pre[l, 2], norm_post[l, 2], *ffn_w[1], tm_ffn)
        if not last:
            x_ctx = _ffn(x_ctx, mod_ctx, 6, norm_pre[l, 2], norm_post[l, 2], *ffn_w[1], tm_ctx)
    return x_lat
```

```python
import functools
import math
from typing import NamedTuple

import jax
import jax.numpy as jnp
from jax import lax
from jax.experimental import pallas as pl
from jax.experimental.pallas import tpu as pltpu

F32 = jnp.float32
BF16 = jnp.bfloat16

D_MODEL = 1024
GRID_W = 64
EPS = 1e-6
N_MOD = 9
MACARON_WEIGHT = 0.5
D_FF = 2816
S5_WIDTH = 256
S5_GROUP = 16
S5_GROUPS = 16
S5_STATE = 64
CONV_WIDTH = 256
CONV_K = 31
POOL_WIDTH = 256
POOL_WINDOWS = (2, 4, 8, 16)
POOL_GROUP = 64
MLA_HEADS = 8
MLA_NOPE = 64
MLA_ROPE = 32
MLA_V = 64
MLA_Q_RANK = 256
MLA_KV_RANK = 128
MLA_WIDTH = MLA_HEADS * MLA_V
ROPE_BASE = 10000.0
D_MIX = S5_WIDTH + CONV_WIDTH + POOL_WIDTH + MLA_WIDTH
IN_S5 = 0
IN_CONV = IN_S5 + S5_WIDTH
IN_POOL = IN_CONV + 2 * CONV_WIDTH
IN_CQ = IN_POOL + POOL_WIDTH
IN_CKV = IN_CQ + MLA_Q_RANK
IN_KR = IN_CKV + MLA_KV_RANK
D_IN = IN_KR + MLA_ROPE

LANES = 128
HEAD_PAD = 128
S5_CHUNK = 16
S5_FLAT = S5_CHUNK * S5_GROUP
HALO = 16
D_IN_EXT = IN_KR + 2 * HEAD_PAD
FF_CHUNKS = ((0, 1024), (1024, 2048), (2048, 2816))
VMEM_LIMIT = 56 * 1024 * 1024
ATTN_HEADS = 2
Q_SCALE = (MLA_NOPE + MLA_ROPE) ** -0.5 * math.log2(math.e)


def _cparams(*sem, flags=None):
    return pltpu.CompilerParams(dimension_semantics=sem, vmem_limit_bytes=VMEM_LIMIT, flags=flags)


class _Param(NamedTuple):
    stacked: jax.Array
    idx: tuple


def _const_spec(p):
    if isinstance(p, _Param):
        n = len(p.idx)
        idx = tuple(p.idx) + (0,) * (p.stacked.ndim - n)
        return pl.BlockSpec((None,) * n + p.stacked.shape[n:], lambda *_: idx, pipeline_mode=pl.Buffered(1))
    nd = len(p)
    return pl.BlockSpec(p, lambda *_: (0,) * nd, pipeline_mode=pl.Buffered(1))


def _rms(x, g):
    return x * lax.rsqrt(jnp.mean(x * x, axis=-1, keepdims=True) + EPS) * g


def _dot(a, b):
    return jnp.dot(a, b, preferred_element_type=F32)


def _mod_kernel(c_ref, w_ref, b_ref, o_ref):
    c = c_ref[...]
    h = (c * jax.nn.sigmoid(c)).astype(BF16)
    o_ref[0] = _dot(h, w_ref[0].astype(BF16)) + b_ref[0]


def _modulation(c_rows, w_ada, b_ada):
    depth = w_ada.shape[0]
    d = D_MODEL
    return pl.pallas_call(
        _mod_kernel,
        grid=(depth, N_MOD),
        in_specs=[
            pl.BlockSpec((8, d), lambda l, j: (0, 0)),
            pl.BlockSpec((1, d, d), lambda l, j: (l, 0, j)),
            pl.BlockSpec((1, 1, d), lambda l, j: (l, 0, j)),
        ],
        out_specs=pl.BlockSpec((1, 8, d), lambda l, j: (l, 0, j)),
        out_shape=jax.ShapeDtypeStruct((depth, 8, N_MOD * d), F32),
        compiler_params=_cparams("arbitrary", "arbitrary"),
        name="adaln_mod",
    )(c_rows, w_ada, b_ada.reshape(depth, 1, N_MOD * d))


def _mod_slices(mod_ref, base):
    d = D_MODEL
    return (mod_ref[0, :, base * d:(base + 1) * d], mod_ref[0, :, (base + 1) * d:(base + 2) * d],
            mod_ref[0, :, (base + 2) * d:(base + 3) * d])


def _ffn_kernel(x_ref, mod_ref, pre_ref, post_ref, wg_ref, wu_ref, wd_ref, o_ref, *, base):
    x = x_ref[0]
    shift, scale, gate = _mod_slices(mod_ref, base)
    h = (_rms(x, pre_ref[...]) * (1.0 + scale) + shift).astype(BF16)
    acc = None
    for lo, hi in FF_CHUNKS:
        g = _dot(h, wg_ref[:, lo:hi])
        u = _dot(h, wu_ref[:, lo:hi])
        a = (g * jax.nn.sigmoid(g) * u).astype(BF16)
        part = _dot(a, wd_ref[lo:hi, :])
        acc = part if acc is None else acc + part
    o_ref[0] = x + MACARON_WEIGHT * gate * _rms(acc, post_ref[...])


def _ffn(x, mod_rows, base, pre_g, post_g, wg, wu, wd, tm):
    b, s, d = x.shape
    per_batch = mod_rows.shape[0] > 1
    return pl.pallas_call(
        functools.partial(_ffn_kernel, base=base),
        grid=(b, s // tm),
        in_specs=[
            pl.BlockSpec((1, tm, d), lambda i, j: (i, j, 0)),
            pl.BlockSpec((1, 1, N_MOD * d), (lambda i, j: (i, 0, 0)) if per_batch else (lambda i, j: (0, 0, 0))),
            _const_spec((1, d)), _const_spec((1, d)),
            _const_spec(wg), _const_spec(wu), _const_spec(wd),
        ],
        out_specs=pl.BlockSpec((1, tm, d), lambda i, j: (i, j, 0)),
        out_shape=jax.ShapeDtypeStruct(x.shape, F32),
        compiler_params=_cparams("arbitrary", "arbitrary"),
        name="ffn_half_step",
    )(x, mod_rows, pre_g.reshape(1, d), post_g.reshape(1, d), wg.stacked, wu.stacked, wd.stacked)


def _inproj_kernel(x_ref, mod_ref, pre_ref, win_ref, qn_ref, kvn_ref, wq_ref, wk_ref, wv_ref, place_ref,
                   cos_ref, sin_ref, u_ref, zc_ref, zp_ref, q_ref, k_ref, v_ref):
    x = x_ref[0]
    shift, scale, _ = _mod_slices(mod_ref, 3)
    h = (_rms(x, pre_ref[...]) * (1.0 + scale) + shift).astype(BF16)
    z = _dot(h, win_ref[...])
    u_ref[0] = z[:, IN_S5:IN_CONV]
    zc_ref[0] = z[:, IN_CONV:IN_POOL]
    zp_ref[0] = z[:, IN_POOL:IN_CQ]
    cos = cos_ref[...]
    sin = sin_ref[...]
    cos_h = jnp.concatenate([cos] * MLA_HEADS, axis=1)
    sin_h = jnp.concatenate([sin] * MLA_HEADS, axis=1)
    cqn = _rms(z[:, IN_CQ:IN_CKV], qn_ref[...]).astype(BF16)
    qq = _dot(cqn, wq_ref[...])
    hw = MLA_HEADS * HEAD_PAD
    q = qq[:, :hw] * cos_h + qq[:, hw:] * sin_h
    q_ref[0] = (q * Q_SCALE).astype(BF16)
    ckvn = _rms(z[:, IN_CKV:IN_KR], kvn_ref[...]).astype(BF16)
    kr = z[:, IN_KR:IN_KR + HEAD_PAD] * cos + z[:, IN_KR + HEAD_PAD:IN_KR + 2 * HEAD_PAD] * sin
    k = _dot(ckvn, wk_ref[...]) + _dot(kr.astype(BF16), place_ref[...])
    k_ref[0] = k.astype(BF16)
    v_ref[0] = _dot(ckvn, wv_ref[...]).astype(BF16)


def _inproj(x, mod_rows, pre_g, lw, cos_t, sin_t, tm):
    b, s, d = x.shape
    per_batch = mod_rows.shape[0] > 1
    hw = MLA_HEADS * HEAD_PAD
    row = lambda w: pl.BlockSpec((1, tm, w), lambda i, j: (i, j, 0))
    params = [lw[name] for name in ("w_in", "q_norm", "kv_norm", "wq", "wk", "wv", "place")]
    return pl.pallas_call(
        _inproj_kernel,
        grid=(b, s // tm),
        in_specs=[
            row(d),
            pl.BlockSpec((1, 1, N_MOD * d), (lambda i, j: (i, 0, 0)) if per_batch else (lambda i, j: (0, 0, 0))),
            _const_spec((1, d)),
            *[_const_spec(p) for p in params],
            pl.BlockSpec((tm, HEAD_PAD), lambda i, j: (j, 0)),
            pl.BlockSpec((tm, HEAD_PAD), lambda i, j: (j, 0)),
        ],
        out_specs=[row(S5_WIDTH), row(2 * CONV_WIDTH), row(POOL_WIDTH), row(hw), row(hw), row(MLA_WIDTH)],
        out_shape=[
            jax.ShapeDtypeStruct((b, s, S5_WIDTH), F32),
            jax.ShapeDtypeStruct((b, s, 2 * CONV_WIDTH), F32),
            jax.ShapeDtypeStruct((b, s, POOL_WIDTH), F32),
            jax.ShapeDtypeStruct((b, s, hw), BF16),
            jax.ShapeDtypeStruct((b, s, hw), BF16),
            jax.ShapeDtypeStruct((b, s, MLA_WIDTH), BF16),
        ],
        compiler_params=_cparams("arbitrary", "arbitrary"),
        name="mixer_in_proj",
    )(x, mod_rows, pre_g.reshape(1, d), *[p.stacked for p in params], cos_t, sin_t)


def _block_transpose(lo, hi):
    n = S5_CHUNK
    half = n // 2
    halves = [[lo[i] for i in range(half)] + [hi[i] for i in range(half)],
              [lo[i + half] for i in range(half)] + [hi[i + half] for i in range(half)]]
    blk = lax.broadcasted_iota(jnp.int32, lo[0].shape, 1) // S5_GROUP
    for d in (4, 2, 1):
        keep = (blk & d) == 0
        for hv in halves:
            for i in range(n):
                if i & d:
                    continue
                x, y = hv[i], hv[i + d]
                hv[i] = jnp.where(keep, x, pltpu.roll(y, d * S5_GROUP, 1))
                hv[i + d] = jnp.where(keep, pltpu.roll(x, LANES - d * S5_GROUP, 1), y)
    return halves


def _strided_rows(ref, phase, period):
    return ref[0, pl.ds(phase, ref.shape[1] // period, stride=period), :]


def _chunk_flat(ua_ref, ub_ref):
    lo, hi = _block_transpose([_strided_rows(ua_ref, t, S5_CHUNK) for t in range(S5_CHUNK)],
                              [_strided_rows(ub_ref, t, S5_CHUNK) for t in range(S5_CHUNK)])
    return [jnp.concatenate([lo[g], hi[g]], axis=1).astype(BF16) for g in range(S5_GROUPS)]


def _s5_state_kernel(ua_ref, ub_ref, m_ref, sf_ref, sb_ref):
    r = ua_ref.shape[1] // S5_CHUNK
    for g, f in enumerate(_chunk_flat(ua_ref, ub_ref)):
        s = _dot(f, m_ref[g])
        sf_ref[0, pl.ds(g, r, stride=S5_GROUPS), :] = s[:, :LANES]
        sb_ref[0, pl.ds(g, r, stride=S5_GROUPS), :] = s[:, LANES:]


def _lane_half(tm, lane_block):
    return pl.BlockSpec((1, tm, LANES), lambda i, j: (i, j, lane_block))


def _s5_chunk_states(u, m_state, tm):
    b, s, _ = u.shape
    out = jax.ShapeDtypeStruct((b, s, LANES), F32)
    return pl.pallas_call(
        _s5_state_kernel,
        grid=(b, s // tm),
        in_specs=[_lane_half(tm, 0), _lane_half(tm, 1), _const_spec(m_state)],
        out_specs=[_lane_half(tm, 0), _lane_half(tm, 0)],
        out_shape=[out, out],
        compiler_params=_cparams("arbitrary", "arbitrary"),
        name="s5_chunk_states",
    )(u, u, m_state.stacked)


def _s5_scan_kernel(scf_ref, scb_ref, sf_ref, sb_ref, a_ref, hcf_ref, hcb_ref, hf_ref, hb_ref, st_ref, wf_ref, wb_ref,
                    hrf_ref, hif_ref, hrb_ref, hib_ref):
    half = LANES // 2
    a_re = a_ref[:, :LANES]
    a_im = a_ref[:, LANES:]
    fwd_lane = lax.broadcasted_iota(jnp.int32, a_re.shape, 1) < half

    def swap(v):
        n, g, w = v.shape
        return pltpu.roll(v.reshape(n * g, w), half, 1).reshape(n, g, w)

    def run(in_f, in_b, out_f, out_b, carry):
        n = in_f.shape[1]
        wf_ref[0:n] = swap(in_f[0])
        wb_ref[0:n] = swap(in_b[0])

        def body(i, c):
            h_re, h_im = c
            r = n - 1 - i
            hrf_ref[i] = h_re
            hif_ref[i] = h_im
            hrb_ref[r] = h_re
            hib_ref[r] = h_im
            s_re = jnp.where(fwd_lane, in_f[0, i], wb_ref[r])
            s_im = jnp.where(fwd_lane, wf_ref[i], in_b[0, r])
            return (a_re * h_re - a_im * h_im + s_re, a_re * h_im + a_im * h_re + s_im)

        carry = lax.fori_loop(0, n, body, carry)
        out_f[0] = jnp.where(fwd_lane, hrf_ref[0:n], swap(hif_ref[0:n]))
        out_b[0] = jnp.where(fwd_lane, swap(hrb_ref[0:n]), hib_ref[0:n])
        return carry

    @pl.when(pl.program_id(1) == 0)
    def _():
        zero = jnp.zeros(a_re.shape, F32)
        c = run(scf_ref, scb_ref, hcf_ref, hcb_ref, (zero, zero))
        st_ref[0] = c[0]
        st_ref[1] = c[1]

    c = run(sf_ref, sb_ref, hf_ref, hb_ref, (st_ref[0], st_ref[1]))
    st_ref[0] = c[0]
    st_ref[1] = c[1]


def _s5_scan(sc_f, sc_b, sl_f, sl_b, a16):
    b, nc, g, _ = sc_f.shape
    c = sl_f.shape[1]
    cb = min(128, c)
    nb = c // cb
    ctx = pl.BlockSpec((1, nc, g, LANES), lambda i, j: (i, 0, 0, 0))
    up = pl.BlockSpec((1, cb, g, LANES), lambda i, j: (i, j, 0, 0))
    down = pl.BlockSpec((1, cb, g, LANES), lambda i, j: (i, nb - 1 - j, 0, 0))
    return pl.pallas_call(
        _s5_scan_kernel,
        grid=(b, nb),
        in_specs=[ctx, ctx, up, down, pl.BlockSpec((g, 2 * LANES), lambda i, j: (0, 0))],
        out_specs=[ctx, ctx, up, down],
        out_shape=[jax.ShapeDtypeStruct(sc_f.shape, F32), jax.ShapeDtypeStruct(sc_f.shape, F32),
                   jax.ShapeDtypeStruct(sl_f.shape, F32), jax.ShapeDtypeStruct(sl_f.shape, F32)],
        scratch_shapes=[pltpu.VMEM((2, g, LANES), F32)] + [pltpu.VMEM((max(cb, nc), g, LANES), F32)] * 6,
        compiler_params=_cparams("arbitrary", "arbitrary"),
        name="s5_chunk_scan",
    )(sc_f, sc_b, sl_f, sl_b, a16)


def _s5_out_kernel(ua_ref, ub_ref, hf_ref, hb_ref, mi_ref, mo_ref, ya_ref, yb_ref):
    lo, hi = [], []
    for g, f in enumerate(_chunk_flat(ua_ref, ub_ref)):
        hf = _strided_rows(hf_ref, g, S5_GROUPS).astype(BF16)
        hb = _strided_rows(hb_ref, g, S5_GROUPS).astype(BF16)
        y = _dot(f, mi_ref[g]) + _dot(hf, mo_ref[g, :LANES, :]) + _dot(hb, mo_ref[g, LANES:, :])
        lo.append(y[:, :LANES])
        hi.append(y[:, LANES:])
    lo, hi = _block_transpose(lo, hi)
    r = ua_ref.shape[1] // S5_CHUNK
    for t in range(S5_CHUNK):
        ya_ref[0, pl.ds(t, r, stride=S5_CHUNK), :] = lo[t]
        yb_ref[0, pl.ds(t, r, stride=S5_CHUNK), :] = hi[t]


def _s5_outputs(u, h_f, h_b, m_intra, m_out, tm):
    b, s, _ = u.shape
    out = jax.ShapeDtypeStruct((b, s, LANES), F32)
    return pl.pallas_call(
        _s5_out_kernel,
        grid=(b, s // tm),
        in_specs=[_lane_half(tm, 0), _lane_half(tm, 1), _lane_half(tm, 0), _lane_half(tm, 0),
                  _const_spec(m_intra), _const_spec(m_out)],
        out_specs=[_lane_half(tm, 0), _lane_half(tm, 0)],
        out_shape=[out, out],
        compiler_params=_cparams("arbitrary", "arbitrary"),
        name="s5_chunk_outputs",
    )(u, u, h_f, h_b, m_intra.stacked, m_out.stacked)


def _s5_matrices(lam_re, lam_im, log_dt, b_re, b_im, c_re, c_im):
    hp = lax.Precision.HIGHEST
    t = S5_CHUNK
    g, p, h = S5_GROUPS, S5_STATE, S5_GROUP
    lam = lax.complex(jnp.minimum(lam_re, -1e-4), lam_im)
    dt = jnp.exp(log_dt)[..., None]
    steps = jnp.arange(t + 1, dtype=F32)[:, None, None, None]
    apow = jnp.exp(lam[None] * dt[None] * steps)
    bb = ((apow[1] - 1.0) / lam)[..., None] * lax.complex(b_re, b_im)
    cc = lax.complex(c_re, c_im)
    taps = jnp.einsum('dghp,jdgp,dgpk->djghk', cc, apow[:t], bb, precision=hp).real
    by_dist = jnp.concatenate([taps[1, :0:-1], (taps[0, 0] + taps[1, 0])[None], taps[0, 1:]], axis=0)
    by_dist = by_dist.transpose(1, 3, 0, 2).reshape(g, h, (2 * t - 1) * h)
    m_intra = jnp.stack([by_dist[:, :, (t - 1 - s) * h:(2 * t - 1 - s) * h] for s in range(t)], axis=1)
    m_intra = m_intra.reshape(g, t * h, t * h).astype(BF16)
    wf = apow[:t][::-1, 0][:, :, :, None] * bb[0][None]
    wb = apow[:t, 1][:, :, :, None] * bb[1][None]
    to_rows = lambda w: w.transpose(1, 0, 3, 2).reshape(g, t * h, p)
    m_state = jnp.concatenate([to_rows(wf.real), to_rows(wf.imag), to_rows(wb.real), to_rows(wb.imag)], axis=-1)
    of = cc[0][None] * apow[1:, 0][:, :, None, :]
    ob = cc[1][None] * apow[1:, 1][::-1][:, :, None, :]
    to_cols = lambda w: w.transpose(1, 3, 0, 2).reshape(g, p, t * h)
    m_out = jnp.concatenate([to_cols(of.real), -to_cols(of.imag), to_cols(ob.real), -to_cols(ob.imag)], axis=1)
    a_t = apow[t]
    a16 = jnp.concatenate([a_t[0].real, a_t[1].real, a_t[0].imag, a_t[1].imag], axis=-1)
    return m_intra, m_state.astype(BF16), m_out.astype(BF16), a16


def _s5_mixer(u_ctx, u_lat, mats):
    m_intra, m_state, m_out, a16 = mats
    b, lc, w = u_ctx.shape
    ll = u_lat.shape[1]
    g, t = S5_GROUPS, S5_CHUNK
    tm_c, tm_l = min(1024, lc), min(1024, ll)
    chunks = lambda x: x.reshape(b, x.shape[1] // g, g, LANES)
    rows = lambda x: x.reshape(b, x.shape[1] * g, LANES)
    sc_f, sc_b = _s5_chunk_states(u_ctx, m_state, tm_c)
    sl_f, sl_b = _s5_chunk_states(u_lat, m_state, tm_l)
    hc_f, hc_b, hl_f, hl_b = _s5_scan(chunks(sc_f), chunks(sc_b), chunks(sl_f), chunks(sl_b), a16)
    y_ctx = _s5_outputs(u_ctx, rows(hc_f), rows(hc_b), m_intra, m_out, tm_c)
    y_lat = _s5_outputs(u_lat, rows(hl_f), rows(hl_b), m_intra, m_out, tm_l)
    return y_ctx, y_lat


def _banded_time_filter(xa_ref, xb_ref, m_ref, ya_ref, yb_ref, split):
    r = ya_ref.shape[0] // S5_CHUNK
    flo, fhi = _block_transpose([xa_ref[pl.ds(t, r + 2, stride=S5_CHUNK), :] for t in range(S5_CHUNK)],
                                [xb_ref[pl.ds(t, r + 2, stride=S5_CHUNK), :] for t in range(S5_CHUNK)])
    out_lo, out_hi = [], []
    for g in range(len(flo)):
        f = jnp.concatenate([flo[g], fhi[g]], axis=1)
        y = None
        for c in range(3):
            fc = f[c:c + r]
            top = fc.astype(BF16)
            parts = [top, (fc - top.astype(F32)).astype(BF16)] if split else [top]
            for p in parts:
                term = _dot(p, m_ref[3 * g + c])
                y = term if y is None else y + term
        out_lo.append(y[:, :LANES])
        out_hi.append(y[:, LANES:])
    out_lo, out_hi = _block_transpose(out_lo, out_hi)
    for t in range(S5_CHUNK):
        ya_ref[pl.ds(t, r, stride=S5_CHUNK), :] = out_lo[t]
        yb_ref[pl.ds(t, r, stride=S5_CHUNK), :] = out_hi[t]
    return jnp.concatenate([ya_ref[...], yb_ref[...]], axis=1)


def _local_kernel(zc_ref, zcp_ref, zcn_ref, zp_ref, zpp_ref, zpn_ref, cm_ref, cb_ref, lg_ref, lb_ref, pw_ref, ps_ref,
                  conv_ref, pool_ref, xa_ref, xb_ref, ya_ref, yb_ref, pe_ref, *, seq_len):
    tm = zc_ref.shape[1]
    j = pl.program_id(1)
    first = j == 0
    last = j == pl.num_programs(1) - 1

    def fill(prev, main, nxt):
        for rows, val in ((slice(0, HALO), jnp.where(first, 0.0, prev)), (slice(HALO, HALO + tm), main),
                          (slice(HALO + tm, HALO + tm + HALO), jnp.where(last, 0.0, nxt))):
            xa_ref[rows] = val[:, :LANES]
            xb_ref[rows] = val[:, LANES:]

    def glu(z):
        return z[:, :CONV_WIDTH] * jax.nn.sigmoid(z[:, CONV_WIDTH:])

    fill(glu(zcp_ref[0]), glu(zc_ref[0]), glu(zcn_ref[0]))
    acc = _banded_time_filter(xa_ref, xb_ref, cm_ref, ya_ref, yb_ref, split=False) + cb_ref[...]
    mu = jnp.mean(acc, axis=-1, keepdims=True)
    xc = acc - mu
    var = jnp.mean(xc * xc, axis=-1, keepdims=True)
    y = xc * lax.rsqrt(var + EPS) * lg_ref[...] + lb_ref[...]
    conv_ref[0] = (y * jax.nn.sigmoid(y)).astype(BF16)

    pe_ref[0:HALO] = jnp.where(first, 0.0, zpp_ref[0])
    pe_ref[HALO:HALO + tm] = zp_ref[0]
    pe_ref[HALO + tm:] = jnp.where(last, 0.0, zpn_ref[0])

    def tap(dlt):
        return pe_ref[pl.ds(HALO + dlt, tm), :]

    z = tap(0)
    sums = {}
    run = z + tap(1)
    sums[2] = run
    run = run + tap(-1) + tap(2)
    sums[4] = run
    run = run + tap(-3) + tap(-2) + tap(3) + tap(4)
    sums[8] = run
    for dlt in (-7, -6, -5, -4, 5, 6, 7, 8):
        run = run + tap(dlt)
    sums[16] = run
    pos = j * tm + lax.broadcasted_iota(jnp.int32, (tm, 1), 0)
    lane = lax.broadcasted_iota(jnp.int32, (tm, POOL_WIDTH), 1)
    mean = None
    for gi, w in enumerate(POOL_WINDOWS):
        lo = jnp.maximum(pos - (w - 1) // 2, 0)
        hi = jnp.minimum(pos + w // 2, seq_len - 1)
        m = sums[w] / (hi - lo + 1).astype(F32)
        mean = m if mean is None else jnp.where(lane >= gi * POOL_GROUP, m, mean)
    d = (mean - z).astype(BF16)
    pool_ref[0] = (_dot(d, pw_ref[...]) * ps_ref[...]).astype(BF16)


def _local_mixers(zc, zp, lw, tm):
    b, s, _ = zc.shape
    nh = tm // HALO
    nblk = s // HALO
    main = lambda w: pl.BlockSpec((1, tm, w), lambda i, j: (i, j, 0))
    prev = lambda w: pl.BlockSpec((1, HALO, w), lambda i, j: (i, jnp.maximum(j * nh - 1, 0), 0))
    nxt = lambda w: pl.BlockSpec((1, HALO, w), lambda i, j: (i, jnp.minimum((j + 1) * nh, nblk - 1), 0))
    cw2, pw = 2 * CONV_WIDTH, POOL_WIDTH
    params = [lw[name] for name in ("conv_m", "conv_b", "conv_ln_g", "conv_ln_b", "pool_w", "pool_scale")]
    return pl.pallas_call(
        functools.partial(_local_kernel, seq_len=s),
        grid=(b, s // tm),
        in_specs=[main(cw2), prev(cw2), nxt(cw2), main(pw), prev(pw), nxt(pw), *[_const_spec(p) for p in params]],
        out_specs=[main(CONV_WIDTH), main(pw)],
        out_shape=[jax.ShapeDtypeStruct((b, s, CONV_WIDTH), BF16), jax.ShapeDtypeStruct((b, s, pw), BF16)],
        scratch_shapes=[pltpu.VMEM((tm + 2 * HALO, LANES), F32), pltpu.VMEM((tm + 2 * HALO, LANES), F32),
                        pltpu.VMEM((tm, LANES), F32), pltpu.VMEM((tm, LANES), F32),
                        pltpu.VMEM((tm + 2 * HALO, pw), F32)],
        compiler_params=_cparams("arbitrary", "arbitrary"),
        name="conv_pool",
    )(zc, zc, zc, zp, zp, zp, *[p.stacked for p in params])


def _attn_kernel(*refs, tk, has_lat):
    if has_lat:
        q_ref, kc_ref, vtc_ref, kl_ref, vtl_ref, o_ref, s_ref = refs
    else:
        q_ref, kc_ref, vtc_ref, o_ref = refs
    tq = q_ref.shape[1]
    heads = range(ATTN_HEADS)
    qs = [q_ref[0, :, h * HEAD_PAD:(h + 1) * HEAD_PAD] for h in heads]

    def scores(h, k):
        return lax.dot_general(k, qs[h], (((1,), (1,)), ((), ())), preferred_element_type=F32)

    def consume(s, mt, vt, state):
        m, l, acc = state
        m_new = jnp.maximum(m, mt)
        p = jnp.exp2(s - m_new)
        alpha = jnp.exp2(m - m_new)
        l = alpha * l + jnp.sum(p, axis=0, keepdims=True)
        acc = alpha * acc + _dot(vt, p.astype(BF16))
        return m_new, l, acc

    init = (jnp.full((1, tq), -1e30, F32), jnp.zeros((1, tq), F32), jnp.zeros((MLA_V, tq), F32))
    state = []
    for h in heads:
        s = scores(h, kc_ref[0, :, h * HEAD_PAD:(h + 1) * HEAD_PAD])
        state.append(consume(s, jnp.max(s, axis=0, keepdims=True), vtc_ref[0, h * MLA_V:(h + 1) * MLA_V, :], init))

    if has_lat:
        n_tiles = kl_ref.shape[1] // tk

        def produce(h, slot, i):
            off = pl.multiple_of(i * tk, tk)
            s = scores(h, kl_ref[0, pl.ds(off, tk), h * HEAD_PAD:(h + 1) * HEAD_PAD])
            s_ref[h, slot] = s
            return jnp.max(s, axis=0, keepdims=True)

        def use(h, slot, i, mt, st):
            off = pl.multiple_of(i * tk, tk)
            return consume(s_ref[h, slot], mt, vtl_ref[0, h * MLA_V:(h + 1) * MLA_V, pl.ds(off, tk)], st)

        def pair(i0, st, mt0, more):
            mt1 = [produce(h, 1, i0 + 1) for h in heads]
            st = [use(h, 0, i0, mt0[h], st[h]) for h in heads]
            if more:
                mt0 = [produce(h, 0, i0 + 2) for h in heads]
            st = [use(h, 1, i0 + 1, mt1[h], st[h]) for h in heads]
            return st, mt0

        mt0 = [produce(h, 0, 0) for h in heads]
        state, mt0 = lax.fori_loop(0, n_tiles // 2 - 1, lambda j, c: pair(2 * j, c[0], c[1], True), (state, mt0))
        state, _ = pair(n_tiles - 2, state, mt0, False)
    out_t = jnp.concatenate([acc / l for _, l, acc in state], axis=0)
    o_ref[0] = out_t.T.astype(BF16)


def _attention(q, k_ctx, vt_ctx, k_lat, vt_lat, tq, tk):
    b, lq, _ = q.shape
    lc = k_ctx.shape[1]
    has_lat = k_lat is not None
    nh = ATTN_HEADS
    in_specs = [pl.BlockSpec((1, tq, nh * HEAD_PAD), lambda i, p, j: (i, j, p)),
                pl.BlockSpec((1, lc, nh * HEAD_PAD), lambda i, p, j: (i, 0, p)),
                pl.BlockSpec((1, nh * MLA_V, lc), lambda i, p, j: (i, p, 0))]
    args = [q, k_ctx, vt_ctx]
    scratch = []
    if has_lat:
        ll = k_lat.shape[1]
        assert ll % (2 * tk) == 0
        in_specs += [pl.BlockSpec((1, ll, nh * HEAD_PAD), lambda i, p, j: (i, 0, p)),
                     pl.BlockSpec((1, nh * MLA_V, ll), lambda i, p, j: (i, p, 0))]
        args += [k_lat, vt_lat]
        scratch = [pltpu.VMEM((nh, 2, tk, tq), F32)]
    return pl.pallas_call(
        functools.partial(_attn_kernel, tk=tk, has_lat=has_lat),
        grid=(b, MLA_HEADS // nh, lq // tq),
        in_specs=in_specs,
        out_specs=pl.BlockSpec((1, tq, nh * MLA_V), lambda i, p, j: (i, j, p)),
        out_shape=jax.ShapeDtypeStruct((b, lq, MLA_WIDTH), BF16),
        scratch_shapes=scratch,
        compiler_params=_cparams("arbitrary", "arbitrary", "arbitrary"),
        name="mla_attention",
    )(*args)


def _mixout_kernel(x_ref, mod_ref, post_ref, y5a_ref, y5b_ref, u_ref, d_ref, wglu_ref, bglu_ref, conv_ref, pool_ref,
                   att_ref, wout_ref, o_ref):
    x = x_ref[0]
    _, _, gate = _mod_slices(mod_ref, 3)
    y5 = jax.nn.gelu(jnp.concatenate([y5a_ref[0], y5b_ref[0]], axis=1) + d_ref[...] * u_ref[0])
    s5 = y5 * jax.nn.sigmoid(_dot(y5.astype(BF16), wglu_ref[...]) + bglu_ref[...])
    o1, o2, o3 = S5_WIDTH, S5_WIDTH + CONV_WIDTH, S5_WIDTH + CONV_WIDTH + POOL_WIDTH
    y = (_dot(s5.astype(BF16), wout_ref[0:o1, :]) + _dot(conv_ref[0], wout_ref[o1:o2, :])
         + _dot(pool_ref[0], wout_ref[o2:o3, :]) + _dot(att_ref[0], wout_ref[o3:, :]))
    o_ref[0] = x + gate * _rms(y, post_ref[...])


def _mix_out(x, mod_rows, post_g, y5, u, conv, pool, att, lw, tm):
    b, s, d = x.shape
    per_batch = mod_rows.shape[0] > 1
    row = lambda w: pl.BlockSpec((1, tm, w), lambda i, j: (i, j, 0))
    return pl.pallas_call(
        _mixout_kernel,
        grid=(b, s // tm),
        in_specs=[
            row(d),
            pl.BlockSpec((1, 1, N_MOD * d), (lambda i, j: (i, 0, 0)) if per_batch else (lambda i, j: (0, 0, 0))),
            _const_spec((1, d)),
            row(LANES), row(LANES), row(S5_WIDTH), _const_spec(lw["s5_d"]), _const_spec(lw["s5_w_glu"]),
            _const_spec(lw["s5_b_glu"]),
            row(CONV_WIDTH), row(POOL_WIDTH), row(MLA_WIDTH), _const_spec(lw["w_out"]),
        ],
        out_specs=row(d),
        out_shape=jax.ShapeDtypeStruct(x.shape, F32),
        compiler_params=_cparams("arbitrary", "arbitrary"),
        name="mixer_out_proj",
    )(x, mod_rows, post_g.reshape(1, d), y5[0], y5[1], u, lw["s5_d"].stacked, lw["s5_w_glu"].stacked,
      lw["s5_b_glu"].stacked, conv, pool, att, lw["w_out"].stacked)


def _rope_partner(w):
    q = MLA_ROPE // 4
    return jnp.concatenate([-w[..., q:2 * q], w[..., 0:q], -w[..., 3 * q:4 * q], w[..., 2 * q:3 * q]], axis=-1)


def _rope_tables(n_lat):
    rows = n_lat // GRID_W
    axis_dim = MLA_ROPE // 2
    inv = ROPE_BASE ** (-jnp.arange(0, axis_dim, 2, dtype=F32) / axis_dim)
    ang_r = jnp.arange(rows, dtype=F32)[:, None] * inv
    ang_c = jnp.arange(GRID_W, dtype=F32)[:, None] * inv
    nq = MLA_ROPE // 4

    def table(fn, fill):
        fr, fc = fn(ang_r), fn(ang_c)
        pad = HEAD_PAD - MLA_NOPE - MLA_ROPE
        by_row = jnp.concatenate([jnp.full((rows, MLA_NOPE), fill, F32), fr, fr, jnp.zeros((rows, 2 * nq), F32),
                                  jnp.full((rows, pad), fill, F32)], axis=1)
        by_col = jnp.concatenate([jnp.zeros((GRID_W, MLA_NOPE + 2 * nq), F32), fc, fc,
                                  jnp.zeros((GRID_W, pad), F32)], axis=1)
        return (by_row[:, None, :] + by_col[None, :, :]).reshape(n_lat, HEAD_PAD)

    return table(jnp.cos, 1.0), table(jnp.sin, 0.0)


def _pad_head(nope, rope):
    r = nope.shape[0]
    z = jnp.zeros((r, MLA_HEADS, HEAD_PAD - MLA_NOPE - MLA_ROPE), nope.dtype)
    return jnp.concatenate([nope, rope, z], axis=-1).reshape(r, MLA_HEADS * HEAD_PAD)


def _band_matrices(w, center):
    n, h = S5_CHUNK, S5_GROUP
    n_taps, width = w.shape
    g = width // h
    k_max = (n - 1) + n + center
    left = k_max - (n_taps - 1)
    line = jnp.pad(w[::-1], ((left, 4 * n - 1 - left - n_taps), (0, 0)))
    spread = line.T.reshape(g, h, -1)[..., None] * jnp.eye(h, dtype=F32)[None, :, None, :]
    spread = spread.reshape(g, h, -1)
    rows = []
    for c in range(3):
        for t_in in range(n):
            off = k_max - (t_in + n * (c - 1) + center)
            rows.append(spread[:, :, off * h:(off + n) * h])
    m = jnp.stack(rows, axis=1)
    return m.reshape(g * 3, n * h, n * h).astype(BF16)


def _prep_layer(w_in, w_out, s5_d, s5_w_glu, s5_b_glu, conv_w, conv_b, conv_ln_g, conv_ln_b, pool_w, pool_scale,
                mla_q_norm, mla_w_uq, mla_kv_norm, mla_w_ukv):
    d = D_MODEL
    wi = w_in
    kr = wi[:, IN_KR:D_IN]
    zpad_l = jnp.zeros((d, MLA_NOPE), F32)
    zpad_r = jnp.zeros((d, HEAD_PAD - MLA_NOPE - MLA_ROPE), F32)
    w_in_ext = jnp.concatenate([wi[:, :IN_KR], zpad_l, kr, zpad_r, zpad_l, _rope_partner(kr), zpad_r], axis=1)
    uq = mla_w_uq.reshape(MLA_Q_RANK, MLA_HEADS, MLA_NOPE + MLA_ROPE)
    q_nope, q_rope = uq[..., :MLA_NOPE], uq[..., MLA_NOPE:]
    wq = jnp.concatenate([_pad_head(q_nope, q_rope), _pad_head(0 * q_nope, _rope_partner(q_rope))], axis=1)
    ukv = mla_w_ukv.reshape(MLA_KV_RANK, MLA_HEADS, MLA_NOPE + MLA_V)
    wk = _pad_head(ukv[..., :MLA_NOPE], jnp.zeros((MLA_KV_RANK, MLA_HEADS, MLA_ROPE), F32))
    wv = ukv[..., MLA_NOPE:].reshape(MLA_KV_RANK, MLA_WIDTH)
    eye = jnp.eye(HEAD_PAD, dtype=F32) * ((jnp.arange(HEAD_PAD) >= MLA_NOPE)
                                          & (jnp.arange(HEAD_PAD) < MLA_NOPE + MLA_ROPE))[:, None]
    place = jnp.tile(eye, (1, MLA_HEADS))
    pool_bd = jnp.zeros((POOL_WIDTH, POOL_WIDTH), F32)
    for gi in range(len(POOL_WINDOWS)):
        sl = slice(gi * POOL_GROUP, (gi + 1) * POOL_GROUP)
        pool_bd = pool_bd.at[sl, sl].set(pool_w[gi])
    return {
        "w_in": w_in_ext.astype(BF16), "w_out": w_out.astype(BF16),
        "q_norm": mla_q_norm.reshape(1, -1), "kv_norm": mla_kv_norm.reshape(1, -1),
        "wq": wq.astype(BF16), "wk": wk.astype(BF16), "wv": wv.astype(BF16), "place": place.astype(BF16),
        "s5_d": s5_d.reshape(1, -1), "s5_w_glu": s5_w_glu.astype(BF16), "s5_b_glu": s5_b_glu.reshape(1, -1),
        "conv_m": _band_matrices(conv_w, CONV_K // 2),
        "conv_b": conv_b.reshape(1, -1), "conv_ln_g": conv_ln_g.reshape(1, -1),
        "conv_ln_b": conv_ln_b.reshape(1, -1),
        "pool_w": pool_bd.astype(BF16), "pool_scale": pool_scale.reshape(1, -1),
    }


def kernel(x, c, ctx, c_ctx, w_ada, b_ada, norm_pre, norm_post, ffn_w_gate, ffn_w_up, ffn_w_down, w_in, w_out,
           s5_lam_re, s5_lam_im, s5_log_dt, s5_b_re, s5_b_im, s5_c_re, s5_c_im, s5_d, s5_w_glu, s5_b_glu, conv_w,
           conv_b, conv_ln_g, conv_ln_b, pool_w, pool_scale, mla_q_norm, mla_w_uq, mla_kv_norm, mla_w_ukv):
    batch, n_lat, d = x.shape
    n_ctx = ctx.shape[1]
    depth = w_ada.shape[0]
    assert d == D_MODEL and batch + 1 <= 8
    tm_lat = min(1024, n_lat)
    tm_ffn = min(512, n_lat)
    tm_ctx = min(512, n_ctx)
    tq_lat = min(1024, n_lat)
    tk = min(1024, n_lat // 2)
    assert n_lat % tm_lat == 0 and n_ctx % S5_CHUNK == 0 and n_lat % GRID_W == 0

    c_rows = jnp.zeros((8, d), F32).at[:batch].set(c).at[batch].set(c_ctx)
    mod = _modulation(c_rows, w_ada, b_ada)
    cos_lat, sin_lat = _rope_tables(n_lat)
    cos_ctx = jnp.ones((n_ctx, HEAD_PAD), F32)
    sin_ctx = jnp.zeros((n_ctx, HEAD_PAD), F32)
    wg = ffn_w_gate.astype(BF16)
    wu = ffn_w_up.astype(BF16)
    wd = ffn_w_down.astype(BF16)
    layer_params = jax.vmap(_prep_layer)(w_in, w_out, s5_d, s5_w_glu, s5_b_glu, conv_w, conv_b, conv_ln_g, conv_ln_b,
                                         pool_w, pool_scale, mla_q_norm, mla_w_uq, mla_kv_norm, mla_w_ukv)
    s5_params = jax.vmap(_s5_matrices)(s5_lam_re, s5_lam_im, s5_log_dt, s5_b_re, s5_b_im, s5_c_re, s5_c_im)

    x_lat, x_ctx = x, ctx
    for l in range(depth):
        last = l == depth - 1
        mod_lat = mod[l, :batch].reshape(batch, 1, N_MOD * d)
        mod_ctx = mod[l, batch].reshape(1, 1, N_MOD * d)
        lw = {name: _Param(v, (l,)) for name, v in layer_params.items()}
        s5_mats = tuple(_Param(m, (l,)) for m in s5_params[:3]) + (s5_params[3][l],)
        ffn_w = [[_Param(w, (l, half)) for w in (wg, wu, wd)] for half in range(2)]
        x_lat = _ffn(x_lat, mod_lat, 0, norm_pre[l, 0], norm_post[l, 0], *ffn_w[0], tm_ffn)
        x_ctx = _ffn(x_ctx, mod_ctx, 0, norm_pre[l, 0], norm_post[l, 0], *ffn_w[0], tm_ctx)
        u_l, zc_l, zp_l, q_l, k_l, v_l = _inproj(x_lat, mod_lat, norm_pre[l, 1], lw, cos_lat, sin_lat, tm_lat)
        u_c, zc_c, zp_c, q_c, k_c, v_c = _inproj(x_ctx, mod_ctx, norm_pre[l, 1], lw, cos_ctx, sin_ctx, tm_ctx)
        y5_c, y5_l = _s5_mixer(u_c, u_l, s5_mats)
        conv_l, pool_l = _local_mixers(zc_l, zp_l, lw, min(2048, n_lat))
        vt_l, vt_c = v_l.transpose(0, 2, 1), v_c.transpose(0, 2, 1)
        att_l = _attention(q_l, k_c, vt_c, k_l, vt_l, tq_lat, tk)
        x_lat = _mix_out(x_lat, mod_lat, norm_post[l, 1], y5_l, u_l, conv_l, pool_l, att_l, lw, tm_lat)
        if not last:
            conv_c, pool_c = _local_mixers(zc_c, zp_c, lw, tm_ctx)
            att_c = _attention(q_c, k_c, vt_c, None, None, tm_ctx, tk)
            x_ctx = _mix_out(x_ctx, mod_ctx, norm_post[l, 1], y5_c, u_c, conv_c, pool_c, att_c, lw, tm_ctx)
        x_lat = _ffn(x_lat, mod_lat, 6, norm_pre[l, 2], norm_post[l, 2], *ffn_w[1], tm_ffn)
        if not last:
            x_ctx = _ffn(x_ctx, mod_ctx, 6, norm_pre[l, 2], norm_post[l, 2], *ffn_w[1], tm_ctx)
    return x_lat
```

```python
import functools
import math
from typing import NamedTuple

import jax
import jax.numpy as jnp
from jax import lax
from jax.experimental import pallas as pl
from jax.experimental.pallas import tpu as pltpu

F32 = jnp.float32
BF16 = jnp.bfloat16

D_MODEL = 1024
GRID_W = 64
EPS = 1e-6
N_MOD = 9
MACARON_WEIGHT = 0.5
D_FF = 2816
S5_WIDTH = 256
S5_GROUP = 16
S5_GROUPS = 16
S5_STATE = 64
CONV_WIDTH = 256
CONV_K = 31
POOL_WIDTH = 256
POOL_WINDOWS = (2, 4, 8, 16)
POOL_GROUP = 64
MLA_HEADS = 8
MLA_NOPE = 64
MLA_ROPE = 32
MLA_V = 64
MLA_Q_RANK = 256
MLA_KV_RANK = 128
MLA_WIDTH = MLA_HEADS * MLA_V
ROPE_BASE = 10000.0
D_MIX = S5_WIDTH + CONV_WIDTH + POOL_WIDTH + MLA_WIDTH
IN_S5 = 0
IN_CONV = IN_S5 + S5_WIDTH
IN_POOL = IN_CONV + 2 * CONV_WIDTH
IN_CQ = IN_POOL + POOL_WIDTH
IN_CKV = IN_CQ + MLA_Q_RANK
IN_KR = IN_CKV + MLA_KV_RANK
D_IN = IN_KR + MLA_ROPE

LANES = 128
HEAD_PAD = 128
S5_CHUNK = 16
S5_FLAT = S5_CHUNK * S5_GROUP
HALO = 16
D_IN_EXT = IN_KR + 2 * HEAD_PAD
FF_CHUNKS = ((0, 1024), (1024, 2048), (2048, 2816))
VMEM_LIMIT = 56 * 1024 * 1024
ATTN_HEADS = 2
Q_SCALE = (MLA_NOPE + MLA_ROPE) ** -0.5 * math.log2(math.e)


def _cparams(*sem):
    return pltpu.CompilerParams(dimension_semantics=sem, vmem_limit_bytes=VMEM_LIMIT)


class _Tiles(NamedTuple):
    ffn: int
    proj: int
    s5: int
    local: int
    q: int
    k: int


def _tiles(n_lat, n_ctx):
    lat = _Tiles(ffn=min(512, n_lat), proj=min(1024, n_lat), s5=min(1024, n_lat), local=min(2048, n_lat),
                 q=min(1024, n_lat), k=min(1024, n_lat // 2))
    ctx = _Tiles(ffn=min(512, n_ctx), proj=min(512, n_ctx), s5=min(1024, n_ctx), local=min(512, n_ctx),
                 q=min(512, n_ctx), k=lat.k)
    return lat, ctx


class _Param(NamedTuple):
    stacked: jax.Array
    idx: tuple


def _const_spec(p):
    if isinstance(p, _Param):
        n = len(p.idx)
        idx = tuple(p.idx) + (0,) * (p.stacked.ndim - n)
        return pl.BlockSpec((None,) * n + p.stacked.shape[n:], lambda *_: idx, pipeline_mode=pl.Buffered(1))
    nd = len(p)
    return pl.BlockSpec(p, lambda *_: (0,) * nd, pipeline_mode=pl.Buffered(1))


def _rms(x, g):
    return x * lax.rsqrt(jnp.mean(x * x, axis=-1, keepdims=True) + EPS) * g


def _dot(a, b):
    return jnp.dot(a, b, preferred_element_type=F32)


def _mod_kernel(c_ref, w_ref, b_ref, o_ref):
    c = c_ref[...]
    h = (c * jax.nn.sigmoid(c)).astype(BF16)
    o_ref[0] = _dot(h, w_ref[0].astype(BF16)) + b_ref[0]


def _modulation(c_rows, w_ada, b_ada):
    depth = w_ada.shape[0]
    d = D_MODEL
    return pl.pallas_call(
        _mod_kernel,
        grid=(depth, N_MOD),
        in_specs=[
            pl.BlockSpec((8, d), lambda l, j: (0, 0)),
            pl.BlockSpec((1, d, d), lambda l, j: (l, 0, j)),
            pl.BlockSpec((1, 1, d), lambda l, j: (l, 0, j)),
        ],
        out_specs=pl.BlockSpec((1, 8, d), lambda l, j: (l, 0, j)),
        out_shape=jax.ShapeDtypeStruct((depth, 8, N_MOD * d), F32),
        compiler_params=_cparams("arbitrary", "arbitrary"),
        name="adaln_mod",
    )(c_rows, w_ada, b_ada.reshape(depth, 1, N_MOD * d))


def _mod_slices(mod_ref, base):
    d = D_MODEL
    return (mod_ref[0, :, base * d:(base + 1) * d], mod_ref[0, :, (base + 1) * d:(base + 2) * d],
            mod_ref[0, :, (base + 2) * d:(base + 3) * d])


def _ffn_kernel(x_ref, mod_ref, pre_ref, post_ref, wg_ref, wu_ref, wd_ref, o_ref, *, base):
    x = x_ref[0]
    shift, scale, gate = _mod_slices(mod_ref, base)
    h = (_rms(x, pre_ref[...]) * (1.0 + scale) + shift).astype(BF16)
    acc = None
    for lo, hi in FF_CHUNKS:
        g = _dot(h, wg_ref[:, lo:hi])
        u = _dot(h, wu_ref[:, lo:hi])
        a = (g * jax.nn.sigmoid(g) * u).astype(BF16)
        part = _dot(a, wd_ref[lo:hi, :])
        acc = part if acc is None else acc + part
    o_ref[0] = x + MACARON_WEIGHT * gate * _rms(acc, post_ref[...])


def _ffn(x, mod_rows, base, pre_g, post_g, wg, wu, wd, tm):
    b, s, d = x.shape
    per_batch = mod_rows.shape[0] > 1
    return pl.pallas_call(
        functools.partial(_ffn_kernel, base=base),
        grid=(b, s // tm),
        in_specs=[
            pl.BlockSpec((1, tm, d), lambda i, j: (i, j, 0)),
            pl.BlockSpec((1, 1, N_MOD * d), (lambda i, j: (i, 0, 0)) if per_batch else (lambda i, j: (0, 0, 0))),
            _const_spec((1, d)), _const_spec((1, d)),
            _const_spec(wg), _const_spec(wu), _const_spec(wd),
        ],
        out_specs=pl.BlockSpec((1, tm, d), lambda i, j: (i, j, 0)),
        out_shape=jax.ShapeDtypeStruct(x.shape, F32),
        compiler_params=_cparams("arbitrary", "arbitrary"),
        name="ffn_half_step",
    )(x, mod_rows, pre_g.reshape(1, d), post_g.reshape(1, d), wg.stacked, wu.stacked, wd.stacked)


def _inproj_kernel(x_ref, mod_ref, pre_ref, win_ref, qn_ref, kvn_ref, wq_ref, wk_ref, wv_ref, place_ref,
                   cos_ref, sin_ref, u_ref, zc_ref, zp_ref, q_ref, k_ref, v_ref):
    x = x_ref[0]
    shift, scale, _ = _mod_slices(mod_ref, 3)
    h = (_rms(x, pre_ref[...]) * (1.0 + scale) + shift).astype(BF16)
    z = _dot(h, win_ref[...])
    u_ref[0] = z[:, IN_S5:IN_CONV]
    zc_ref[0] = z[:, IN_CONV:IN_POOL]
    zp_ref[0] = z[:, IN_POOL:IN_CQ]
    cos = cos_ref[...]
    sin = sin_ref[...]
    cos_h = jnp.concatenate([cos] * MLA_HEADS, axis=1)
    sin_h = jnp.concatenate([sin] * MLA_HEADS, axis=1)
    cqn = _rms(z[:, IN_CQ:IN_CKV], qn_ref[...]).astype(BF16)
    qq = _dot(cqn, wq_ref[...])
    hw = MLA_HEADS * HEAD_PAD
    q = qq[:, :hw] * cos_h + qq[:, hw:] * sin_h
    q_ref[0] = (q * Q_SCALE).astype(BF16)
    ckvn = _rms(z[:, IN_CKV:IN_KR], kvn_ref[...]).astype(BF16)
    kr = z[:, IN_KR:IN_KR + HEAD_PAD] * cos + z[:, IN_KR + HEAD_PAD:IN_KR + 2 * HEAD_PAD] * sin
    k = _dot(ckvn, wk_ref[...]) + _dot(kr.astype(BF16), place_ref[...])
    k_ref[0] = k.astype(BF16)
    v_ref[0] = _dot(ckvn, wv_ref[...]).astype(BF16)


def _inproj(x, mod_rows, pre_g, lw, cos_t, sin_t, tm):
    b, s, d = x.shape
    per_batch = mod_rows.shape[0] > 1
    hw = MLA_HEADS * HEAD_PAD
    row = lambda w: pl.BlockSpec((1, tm, w), lambda i, j: (i, j, 0))
    params = [lw[name] for name in ("w_in", "q_norm", "kv_norm", "wq", "wk", "wv", "place")]
    return pl.pallas_call(
        _inproj_kernel,
        grid=(b, s // tm),
        in_specs=[
            row(d),
            pl.BlockSpec((1, 1, N_MOD * d), (lambda i, j: (i, 0, 0)) if per_batch else (lambda i, j: (0, 0, 0))),
            _const_spec((1, d)),
            *[_const_spec(p) for p in params],
            pl.BlockSpec((tm, HEAD_PAD), lambda i, j: (j, 0)),
            pl.BlockSpec((tm, HEAD_PAD), lambda i, j: (j, 0)),
        ],
        out_specs=[row(S5_WIDTH), row(2 * CONV_WIDTH), row(POOL_WIDTH), row(hw), row(hw), row(MLA_WIDTH)],
        out_shape=[
            jax.ShapeDtypeStruct((b, s, S5_WIDTH), F32),
            jax.ShapeDtypeStruct((b, s, 2 * CONV_WIDTH), F32),
            jax.ShapeDtypeStruct((b, s, POOL_WIDTH), F32),
            jax.ShapeDtypeStruct((b, s, hw), BF16),
            jax.ShapeDtypeStruct((b, s, hw), BF16),
            jax.ShapeDtypeStruct((b, s, MLA_WIDTH), BF16),
        ],
        compiler_params=_cparams("arbitrary", "arbitrary"),
        name="mixer_in_proj",
    )(x, mod_rows, pre_g.reshape(1, d), *[p.stacked for p in params], cos_t, sin_t)


def _block_transpose(lo, hi):
    n = S5_CHUNK
    half = n // 2
    halves = [[lo[i] for i in range(half)] + [hi[i] for i in range(half)],
              [lo[i + half] for i in range(half)] + [hi[i + half] for i in range(half)]]
    blk = lax.broadcasted_iota(jnp.int32, lo[0].shape, 1) // S5_GROUP
    for d in (4, 2, 1):
        keep = (blk & d) == 0
        for hv in halves:
            for i in range(n):
                if i & d:
                    continue
                x, y = hv[i], hv[i + d]
                hv[i] = jnp.where(keep, x, pltpu.roll(y, d * S5_GROUP, 1))
                hv[i + d] = jnp.where(keep, pltpu.roll(x, LANES - d * S5_GROUP, 1), y)
    return halves


def _strided_rows(ref, phase, period):
    return ref[0, pl.ds(phase, ref.shape[1] // period, stride=period), :]


def _chunk_flat(ua_ref, ub_ref):
    lo, hi = _block_transpose([_strided_rows(ua_ref, t, S5_CHUNK) for t in range(S5_CHUNK)],
                              [_strided_rows(ub_ref, t, S5_CHUNK) for t in range(S5_CHUNK)])
    return [jnp.concatenate([lo[g], hi[g]], axis=1).astype(BF16) for g in range(S5_GROUPS)]


def _s5_state_kernel(ua_ref, ub_ref, m_ref, sf_ref, sb_ref):
    r = ua_ref.shape[1] // S5_CHUNK
    for g, f in enumerate(_chunk_flat(ua_ref, ub_ref)):
        s = _dot(f, m_ref[g])
        sf_ref[0, pl.ds(g, r, stride=S5_GROUPS), :] = s[:, :LANES]
        sb_ref[0, pl.ds(g, r, stride=S5_GROUPS), :] = s[:, LANES:]


def _lane_half(tm, lane_block):
    return pl.BlockSpec((1, tm, LANES), lambda i, j: (i, j, lane_block))


def _s5_chunk_states(u, m_state, tm):
    b, s, _ = u.shape
    out = jax.ShapeDtypeStruct((b, s, LANES), F32)
    return pl.pallas_call(
        _s5_state_kernel,
        grid=(b, s // tm),
        in_specs=[_lane_half(tm, 0), _lane_half(tm, 1), _const_spec(m_state)],
        out_specs=[_lane_half(tm, 0), _lane_half(tm, 0)],
        out_shape=[out, out],
        compiler_params=_cparams("arbitrary", "arbitrary"),
        name="s5_chunk_states",
    )(u, u, m_state.stacked)


def _s5_scan_kernel(scf_ref, scb_ref, sf_ref, sb_ref, a_ref, hcf_ref, hcb_ref, hf_ref, hb_ref, st_ref, wf_ref, wb_ref,
                    hrf_ref, hif_ref, hrb_ref, hib_ref):
    half = LANES // 2
    a_re = a_ref[:, :LANES]
    a_im = a_ref[:, LANES:]
    fwd_lane = lax.broadcasted_iota(jnp.int32, a_re.shape, 1) < half

    def swap(v):
        n, g, w = v.shape
        return pltpu.roll(v.reshape(n * g, w), half, 1).reshape(n, g, w)

    def run(in_f, in_b, out_f, out_b, carry):
        n = in_f.shape[1]
        wf_ref[0:n] = swap(in_f[0])
        wb_ref[0:n] = swap(in_b[0])

        def body(i, c):
            h_re, h_im = c
            r = n - 1 - i
            hrf_ref[i] = h_re
            hif_ref[i] = h_im
            hrb_ref[r] = h_re
            hib_ref[r] = h_im
            s_re = jnp.where(fwd_lane, in_f[0, i], wb_ref[r])
            s_im = jnp.where(fwd_lane, wf_ref[i], in_b[0, r])
            return (a_re * h_re - a_im * h_im + s_re, a_re * h_im + a_im * h_re + s_im)

        carry = lax.fori_loop(0, n, body, carry)
        out_f[0] = jnp.where(fwd_lane, hrf_ref[0:n], swap(hif_ref[0:n]))
        out_b[0] = jnp.where(fwd_lane, swap(hrb_ref[0:n]), hib_ref[0:n])
        return carry

    @pl.when(pl.program_id(1) == 0)
    def _():
        zero = jnp.zeros(a_re.shape, F32)
        c = run(scf_ref, scb_ref, hcf_ref, hcb_ref, (zero, zero))
        st_ref[0] = c[0]
        st_ref[1] = c[1]

    c = run(sf_ref, sb_ref, hf_ref, hb_ref, (st_ref[0], st_ref[1]))
    st_ref[0] = c[0]
    st_ref[1] = c[1]


def _s5_scan(sc_f, sc_b, sl_f, sl_b, a16):
    b, nc, g, _ = sc_f.shape
    c = sl_f.shape[1]
    cb = min(128, c)
    nb = c // cb
    ctx = pl.BlockSpec((1, nc, g, LANES), lambda i, j: (i, 0, 0, 0))
    up = pl.BlockSpec((1, cb, g, LANES), lambda i, j: (i, j, 0, 0))
    down = pl.BlockSpec((1, cb, g, LANES), lambda i, j: (i, nb - 1 - j, 0, 0))
    return pl.pallas_call(
        _s5_scan_kernel,
        grid=(b, nb),
        in_specs=[ctx, ctx, up, down, pl.BlockSpec((g, 2 * LANES), lambda i, j: (0, 0))],
        out_specs=[ctx, ctx, up, down],
        out_shape=[jax.ShapeDtypeStruct(sc_f.shape, F32), jax.ShapeDtypeStruct(sc_f.shape, F32),
                   jax.ShapeDtypeStruct(sl_f.shape, F32), jax.ShapeDtypeStruct(sl_f.shape, F32)],
        scratch_shapes=[pltpu.VMEM((2, g, LANES), F32)] + [pltpu.VMEM((max(cb, nc), g, LANES), F32)] * 6,
        compiler_params=_cparams("arbitrary", "arbitrary"),
        name="s5_chunk_scan",
    )(sc_f, sc_b, sl_f, sl_b, a16)


def _s5_out_kernel(ua_ref, ub_ref, hf_ref, hb_ref, mi_ref, mo_ref, ya_ref, yb_ref):
    lo, hi = [], []
    for g, f in enumerate(_chunk_flat(ua_ref, ub_ref)):
        hf = _strided_rows(hf_ref, g, S5_GROUPS).astype(BF16)
        hb = _strided_rows(hb_ref, g, S5_GROUPS).astype(BF16)
        y = _dot(f, mi_ref[g]) + _dot(hf, mo_ref[g, :LANES, :]) + _dot(hb, mo_ref[g, LANES:, :])
        lo.append(y[:, :LANES])
        hi.append(y[:, LANES:])
    lo, hi = _block_transpose(lo, hi)
    r = ua_ref.shape[1] // S5_CHUNK
    for t in range(S5_CHUNK):
        ya_ref[0, pl.ds(t, r, stride=S5_CHUNK), :] = lo[t]
        yb_ref[0, pl.ds(t, r, stride=S5_CHUNK), :] = hi[t]


def _s5_outputs(u, h_f, h_b, m_intra, m_out, tm):
    b, s, _ = u.shape
    out = jax.ShapeDtypeStruct((b, s, LANES), F32)
    return pl.pallas_call(
        _s5_out_kernel,
        grid=(b, s // tm),
        in_specs=[_lane_half(tm, 0), _lane_half(tm, 1), _lane_half(tm, 0), _lane_half(tm, 0),
                  _const_spec(m_intra), _const_spec(m_out)],
        out_specs=[_lane_half(tm, 0), _lane_half(tm, 0)],
        out_shape=[out, out],
        compiler_params=_cparams("arbitrary", "arbitrary"),
        name="s5_chunk_outputs",
    )(u, u, h_f, h_b, m_intra.stacked, m_out.stacked)


def _s5_matrices(lam_re, lam_im, log_dt, b_re, b_im, c_re, c_im):
    hp = lax.Precision.HIGHEST
    t = S5_CHUNK
    g, p, h = S5_GROUPS, S5_STATE, S5_GROUP
    lam = lax.complex(jnp.minimum(lam_re, -1e-4), lam_im)
    dt = jnp.exp(log_dt)[..., None]
    steps = jnp.arange(t + 1, dtype=F32)[:, None, None, None]
    apow = jnp.exp(lam[None] * dt[None] * steps)
    bb = ((apow[1] - 1.0) / lam)[..., None] * lax.complex(b_re, b_im)
    cc = lax.complex(c_re, c_im)
    taps = jnp.einsum('dghp,jdgp,dgpk->djghk', cc, apow[:t], bb, precision=hp).real
    by_dist = jnp.concatenate([taps[1, :0:-1], (taps[0, 0] + taps[1, 0])[None], taps[0, 1:]], axis=0)
    by_dist = by_dist.transpose(1, 3, 0, 2).reshape(g, h, (2 * t - 1) * h)
    m_intra = jnp.stack([by_dist[:, :, (t - 1 - s) * h:(2 * t - 1 - s) * h] for s in range(t)], axis=1)
    m_intra = m_intra.reshape(g, t * h, t * h).astype(BF16)
    wf = apow[:t][::-1, 0][:, :, :, None] * bb[0][None]
    wb = apow[:t, 1][:, :, :, None] * bb[1][None]
    to_rows = lambda w: w.transpose(1, 0, 3, 2).reshape(g, t * h, p)
    m_state = jnp.concatenate([to_rows(wf.real), to_rows(wf.imag), to_rows(wb.real), to_rows(wb.imag)], axis=-1)
    of = cc[0][None] * apow[1:, 0][:, :, None, :]
    ob = cc[1][None] * apow[1:, 1][::-1][:, :, None, :]
    to_cols = lambda w: w.transpose(1, 3, 0, 2).reshape(g, p, t * h)
    m_out = jnp.concatenate([to_cols(of.real), -to_cols(of.imag), to_cols(ob.real), -to_cols(ob.imag)], axis=1)
    a_t = apow[t]
    a16 = jnp.concatenate([a_t[0].real, a_t[1].real, a_t[0].imag, a_t[1].imag], axis=-1)
    return m_intra, m_state.astype(BF16), m_out.astype(BF16), a16


def _s5_mixer(u_ctx, u_lat, mats, tm_c, tm_l):
    m_intra, m_state, m_out, a16 = mats
    b = u_ctx.shape[0]
    g = S5_GROUPS
    chunks = lambda x: x.reshape(b, x.shape[1] // g, g, LANES)
    rows = lambda x: x.reshape(b, x.shape[1] * g, LANES)
    sc_f, sc_b = _s5_chunk_states(u_ctx, m_state, tm_c)
    sl_f, sl_b = _s5_chunk_states(u_lat, m_state, tm_l)
    hc_f, hc_b, hl_f, hl_b = _s5_scan(chunks(sc_f), chunks(sc_b), chunks(sl_f), chunks(sl_b), a16)
    y_ctx = _s5_outputs(u_ctx, rows(hc_f), rows(hc_b), m_intra, m_out, tm_c)
    y_lat = _s5_outputs(u_lat, rows(hl_f), rows(hl_b), m_intra, m_out, tm_l)
    return y_ctx, y_lat


def _banded_time_filter(xa_ref, xb_ref, m_ref, ya_ref, yb_ref, split):
    r = ya_ref.shape[0] // S5_CHUNK
    flo, fhi = _block_transpose([xa_ref[pl.ds(t, r + 2, stride=S5_CHUNK), :] for t in range(S5_CHUNK)],
                                [xb_ref[pl.ds(t, r + 2, stride=S5_CHUNK), :] for t in range(S5_CHUNK)])
    out_lo, out_hi = [], []
    for g in range(len(flo)):
        f = jnp.concatenate([flo[g], fhi[g]], axis=1)
        y = None
        for c in range(3):
            fc = f[c:c + r]
            top = fc.astype(BF16)
            parts = [top, (fc - top.astype(F32)).astype(BF16)] if split else [top]
            for p in parts:
                term = _dot(p, m_ref[3 * g + c])
                y = term if y is None else y + term
        out_lo.append(y[:, :LANES])
        out_hi.append(y[:, LANES:])
    out_lo, out_hi = _block_transpose(out_lo, out_hi)
    for t in range(S5_CHUNK):
        ya_ref[pl.ds(t, r, stride=S5_CHUNK), :] = out_lo[t]
        yb_ref[pl.ds(t, r, stride=S5_CHUNK), :] = out_hi[t]
    return jnp.concatenate([ya_ref[...], yb_ref[...]], axis=1)


def _local_kernel(zc_ref, zcp_ref, zcn_ref, zp_ref, zpp_ref, zpn_ref, cm_ref, cb_ref, lg_ref, lb_ref, pw_ref, ps_ref,
                  conv_ref, pool_ref, xa_ref, xb_ref, ya_ref, yb_ref, pe_ref, *, seq_len):
    tm = zc_ref.shape[1]
    j = pl.program_id(1)
    first = j == 0
    last = j == pl.num_programs(1) - 1

    def fill(prev, main, nxt):
        for rows, val in ((slice(0, HALO), jnp.where(first, 0.0, prev)), (slice(HALO, HALO + tm), main),
                          (slice(HALO + tm, HALO + tm + HALO), jnp.where(last, 0.0, nxt))):
            xa_ref[rows] = val[:, :LANES]
            xb_ref[rows] = val[:, LANES:]

    def glu(z):
        return z[:, :CONV_WIDTH] * jax.nn.sigmoid(z[:, CONV_WIDTH:])

    fill(glu(zcp_ref[0]), glu(zc_ref[0]), glu(zcn_ref[0]))
    acc = _banded_time_filter(xa_ref, xb_ref, cm_ref, ya_ref, yb_ref, split=False) + cb_ref[...]
    mu = jnp.mean(acc, axis=-1, keepdims=True)
    xc = acc - mu
    var = jnp.mean(xc * xc, axis=-1, keepdims=True)
    y = xc * lax.rsqrt(var + EPS) * lg_ref[...] + lb_ref[...]
    conv_ref[0] = (y * jax.nn.sigmoid(y)).astype(BF16)

    pe_ref[0:HALO] = jnp.where(first, 0.0, zpp_ref[0])
    pe_ref[HALO:HALO + tm] = zp_ref[0]
    pe_ref[HALO + tm:] = jnp.where(last, 0.0, zpn_ref[0])

    def tap(dlt):
        return pe_ref[pl.ds(HALO + dlt, tm), :]

    z = tap(0)
    sums = {}
    run = z + tap(1)
    sums[2] = run
    run = run + tap(-1) + tap(2)
    sums[4] = run
    run = run + tap(-3) + tap(-2) + tap(3) + tap(4)
    sums[8] = run
    for dlt in (-7, -6, -5, -4, 5, 6, 7, 8):
        run = run + tap(dlt)
    sums[16] = run
    pos = j * tm + lax.broadcasted_iota(jnp.int32, (tm, 1), 0)
    lane = lax.broadcasted_iota(jnp.int32, (tm, POOL_WIDTH), 1)
    mean = None
    for gi, w in enumerate(POOL_WINDOWS):
        lo = jnp.maximum(pos - (w - 1) // 2, 0)
        hi = jnp.minimum(pos + w // 2, seq_len - 1)
        m = sums[w] / (hi - lo + 1).astype(F32)
        mean = m if mean is None else jnp.where(lane >= gi * POOL_GROUP, m, mean)
    d = (mean - z).astype(BF16)
    pool_ref[0] = (_dot(d, pw_ref[...]) * ps_ref[...]).astype(BF16)


def _local_mixers(zc, zp, lw, tm):
    b, s, _ = zc.shape
    nh = tm // HALO
    nblk = s // HALO
    main = lambda w: pl.BlockSpec((1, tm, w), lambda i, j: (i, j, 0))
    prev = lambda w: pl.BlockSpec((1, HALO, w), lambda i, j: (i, jnp.maximum(j * nh - 1, 0), 0))
    nxt = lambda w: pl.BlockSpec((1, HALO, w), lambda i, j: (i, jnp.minimum((j + 1) * nh, nblk - 1), 0))
    cw2, pw = 2 * CONV_WIDTH, POOL_WIDTH
    params = [lw[name] for name in ("conv_m", "conv_b", "conv_ln_g", "conv_ln_b", "pool_w", "pool_scale")]
    return pl.pallas_call(
        functools.partial(_local_kernel, seq_len=s),
        grid=(b, s // tm),
        in_specs=[main(cw2), prev(cw2), nxt(cw2), main(pw), prev(pw), nxt(pw), *[_const_spec(p) for p in params]],
        out_specs=[main(CONV_WIDTH), main(pw)],
        out_shape=[jax.ShapeDtypeStruct((b, s, CONV_WIDTH), BF16), jax.ShapeDtypeStruct((b, s, pw), BF16)],
        scratch_shapes=[pltpu.VMEM((tm + 2 * HALO, LANES), F32), pltpu.VMEM((tm + 2 * HALO, LANES), F32),
                        pltpu.VMEM((tm, LANES), F32), pltpu.VMEM((tm, LANES), F32),
                        pltpu.VMEM((tm + 2 * HALO, pw), F32)],
        compiler_params=_cparams("arbitrary", "arbitrary"),
        name="conv_pool",
    )(zc, zc, zc, zp, zp, zp, *[p.stacked for p in params])


def _attn_kernel(*refs, tk, has_lat):
    if has_lat:
        q_ref, kc_ref, vtc_ref, kl_ref, vtl_ref, o_ref, s_ref = refs
    else:
        q_ref, kc_ref, vtc_ref, o_ref = refs
    tq = q_ref.shape[1]
    heads = range(ATTN_HEADS)
    qs = [q_ref[0, :, h * HEAD_PAD:(h + 1) * HEAD_PAD] for h in heads]

    def scores(h, k):
        return lax.dot_general(k, qs[h], (((1,), (1,)), ((), ())), preferred_element_type=F32)

    def consume(s, mt, vt, state):
        m, l, acc = state
        m_new = jnp.maximum(m, mt)
        p = jnp.exp2(s - m_new)
        alpha = jnp.exp2(m - m_new)
        l = alpha * l + jnp.sum(p, axis=0, keepdims=True)
        acc = alpha * acc + _dot(vt, p.astype(BF16))
        return m_new, l, acc

    init = (jnp.full((1, tq), -1e30, F32), jnp.zeros((1, tq), F32), jnp.zeros((MLA_V, tq), F32))
    state = []
    for h in heads:
        s = scores(h, kc_ref[0, :, h * HEAD_PAD:(h + 1) * HEAD_PAD])
        state.append(consume(s, jnp.max(s, axis=0, keepdims=True), vtc_ref[0, h * MLA_V:(h + 1) * MLA_V, :], init))

    if has_lat:
        n_tiles = kl_ref.shape[1] // tk

        def produce(h, slot, i):
            off = pl.multiple_of(i * tk, tk)
            s = scores(h, kl_ref[0, pl.ds(off, tk), h * HEAD_PAD:(h + 1) * HEAD_PAD])
            s_ref[h, slot] = s
            return jnp.max(s, axis=0, keepdims=True)

        def use(h, slot, i, mt, st):
            off = pl.multiple_of(i * tk, tk)
            return consume(s_ref[h, slot], mt, vtl_ref[0, h * MLA_V:(h + 1) * MLA_V, pl.ds(off, tk)], st)

        def pair(i0, st, mt0, more):
            mt1 = [produce(h, 1, i0 + 1) for h in heads]
            st = [use(h, 0, i0, mt0[h], st[h]) for h in heads]
            if more:
                mt0 = [produce(h, 0, i0 + 2) for h in heads]
            st = [use(h, 1, i0 + 1, mt1[h], st[h]) for h in heads]
            return st, mt0

        mt0 = [produce(h, 0, 0) for h in heads]
        state, mt0 = lax.fori_loop(0, n_tiles // 2 - 1, lambda j, c: pair(2 * j, c[0], c[1], True), (state, mt0))
        state, _ = pair(n_tiles - 2, state, mt0, False)
    out_t = jnp.concatenate([acc / l for _, l, acc in state], axis=0)
    o_ref[0] = out_t.T.astype(BF16)


def _attention(q, k_ctx, vt_ctx, k_lat, vt_lat, tq, tk):
    b, lq, _ = q.shape
    lc = k_ctx.shape[1]
    has_lat = k_lat is not None
    nh = ATTN_HEADS
    in_specs = [pl.BlockSpec((1, tq, nh * HEAD_PAD), lambda i, p, j: (i, j, p)),
                pl.BlockSpec((1, lc, nh * HEAD_PAD), lambda i, p, j: (i, 0, p)),
                pl.BlockSpec((1, nh * MLA_V, lc), lambda i, p, j: (i, p, 0))]
    args = [q, k_ctx, vt_ctx]
    scratch = []
    if has_lat:
        ll = k_lat.shape[1]
        assert ll % (2 * tk) == 0
        in_specs += [pl.BlockSpec((1, ll, nh * HEAD_PAD), lambda i, p, j: (i, 0, p)),
                     pl.BlockSpec((1, nh * MLA_V, ll), lambda i, p, j: (i, p, 0))]
        args += [k_lat, vt_lat]
        scratch = [pltpu.VMEM((nh, 2, tk, tq), F32)]
    return pl.pallas_call(
        functools.partial(_attn_kernel, tk=tk, has_lat=has_lat),
        grid=(b, MLA_HEADS // nh, lq // tq),
        in_specs=in_specs,
        out_specs=pl.BlockSpec((1, tq, nh * MLA_V), lambda i, p, j: (i, j, p)),
        out_shape=jax.ShapeDtypeStruct((b, lq, MLA_WIDTH), BF16),
        scratch_shapes=scratch,
        compiler_params=_cparams("arbitrary", "arbitrary", "arbitrary"),
        name="mla_attention",
    )(*args)


def _mixout_kernel(x_ref, mod_ref, post_ref, y5a_ref, y5b_ref, u_ref, d_ref, wglu_ref, bglu_ref, conv_ref, pool_ref,
                   att_ref, wout_ref, o_ref):
    x = x_ref[0]
    _, _, gate = _mod_slices(mod_ref, 3)
    y5 = jax.nn.gelu(jnp.concatenate([y5a_ref[0], y5b_ref[0]], axis=1) + d_ref[...] * u_ref[0])
    s5 = y5 * jax.nn.sigmoid(_dot(y5.astype(BF16), wglu_ref[...]) + bglu_ref[...])
    o1, o2, o3 = S5_WIDTH, S5_WIDTH + CONV_WIDTH, S5_WIDTH + CONV_WIDTH + POOL_WIDTH
    y = (_dot(s5.astype(BF16), wout_ref[0:o1, :]) + _dot(conv_ref[0], wout_ref[o1:o2, :])
         + _dot(pool_ref[0], wout_ref[o2:o3, :]) + _dot(att_ref[0], wout_ref[o3:, :]))
    o_ref[0] = x + gate * _rms(y, post_ref[...])


def _mix_out(x, mod_rows, post_g, y5, u, conv, pool, att, lw, tm):
    b, s, d = x.shape
    per_batch = mod_rows.shape[0] > 1
    row = lambda w: pl.BlockSpec((1, tm, w), lambda i, j: (i, j, 0))
    return pl.pallas_call(
        _mixout_kernel,
        grid=(b, s // tm),
        in_specs=[
            row(d),
            pl.BlockSpec((1, 1, N_MOD * d), (lambda i, j: (i, 0, 0)) if per_batch else (lambda i, j: (0, 0, 0))),
            _const_spec((1, d)),
            row(LANES), row(LANES), row(S5_WIDTH), _const_spec(lw["s5_d"]), _const_spec(lw["s5_w_glu"]),
            _const_spec(lw["s5_b_glu"]),
            row(CONV_WIDTH), row(POOL_WIDTH), row(MLA_WIDTH), _const_spec(lw["w_out"]),
        ],
        out_specs=row(d),
        out_shape=jax.ShapeDtypeStruct(x.shape, F32),
        compiler_params=_cparams("arbitrary", "arbitrary"),
        name="mixer_out_proj",
    )(x, mod_rows, post_g.reshape(1, d), y5[0], y5[1], u, lw["s5_d"].stacked, lw["s5_w_glu"].stacked,
      lw["s5_b_glu"].stacked, conv, pool, att, lw["w_out"].stacked)


def _rope_partner(w):
    q = MLA_ROPE // 4
    return jnp.concatenate([-w[..., q:2 * q], w[..., 0:q], -w[..., 3 * q:4 * q], w[..., 2 * q:3 * q]], axis=-1)


def _rope_tables(n_lat):
    rows = n_lat // GRID_W
    axis_dim = MLA_ROPE // 2
    inv = ROPE_BASE ** (-jnp.arange(0, axis_dim, 2, dtype=F32) / axis_dim)
    ang_r = jnp.arange(rows, dtype=F32)[:, None] * inv
    ang_c = jnp.arange(GRID_W, dtype=F32)[:, None] * inv
    nq = MLA_ROPE // 4

    def table(fn, fill):
        fr, fc = fn(ang_r), fn(ang_c)
        pad = HEAD_PAD - MLA_NOPE - MLA_ROPE
        by_row = jnp.concatenate([jnp.full((rows, MLA_NOPE), fill, F32), fr, fr, jnp.zeros((rows, 2 * nq), F32),
                                  jnp.full((rows, pad), fill, F32)], axis=1)
        by_col = jnp.concatenate([jnp.zeros((GRID_W, MLA_NOPE + 2 * nq), F32), fc, fc,
                                  jnp.zeros((GRID_W, pad), F32)], axis=1)
        return (by_row[:, None, :] + by_col[None, :, :]).reshape(n_lat, HEAD_PAD)

    return table(jnp.cos, 1.0), table(jnp.sin, 0.0)


def _pad_head(nope, rope):
    r = nope.shape[0]
    z = jnp.zeros((r, MLA_HEADS, HEAD_PAD - MLA_NOPE - MLA_ROPE), nope.dtype)
    return jnp.concatenate([nope, rope, z], axis=-1).reshape(r, MLA_HEADS * HEAD_PAD)


def _band_matrices(w, center):
    n, h = S5_CHUNK, S5_GROUP
    n_taps, width = w.shape
    g = width // h
    k_max = (n - 1) + n + center
    left = k_max - (n_taps - 1)
    line = jnp.pad(w[::-1], ((left, 4 * n - 1 - left - n_taps), (0, 0)))
    spread = line.T.reshape(g, h, -1)[..., None] * jnp.eye(h, dtype=F32)[None, :, None, :]
    spread = spread.reshape(g, h, -1)
    rows = []
    for c in range(3):
        for t_in in range(n):
            off = k_max - (t_in + n * (c - 1) + center)
            rows.append(spread[:, :, off * h:(off + n) * h])
    m = jnp.stack(rows, axis=1)
    return m.reshape(g * 3, n * h, n * h).astype(BF16)


def _prep_layer(w_in, w_out, s5_d, s5_w_glu, s5_b_glu, conv_w, conv_b, conv_ln_g, conv_ln_b, pool_w, pool_scale,
                mla_q_norm, mla_w_uq, mla_kv_norm, mla_w_ukv):
    d = D_MODEL
    wi = w_in
    kr = wi[:, IN_KR:D_IN]
    zpad_l = jnp.zeros((d, MLA_NOPE), F32)
    zpad_r = jnp.zeros((d, HEAD_PAD - MLA_NOPE - MLA_ROPE), F32)
    w_in_ext = jnp.concatenate([wi[:, :IN_KR], zpad_l, kr, zpad_r, zpad_l, _rope_partner(kr), zpad_r], axis=1)
    uq = mla_w_uq.reshape(MLA_Q_RANK, MLA_HEADS, MLA_NOPE + MLA_ROPE)
    q_nope, q_rope = uq[..., :MLA_NOPE], uq[..., MLA_NOPE:]
    wq = jnp.concatenate([_pad_head(q_nope, q_rope), _pad_head(0 * q_nope, _rope_partner(q_rope))], axis=1)
    ukv = mla_w_ukv.reshape(MLA_KV_RANK, MLA_HEADS, MLA_NOPE + MLA_V)
    wk = _pad_head(ukv[..., :MLA_NOPE], jnp.zeros((MLA_KV_RANK, MLA_HEADS, MLA_ROPE), F32))
    wv = ukv[..., MLA_NOPE:].reshape(MLA_KV_RANK, MLA_WIDTH)
    eye = jnp.eye(HEAD_PAD, dtype=F32) * ((jnp.arange(HEAD_PAD) >= MLA_NOPE)
                                          & (jnp.arange(HEAD_PAD) < MLA_NOPE + MLA_ROPE))[:, None]
    place = jnp.tile(eye, (1, MLA_HEADS))
    pool_bd = jnp.zeros((POOL_WIDTH, POOL_WIDTH), F32)
    for gi in range(len(POOL_WINDOWS)):
        sl = slice(gi * POOL_GROUP, (gi + 1) * POOL_GROUP)
        pool_bd = pool_bd.at[sl, sl].set(pool_w[gi])
    return {
        "w_in": w_in_ext.astype(BF16), "w_out": w_out.astype(BF16),
        "q_norm": mla_q_norm.reshape(1, -1), "kv_norm": mla_kv_norm.reshape(1, -1),
        "wq": wq.astype(BF16), "wk": wk.astype(BF16), "wv": wv.astype(BF16), "place": place.astype(BF16),
        "s5_d": s5_d.reshape(1, -1), "s5_w_glu": s5_w_glu.astype(BF16), "s5_b_glu": s5_b_glu.reshape(1, -1),
        "conv_m": _band_matrices(conv_w, CONV_K // 2),
        "conv_b": conv_b.reshape(1, -1), "conv_ln_g": conv_ln_g.reshape(1, -1),
        "conv_ln_b": conv_ln_b.reshape(1, -1),
        "pool_w": pool_bd.astype(BF16), "pool_scale": pool_scale.reshape(1, -1),
    }


def kernel(x, c, ctx, c_ctx, w_ada, b_ada, norm_pre, norm_post, ffn_w_gate, ffn_w_up, ffn_w_down, w_in, w_out,
           s5_lam_re, s5_lam_im, s5_log_dt, s5_b_re, s5_b_im, s5_c_re, s5_c_im, s5_d, s5_w_glu, s5_b_glu, conv_w,
           conv_b, conv_ln_g, conv_ln_b, pool_w, pool_scale, mla_q_norm, mla_w_uq, mla_kv_norm, mla_w_ukv):
    batch, n_lat, d = x.shape
    n_ctx = ctx.shape[1]
    depth = w_ada.shape[0]
    assert d == D_MODEL and batch + 1 <= 8
    tl, tc = _tiles(n_lat, n_ctx)
    assert n_lat % tl.local == 0 and n_lat % (2 * tl.k) == 0 and n_ctx % tc.local == 0 and n_lat % GRID_W == 0

    c_rows = jnp.zeros((8, d), F32).at[:batch].set(c).at[batch].set(c_ctx)
    mod = _modulation(c_rows, w_ada, b_ada)
    cos_lat, sin_lat = _rope_tables(n_lat)
    cos_ctx = jnp.ones((n_ctx, HEAD_PAD), F32)
    sin_ctx = jnp.zeros((n_ctx, HEAD_PAD), F32)
    wg = ffn_w_gate.astype(BF16)
    wu = ffn_w_up.astype(BF16)
    wd = ffn_w_down.astype(BF16)
    layer_params = jax.vmap(_prep_layer)(w_in, w_out, s5_d, s5_w_glu, s5_b_glu, conv_w, conv_b, conv_ln_g, conv_ln_b,
                                         pool_w, pool_scale, mla_q_norm, mla_w_uq, mla_kv_norm, mla_w_ukv)
    s5_params = jax.vmap(_s5_matrices)(s5_lam_re, s5_lam_im, s5_log_dt, s5_b_re, s5_b_im, s5_c_re, s5_c_im)

    x_lat, x_ctx = x, ctx
    for l in range(depth):
        last = l == depth - 1
        mod_lat = mod[l, :batch].reshape(batch, 1, N_MOD * d)
        mod_ctx = mod[l, batch].reshape(1, 1, N_MOD * d)
        lw = {name: _Param(v, (l,)) for name, v in layer_params.items()}
        s5_mats = tuple(_Param(m, (l,)) for m in s5_params[:3]) + (s5_params[3][l],)
        ffn_w = [[_Param(w, (l, half)) for w in (wg, wu, wd)] for half in range(2)]
        x_lat = _ffn(x_lat, mod_lat, 0, norm_pre[l, 0], norm_post[l, 0], *ffn_w[0], tl.ffn)
        x_ctx = _ffn(x_ctx, mod_ctx, 0, norm_pre[l, 0], norm_post[l, 0], *ffn_w[0], tc.ffn)
        u_l, zc_l, zp_l, q_l, k_l, v_l = _inproj(x_lat, mod_lat, norm_pre[l, 1], lw, cos_lat, sin_lat, tl.proj)
        u_c, zc_c, zp_c, q_c, k_c, v_c = _inproj(x_ctx, mod_ctx, norm_pre[l, 1], lw, cos_ctx, sin_ctx, tc.proj)
        y5_c, y5_l = _s5_mixer(u_c, u_l, s5_mats, tc.s5, tl.s5)
        conv_l, pool_l = _local_mixers(zc_l, zp_l, lw, tl.local)
        vt_l, vt_c = v_l.transpose(0, 2, 1), v_c.transpose(0, 2, 1)
        att_l = _attention(q_l, k_c, vt_c, k_l, vt_l, tl.q, tl.k)
        x_lat = _mix_out(x_lat, mod_lat, norm_post[l, 1], y5_l, u_l, conv_l, pool_l, att_l, lw, tl.proj)
        if not last:
            conv_c, pool_c = _local_mixers(zc_c, zp_c, lw, tc.local)
            att_c = _attention(q_c, k_c, vt_c, None, None, tc.q, tc.k)
            x_ctx = _mix_out(x_ctx, mod_ctx, norm_post[l, 1], y5_c, u_c, conv_c, pool_c, att_c, lw, tc.proj)
        x_lat = _ffn(x_lat, mod_lat, 6, norm_pre[l, 2], norm_post[l, 2], *ffn_w[1], tl.ffn)
        if not last:
            x_ctx = _ffn(x_ctx, mod_ctx, 6, norm_pre[l, 2], norm_post[l, 2], *ffn_w[1], tc.ffn)
    return x_lat
```
